```python
import math
import jax, jax.numpy as jnp
from jax import lax
import numpy as np


D_MODEL = 1024
BATCH = 2
SEQ = 16384
DEPTH = 4

N_A_LAYERS = DEPTH // 2
N_B_LAYERS = DEPTH - N_A_LAYERS
D_FF = 4 * D_MODEL
NORM_EPS = 1e-6

M_HEADS = 4
M_V_DIM = D_MODEL // M_HEADS
M_QK_DIM = M_V_DIM // 2
M_CHUNK = 64
GATE_SOFTCAP = 15.0
M_QK_COLS = M_HEADS * M_QK_DIM
M_PROJ = 2 * M_QK_COLS + 2 * D_MODEL + 2 * M_HEADS

A_HEADS = 8
A_HEAD_DIM = D_MODEL // (2 * A_HEADS)
A_V_DIM = 2 * A_HEAD_DIM
Q_BLOCK = 128
ROPE_THETA = 10000.0

kernel_name = 'yoco_mlstm_diffattn_hybrid'


def rmsnorm(x, g):
    xf = x.astype(jnp.float32)
    y = xf * lax.rsqrt(jnp.mean(xf * xf, axis=-1, keepdims=True) + NORM_EPS)
    return (y * g.astype(jnp.float32)).astype(x.dtype)


def softcap(x, cap):
    return cap * jnp.tanh(x / cap)


def rope_tables(positions, dim):
    inv = 1.0 / (ROPE_THETA ** (jnp.arange(0, dim, 2, dtype=jnp.float32) / dim))
    ang = positions.astype(jnp.float32)[..., None] * inv
    return jnp.cos(ang), jnp.sin(ang)


def apply_rope(t, cos, sin):
    t1, t2 = jnp.split(t.astype(jnp.float32), 2, axis=-1)
    c = cos[:, :, None, :]
    s = sin[:, :, None, :]
    return jnp.concatenate([t1 * c - t2 * s, t2 * c + t1 * s], axis=-1).astype(t.dtype)


def sqrelu_mlp(h, w_up, w_down):
    u = jax.nn.relu(h @ w_up)
    return (u * u) @ w_down


def mlstm_mixer(h, w_in, b_gates, g_head, w_out):
    B, S, _ = h.shape
    L = M_CHUNK
    nc = S // L
    proj = h @ w_in
    cuts = [M_QK_COLS, 2 * M_QK_COLS, 2 * M_QK_COLS + D_MODEL, 2 * M_QK_COLS + 2 * D_MODEL]
    q, k, v, o, gts = jnp.split(proj, cuts, axis=-1)
    gts = softcap((gts + b_gates).astype(jnp.float32), GATE_SOFTCAP)
    log_i = gts[..., :M_HEADS]
    log_f = jax.nn.log_sigmoid(gts[..., M_HEADS:])

    def to_chunks(t, dh):
        return t.astype(jnp.float32).reshape(B, nc, L, M_HEADS, dh).transpose(1, 0, 3, 2, 4)

    def gate_chunks(t):
        return t.reshape(B, nc, L, M_HEADS).transpose(1, 0, 3, 2)

    qc = to_chunks(q, M_QK_DIM)
    kc = to_chunks(k, M_QK_DIM) * (M_QK_DIM ** -0.5)
    vc = to_chunks(v, M_V_DIM)
    ic = gate_chunks(log_i)
    fc = gate_chunks(log_f)
    causal = jnp.tril(jnp.ones((L, L), dtype=bool))

    def step(carry, xs):
        C, n, m = carry
        q_, k_, v_, li, lf = xs
        b = jnp.cumsum(lf, axis=-1)
        dmat = b[..., :, None] - b[..., None, :] + li[..., None, :]
        dmat = jnp.where(causal, dmat, -jnp.inf)
        inter = b + m[..., None]
        m_row = jnp.maximum(inter, jnp.max(dmat, axis=-1))
        w_intra = jnp.exp(dmat - m_row[..., None])
        s_inter = jnp.exp(inter - m_row)
        s_qk = jnp.einsum('bhjd,bhsd->bhjs', q_, k_) * w_intra
        num = (s_inter[..., None] * jnp.einsum('bhjd,bhde->bhje', q_, C)
               + jnp.einsum('bhjs,bhse->bhje', s_qk, v_))
        den = s_inter * jnp.einsum('bhjd,bhd->bhj', q_, n) + jnp.sum(s_qk, axis=-1)
        h_out = num / jnp.maximum(jnp.abs(den), jnp.exp(-m_row))[..., None]
        b_last = b[..., -1]
        g_s = b_last[..., None] - b + li
        m_new = jnp.maximum(b_last + m, jnp.max(g_s, axis=-1))
        decay = jnp.exp(b_last + m - m_new)
        w_s = jnp.exp(g_s - m_new[..., None])
        C_new = decay[..., None, None] * C + jnp.einsum('bhs,bhsd,bhse->bhde', w_s, k_, v_)
        n_new = decay[..., None] * n + jnp.einsum('bhs,bhsd->bhd', w_s, k_)
        return (C_new, n_new, m_new), h_out

    init = (jnp.zeros((B, M_HEADS, M_QK_DIM, M_V_DIM), jnp.float32),
            jnp.zeros((B, M_HEADS, M_QK_DIM), jnp.float32),
            jnp.zeros((B, M_HEADS), jnp.float32))
    _, hs = lax.scan(step, init, (qc, kc, vc, ic, fc))
    hs = hs.transpose(1, 0, 3, 2, 4).reshape(B, S, M_HEADS, M_V_DIM)
    hs = rmsnorm(hs, g_head.reshape(M_HEADS, M_V_DIM))
    og = jax.nn.sigmoid(o.astype(jnp.float32)).reshape(B, S, M_HEADS, M_V_DIM)
    y = (og * hs).reshape(B, S, D_MODEL).astype(h.dtype)
    return y @ w_out


def shared_kv(x, kv_norm_g, w_kv, cos, sin):
    B, S, _ = x.shape
    kv = rmsnorm(x, kv_norm_g) @ w_kv
    k = kv[..., :D_MODEL].reshape(B, S, 2 * A_HEADS, A_HEAD_DIM)
    k = apply_rope(k, cos, sin).reshape(B, S, A_HEADS, 2, A_HEAD_DIM)
    v = kv[..., D_MODEL:].reshape(B, S, A_HEADS, A_V_DIM)
    return k, v


def diff_attention(h, w_q, lam, g_head, w_out, k_sh, v_sh, cos, sin, lam_init):
    B, S, _ = h.shape
    nb = S // Q_BLOCK
    q = (h @ w_q).reshape(B, S, 2 * A_HEADS, A_HEAD_DIM)
    q = apply_rope(q, cos, sin) * (A_HEAD_DIM ** -0.5)
    q = q.reshape(B, nb, Q_BLOCK, A_HEADS, 2, A_HEAD_DIM).transpose(1, 0, 2, 3, 4, 5)
    lamf = lam.astype(jnp.float32)
    lam_full = (jnp.exp(jnp.sum(lamf[0] * lamf[1])) - jnp.exp(jnp.sum(lamf[2] * lamf[3]))
                + lam_init)
    k_idx = jnp.arange(S)

    def block(args):
        qi, bi = args
        scores = jnp.einsum('bqhcd,bkhcd->bhcqk', qi, k_sh).astype(jnp.float32)
        q_idx = bi * Q_BLOCK + jnp.arange(Q_BLOCK)
        mask = k_idx[None, :] <= q_idx[:, None]
        scores = jnp.where(mask, scores, -jnp.inf)
        p = jax.nn.softmax(scores, axis=-1)
        pd = p[:, :, 0] - lam_full * p[:, :, 1]
        return jnp.einsum('bhqk,bkhe->bqhe', pd.astype(v_sh.dtype), v_sh)

    o = lax.map(block, (q, jnp.arange(nb)))
    o = o.transpose(1, 0, 2, 3, 4).reshape(B, S, A_HEADS, A_V_DIM)
    o = rmsnorm(o, g_head.reshape(A_HEADS, A_V_DIM)) * (1.0 - lam_init)
    return o.reshape(B, S, D_MODEL) @ w_out


def setup_inputs(seed: int = 0) -> dict:
    key = jax.random.key(seed)
    ks = jax.random.split(key, 16)
    f32 = jnp.float32
    nrm = jax.random.normal
    x = nrm(ks[0], (BATCH, SEQ, D_MODEL), f32)
    positions = jnp.broadcast_to(jnp.arange(SEQ, dtype=jnp.int32)[None, :], (BATCH, SEQ))
    norm_g = 1.0 + 0.02 * nrm(ks[1], (DEPTH, 4, D_MODEL), f32)
    a_w_in = nrm(ks[2], (N_A_LAYERS, D_MODEL, M_PROJ), f32) * D_MODEL ** -0.5
    ig_b = 0.1 * nrm(ks[3], (N_A_LAYERS, M_HEADS), f32)
    fg_b = jnp.linspace(3.0, 6.0, M_HEADS, dtype=f32)[None, :] + 0.1 * nrm(ks[4], (N_A_LAYERS, M_HEADS), f32)
    a_b_gates = jnp.concatenate([ig_b, fg_b], axis=-1)
    a_g_head = 1.0 + 0.02 * nrm(ks[5], (N_A_LAYERS, D_MODEL), f32)
    a_w_out = nrm(ks[6], (N_A_LAYERS, D_MODEL, D_MODEL), f32) * D_MODEL ** -0.5
    kv_norm_g = 1.0 + 0.02 * nrm(ks[7], (D_MODEL,), f32)
    w_kv = nrm(ks[8], (D_MODEL, 2 * D_MODEL), f32) * D_MODEL ** -0.5
    b_w_q = nrm(ks[9], (N_B_LAYERS, D_MODEL, D_MODEL), f32) * D_MODEL ** -0.5
    b_lam = 0.1 * nrm(ks[10], (N_B_LAYERS, 4, A_HEAD_DIM), f32)
    b_g_head = 1.0 + 0.02 * nrm(ks[11], (N_B_LAYERS, D_MODEL), f32)
    b_w_out = nrm(ks[12], (N_B_LAYERS, D_MODEL, D_MODEL), f32) * D_MODEL ** -0.5
    mlp_w_up = nrm(ks[13], (DEPTH, D_MODEL, D_FF), f32) * D_MODEL ** -0.5
    mlp_w_down = nrm(ks[14], (DEPTH, D_FF, D_MODEL), f32) * D_FF ** -0.5
    return {'x': x, 'positions': positions, 'norm_g': norm_g,
            'a_w_in': a_w_in, 'a_b_gates': a_b_gates, 'a_g_head': a_g_head, 'a_w_out': a_w_out,
            'kv_norm_g': kv_norm_g, 'w_kv': w_kv,
            'b_w_q': b_w_q, 'b_lam': b_lam, 'b_g_head': b_g_head, 'b_w_out': b_w_out,
            'mlp_w_up': mlp_w_up, 'mlp_w_down': mlp_w_down}


def reference(x, positions, norm_g, a_w_in, a_b_gates, a_g_head, a_w_out,
              kv_norm_g, w_kv, b_w_q, b_lam, b_g_head, b_w_out, mlp_w_up, mlp_w_down):
    cos, sin = rope_tables(positions, A_HEAD_DIM)
    k_sh = None
    v_sh = None
    for layer in range(DEPTH):
        g = norm_g[layer]
        if layer < N_A_LAYERS:
            y = mlstm_mixer(rmsnorm(x, g[0]), a_w_in[layer], a_b_gates[layer],
                            a_g_head[layer], a_w_out[layer])
        else:
            j = layer - N_A_LAYERS
            if j == 0:
                k_sh, v_sh = shared_kv(x, kv_norm_g, w_kv, cos, sin)
            lam_init = 0.8 - 0.6 * math.exp(-0.3 * layer)
            y = diff_attention(rmsnorm(x, g[0]), b_w_q[j], b_lam[j], b_g_head[j], b_w_out[j],
                               k_sh, v_sh, cos, sin, lam_init)
        x = x + rmsnorm(y, g[1])
        x = x + rmsnorm(sqrelu_mlp(rmsnorm(x, g[2]), mlp_w_up[layer], mlp_w_down[layer]), g[3])
    return x
```

```python
import functools
import math

import jax
import jax.numpy as jnp
from jax import lax
from jax.experimental import pallas as pl
from jax.experimental.pallas import tpu as pltpu

F32 = jnp.float32
BF16 = jnp.bfloat16

D_MODEL = 1024
D_FF = 4 * D_MODEL
NORM_EPS = 1e-6

M_HEADS = 4
M_V_DIM = D_MODEL // M_HEADS
M_QK_DIM = M_V_DIM // 2
M_QK_COLS = M_HEADS * M_QK_DIM
GATE_SOFTCAP = 15.0
M_CHUNK = 256

A_HEADS = 8
A_HEAD_DIM = D_MODEL // (2 * A_HEADS)
A_V_DIM = 2 * A_HEAD_DIM
ROPE_THETA = 10000.0

LANES = 128
VMEM_LIMIT_BYTES = 56 * 1024 * 1024

ROW_TILE = 512
ATT_TQ = 256
ATT_TK = 256

_NT = (((1,), (1,)), ((), ()))
_TN = (((0,), (0,)), ((), ()))


def _params(*sem):
    return pltpu.CompilerParams(dimension_semantics=sem, vmem_limit_bytes=VMEM_LIMIT_BYTES)


def _rms(x, g):
    ms = jnp.mean(x * x, axis=-1, keepdims=True)
    return x * lax.rsqrt(ms + NORM_EPS) * g


def _const_spec(shape):
    nd = len(shape)
    return pl.BlockSpec(shape, lambda *_: (0,) * nd, pipeline_mode=pl.Buffered(1))


def _rope_table_kernel(pos_ref, inv_ref, sign_ref, cos_ref, sin_ref):
    ang = pos_ref[...].astype(F32) * inv_ref[...]
    cos_ref[...] = jnp.cos(ang)
    sin_ref[...] = jnp.sin(ang) * sign_ref[...]


def _rope_tables(pos):
    t = pos.shape[0]
    half = A_HEAD_DIM // 2
    inv = 1.0 / (ROPE_THETA ** (jnp.arange(0, A_HEAD_DIM, 2, dtype=F32) / A_HEAD_DIM))
    inv = jnp.tile(inv, LANES // half).reshape(1, LANES)
    sign = jnp.where(jnp.arange(LANES) < LANES // 2, -1.0, 1.0).astype(F32).reshape(1, LANES)
    tm = ROW_TILE
    return pl.pallas_call(
        _rope_table_kernel,
        grid=(t // tm,),
        in_specs=[pl.BlockSpec((tm, 1), lambda i: (i, 0)),
                  _const_spec((1, LANES)), _const_spec((1, LANES))],
        out_specs=[pl.BlockSpec((tm, LANES), lambda i: (i, 0))] * 2,
        out_shape=[jax.ShapeDtypeStruct((t, LANES), F32)] * 2,
        compiler_params=_params("parallel"),
        name="rope_tables",
    )(pos, inv, sign)


def _rope(t, cos, sin):
    return t * cos + pltpu.roll(t, LANES // 2, axis=1) * sin


def _a_proj_kernel(x_ref, g_ref, w_ref, wgt_ref, bg_ref, qkvo_ref, gates_ref):
    hb = _rms(x_ref[...], g_ref[...]).astype(BF16)
    n_out = qkvo_ref.shape[1]
    for n in range(0, n_out, 512):
        qkvo_ref[:, n:n + 512] = jnp.dot(
            hb, w_ref[:, n:n + 512], preferred_element_type=F32).astype(BF16)
    gr = lax.dot_general(wgt_ref[...], hb, _NT, preferred_element_type=F32) + bg_ref[...]
    gr = GATE_SOFTCAP * jnp.tanh(gr / GATE_SOFTCAP)
    log_sig = jnp.minimum(gr, 0.0) - jnp.log1p(jnp.exp(-jnp.abs(gr)))
    row = lax.broadcasted_iota(jnp.int32, gr.shape, 0)
    gates_ref[...] = jnp.where(row < M_HEADS, gr, log_sig)


def _a_proj(x, g, w_qkvo, w_gt, b_g):
    t = x.shape[0]
    tm = ROW_TILE
    n_out = w_qkvo.shape[1]
    return pl.pallas_call(
        _a_proj_kernel,
        grid=(t // tm,),
        in_specs=[pl.BlockSpec((tm, D_MODEL), lambda i: (i, 0)),
                  _const_spec((1, D_MODEL)),
                  _const_spec((D_MODEL, n_out)),
                  _const_spec((2 * M_HEADS, D_MODEL)),
                  _const_spec((2 * M_HEADS, 1))],
        out_specs=[pl.BlockSpec((tm, n_out), lambda i: (i, 0)),
                   pl.BlockSpec((2 * M_HEADS, tm), lambda i: (0, i))],
        out_shape=[jax.ShapeDtypeStruct((t, n_out), BF16),
                   jax.ShapeDtypeStruct((2 * M_HEADS, t), F32)],
        compiler_params=_params("parallel"),
        name="mlstm_proj",
    )(x, g, w_qkvo, w_gt, b_g)


def _mlstm_kernel(q_ref, k_ref, v_ref, o_ref, li_ref, lf_ref, gh_ref, y_ref,
                  c_ref, n_ref, m_ref):
    L = q_ref.shape[0]

    @pl.when(pl.program_id(2) == 0)
    def _():
        c_ref[...] = jnp.zeros_like(c_ref)
        n_ref[...] = jnp.zeros_like(n_ref)
        m_ref[...] = jnp.zeros_like(m_ref)

    q = q_ref[...]
    k = k_ref[...]
    v = v_ref[...]
    li_row = li_ref[0]
    lf_row = lf_ref[0]
    m_prev = m_ref[...]
    c_prev = c_ref[...]
    n_prev = n_ref[...]

    jj = lax.broadcasted_iota(jnp.int32, (L, L), 0)
    ss = lax.broadcasted_iota(jnp.int32, (L, L), 1)
    causal = ss <= jj
    upper = (jj <= ss).astype(F32)
    b_row = jnp.dot(jnp.broadcast_to(lf_row, (8, L)), upper,
                    preferred_element_type=F32, precision=lax.Precision.HIGHEST)[0:1]
    b_col = jnp.sum(jnp.where(causal, lf_row, 0.0), axis=1, keepdims=True)
    li_col = jnp.sum(jnp.where(jj == ss, li_row, 0.0), axis=1, keepdims=True)

    r_row = li_row - b_row
    r_mat = jnp.where(causal, r_row, -jnp.inf)
    m_col = jnp.maximum(jnp.max(r_mat, axis=1, keepdims=True), m_prev)
    w_intra = jnp.exp(r_mat - m_col)
    s_inter = jnp.exp(m_prev - m_col)

    s_qk = lax.dot_general(q, k, _NT, preferred_element_type=F32) * w_intra
    q_c = jnp.dot(q, c_prev.astype(BF16), preferred_element_type=F32)
    num = s_inter * q_c + jnp.dot(s_qk.astype(BF16), v, preferred_element_type=F32)
    q_n = jnp.sum(q.astype(F32) * n_prev, axis=1, keepdims=True)
    den = s_inter * q_n + jnp.sum(s_qk, axis=1, keepdims=True)
    floor = jnp.exp(-(b_col + m_col))
    h_out = num * (1.0 / jnp.maximum(jnp.abs(den), floor))

    b_last = b_row[:, L - 1:L]
    m_last = jnp.maximum(jnp.max(r_row, axis=1, keepdims=True), m_prev)
    decay = jnp.exp(m_prev - m_last)
    w_col = jnp.exp(li_col - b_col - m_last)
    kw = k.astype(F32) * w_col
    c_ref[...] = decay * c_prev + lax.dot_general(kw.astype(BF16), v, _TN,
                                                  preferred_element_type=F32)
    n_ref[...] = decay * n_prev + jnp.sum(kw, axis=0, keepdims=True)
    m_ref[...] = b_last + m_last

    hn = _rms(h_out, gh_ref[...])
    og = jax.nn.sigmoid(o_ref[...].astype(F32))
    y_ref[...] = (og * hn).astype(BF16)


def _mlstm(qkvo, gates, g_head, batch, seq):
    L = M_CHUNK
    nc = seq // L
    t = batch * seq
    qk_blocks = M_QK_COLS // M_QK_DIM
    v_off = 2 * M_QK_COLS // M_V_DIM
    o_off = v_off + M_HEADS
    row = lambda b, h, c: b * nc + c
    return pl.pallas_call(
        _mlstm_kernel,
        grid=(batch, M_HEADS, nc),
        in_specs=[pl.BlockSpec((L, M_QK_DIM), lambda b, h, c: (row(b, h, c), h)),
                  pl.BlockSpec((L, M_QK_DIM), lambda b, h, c: (row(b, h, c), qk_blocks + h)),
                  pl.BlockSpec((L, M_V_DIM), lambda b, h, c: (row(b, h, c), v_off + h)),
                  pl.BlockSpec((L, M_V_DIM), lambda b, h, c: (row(b, h, c), o_off + h)),
                  pl.BlockSpec((1, 1, L), lambda b, h, c: (h, 0, row(b, h, c))),
                  pl.BlockSpec((1, 1, L), lambda b, h, c: (M_HEADS + h, 0, row(b, h, c))),
                  pl.BlockSpec((1, M_V_DIM), lambda b, h, c: (0, h))],
        out_specs=pl.BlockSpec((L, M_V_DIM), lambda b, h, c: (row(b, h, c), h)),
        out_shape=jax.ShapeDtypeStruct((t, D_MODEL), BF16),
        scratch_shapes=[pltpu.VMEM((M_QK_DIM, M_V_DIM), F32),
                        pltpu.VMEM((1, M_QK_DIM), F32),
                        pltpu.VMEM((1, 1), F32)],
        compiler_params=_params("parallel", "parallel", "arbitrary"),
        name="mlstm_mixer",
    )(qkvo, qkvo, qkvo, qkvo, gates, gates, g_head)


def _post_kernel(x_ref, y_ref, wo_ref, g1_ref, g2_ref, g3_ref, wu_ref, wd_ref, out_ref):
    z = jnp.dot(y_ref[...], wo_ref[...], preferred_element_type=F32)
    x1 = x_ref[...] + _rms(z, g1_ref[...])
    hb = _rms(x1, g2_ref[...]).astype(BF16)
    acc = jnp.zeros(x1.shape, F32)
    for f in range(0, D_FF, 1024):
        u = jnp.maximum(jnp.dot(hb, wu_ref[:, f:f + 1024], preferred_element_type=F32), 0.0)
        acc = acc + jnp.dot((u * u).astype(BF16), wd_ref[f:f + 1024, :],
                            preferred_element_type=F32)
    out_ref[...] = x1 + _rms(acc, g3_ref[...])


def _post(x, y, w_out, g1, g2, g3, w_up, w_down):
    t = x.shape[0]
    tm = ROW_TILE
    return pl.pallas_call(
        _post_kernel,
        grid=(t // tm,),
        in_specs=[pl.BlockSpec((tm, D_MODEL), lambda i: (i, 0)),
                  pl.BlockSpec((tm, D_MODEL), lambda i: (i, 0)),
                  _const_spec((D_MODEL, D_MODEL)),
                  _const_spec((1, D_MODEL)), _const_spec((1, D_MODEL)), _const_spec((1, D_MODEL)),
                  _const_spec((D_MODEL, D_FF)),
                  _const_spec((D_FF, D_MODEL))],
        out_specs=pl.BlockSpec((tm, D_MODEL), lambda i: (i, 0)),
        out_shape=jax.ShapeDtypeStruct((t, D_MODEL), F32),
        compiler_params=_params("parallel"),
        name="outproj_mlp",
    )(x, y, w_out, g1, g2, g3, w_up, w_down)


def _kv_proj_kernel(x_ref, g_ref, wk_ref, wvt_ref, cos_ref, sin_ref, k_ref, vt_ref):
    hb = _rms(x_ref[...], g_ref[...]).astype(BF16)
    cos = cos_ref[...]
    sin = sin_ref[...]
    kk = jnp.dot(hb, wk_ref[...], preferred_element_type=F32)
    for h in range(A_HEADS):
        sl = slice(h * LANES, (h + 1) * LANES)
        k_ref[:, sl] = _rope(kk[:, sl], cos, sin).astype(BF16)
    vt_ref[0, 0] = lax.dot_general(wvt_ref[...], hb, _NT,
                                   preferred_element_type=F32).astype(BF16)


def _kv_proj(x, g, w_k, w_vt, cos, sin, batch, seq):
    t = x.shape[0]
    tm = ATT_TK
    nk = seq // tm
    return pl.pallas_call(
        _kv_proj_kernel,
        grid=(t // tm,),
        in_specs=[pl.BlockSpec((tm, D_MODEL), lambda i: (i, 0)),
                  _const_spec((1, D_MODEL)),
                  _const_spec((D_MODEL, D_MODEL)),
                  _const_spec((D_MODEL, D_MODEL)),
                  pl.BlockSpec((tm, LANES), lambda i: (i, 0)),
                  pl.BlockSpec((tm, LANES), lambda i: (i, 0))],
        out_specs=[pl.BlockSpec((tm, D_MODEL), lambda i: (i, 0)),
                   pl.BlockSpec((1, 1, D_MODEL, tm), lambda i: (i // nk, i % nk, 0, 0))],
        out_shape=[jax.ShapeDtypeStruct((t, D_MODEL), BF16),
                   jax.ShapeDtypeStruct((batch, nk, D_MODEL, tm), BF16)],
        compiler_params=_params("parallel"),
        name="kv_proj",
    )(x, g, w_k, w_vt, cos, sin)


def _q_proj_kernel(x_ref, g_ref, wq_ref, cos_ref, sin_ref, q_ref):
    hb = _rms(x_ref[...], g_ref[...]).astype(BF16)
    cos = cos_ref[...]
    sin = sin_ref[...]
    qq = jnp.dot(hb, wq_ref[...], preferred_element_type=F32)
    for h in range(A_HEADS):
        sl = slice(h * LANES, (h + 1) * LANES)
        q_ref[:, sl] = _rope(qq[:, sl], cos, sin).astype(BF16)


def _q_proj(x, g, w_q, cos, sin):
    t = x.shape[0]
    tm = ROW_TILE
    return pl.pallas_call(
        _q_proj_kernel,
        grid=(t // tm,),
        in_specs=[pl.BlockSpec((tm, D_MODEL), lambda i: (i, 0)),
                  _const_spec((1, D_MODEL)),
                  _const_spec((D_MODEL, D_MODEL)),
                  pl.BlockSpec((tm, LANES), lambda i: (i, 0)),
                  pl.BlockSpec((tm, LANES), lambda i: (i, 0))],
        out_specs=pl.BlockSpec((tm, D_MODEL), lambda i: (i, 0)),
        out_shape=jax.ShapeDtypeStruct((t, D_MODEL), BF16),
        compiler_params=_params("parallel"),
        name="q_proj",
    )(x, g, w_q, cos, sin)


def _attn_kernel(lam_ref, q_ref, k_ref, vt_ref, g_ref, o_ref, acc_ref, *, lam_init):
    tq = q_ref.shape[0]
    tk = vt_ref.shape[3]
    i = pl.program_id(2)

    lam = lam_ref[...]
    lam_full = (jnp.exp(jnp.sum(lam[0:1] * lam[1:2], keepdims=True))
                - jnp.exp(jnp.sum(lam[2:3] * lam[3:4], keepdims=True)) + lam_init)

    q_t = q_ref[...].astype(F32).T
    row_map = (lax.broadcasted_iota(jnp.int32, q_t.shape, 0) >> 5) & 1
    q_maps = [jnp.where(row_map == c, q_t, 0.0).astype(BF16) for c in range(2)]

    acc_ref[...] = jnp.zeros_like(acc_ref)
    key_i = lax.broadcasted_iota(jnp.int32, (tk, tq), 0)
    qry_i = lax.broadcasted_iota(jnp.int32, (tk, tq), 1)

    def step(j, carry, diagonal):
        kb = k_ref[pl.ds(pl.multiple_of(j * tk, tk), tk), :]
        vb = vt_ref[0, j]
        new = []
        for c in range(2):
            m_old, l_old = carry[2 * c], carry[2 * c + 1]
            s = jnp.dot(kb, q_maps[c], preferred_element_type=F32)
            if diagonal:
                s = jnp.where(key_i <= qry_i, s, -jnp.inf)
            m_new = jnp.maximum(m_old, jnp.max(s, axis=0, keepdims=True))
            alpha = jnp.exp(m_old - m_new)
            p = jnp.exp(s - m_new)
            l_new = alpha * l_old + jnp.sum(p, axis=0, keepdims=True)
            acc_ref[c] = alpha * acc_ref[c] + jnp.dot(vb, p.astype(BF16),
                                                      preferred_element_type=F32)
            new += [m_new, l_new]
        return tuple(new)

    neg = jnp.full((1, tq), -jnp.inf, F32)
    zero = jnp.zeros((1, tq), F32)
    carry = lax.fori_loop(0, i, lambda j, c: step(j, c, False), (neg, zero, neg, zero))
    _, l0, _, l1 = step(i, carry, True)

    o = acc_ref[0] * (1.0 / l0) - lam_full * (acc_ref[1] * (1.0 / l1))
    ms = jnp.mean(o * o, axis=0, keepdims=True)
    o = o * lax.rsqrt(ms + NORM_EPS) * g_ref[...] * (1.0 - lam_init)
    o_ref[...] = o.T.astype(BF16)


def _attention(q, k, vt, lam, g_col, batch, seq, lam_init):
    tq, tk = ATT_TQ, ATT_TK
    assert tq == tk
    nq = seq // tq
    nk = seq // tk
    t = batch * seq
    return pl.pallas_call(
        functools.partial(_attn_kernel, lam_init=lam_init),
        grid=(batch, A_HEADS, nq),
        in_specs=[pl.BlockSpec((4, A_HEAD_DIM), lambda b, h, i: (0, 0)),
                  pl.BlockSpec((tq, LANES), lambda b, h, i: (b * nq + i, h)),
                  pl.BlockSpec((seq, LANES), lambda b, h, i: (b, h)),
                  pl.BlockSpec((1, nk, A_V_DIM, tk), lambda b, h, i: (b, 0, h, 0)),
                  pl.BlockSpec((A_V_DIM, 1), lambda b, h, i: (h, 0))],
        out_specs=pl.BlockSpec((tq, LANES), lambda b, h, i: (b * nq + i, h)),
        out_shape=jax.ShapeDtypeStruct((t, D_MODEL), BF16),
        scratch_shapes=[pltpu.VMEM((2, A_V_DIM, tq), F32)],
        compiler_params=_params("parallel", "parallel", "arbitrary"),
        name="diff_attention",
    )(lam, q, k, vt, g_col)


def _head_perm(w):
    half = A_HEAD_DIM // 2
    w = w.reshape(w.shape[0], A_HEADS, 2, 2, half)
    return w.transpose(0, 1, 3, 2, 4).reshape(w.shape[0], D_MODEL)


def kernel(x, positions, norm_g, a_w_in, a_b_gates, a_g_head, a_w_out, kv_norm_g, w_kv,
           b_w_q, b_lam, b_g_head, b_w_out, mlp_w_up, mlp_w_down):
    batch, seq, _ = x.shape
    t = batch * seq
    depth = norm_g.shape[0]
    n_a = a_w_in.shape[0]
    xf = x.reshape(t, D_MODEL)
    row = lambda v: v.reshape(1, -1).astype(F32)

    cos, sin = _rope_tables(positions.reshape(t, 1))
    k_sh = vt_sh = None
    for layer in range(depth):
        g = norm_g[layer]
        if layer < n_a:
            w_in = a_w_in[layer]
            qkvo_cols = 2 * M_QK_COLS + 2 * D_MODEL
            col_scale = jnp.where((jnp.arange(qkvo_cols) >= M_QK_COLS)
                                  & (jnp.arange(qkvo_cols) < 2 * M_QK_COLS),
                                  M_QK_DIM ** -0.5, 1.0).astype(F32)
            w_qkvo = (w_in[:, :qkvo_cols] * col_scale).astype(BF16)
            w_gt = w_in[:, qkvo_cols:].T.astype(BF16)
            qkvo, gates = _a_proj(xf, row(g[0]), w_qkvo, w_gt,
                                  a_b_gates[layer].reshape(-1, 1).astype(F32))
            y = _mlstm(qkvo, gates.reshape(2 * M_HEADS, 1, t), row(a_g_head[layer]),
                       batch, seq)
            w_out = a_w_out[layer]
        else:
            j = layer - n_a
            if j == 0:
                w_k = _head_perm(w_kv[:, :D_MODEL]).astype(BF16)
                w_vt = w_kv[:, D_MODEL:].T.astype(BF16)
                k_sh, vt_sh = _kv_proj(xf, row(kv_norm_g), w_k, w_vt, cos, sin, batch, seq)
            lam_init = 0.8 - 0.6 * math.exp(-0.3 * layer)
            w_q = (_head_perm(b_w_q[j]) * (A_HEAD_DIM ** -0.5)).astype(BF16)
            q = _q_proj(xf, row(g[0]), w_q, cos, sin)
            y = _attention(q, k_sh, vt_sh, b_lam[j].astype(F32),
                           b_g_head[j].reshape(-1, 1).astype(F32), batch, seq, lam_init)
            w_out = b_w_out[j]
        xf = _post(xf, y, w_out.astype(BF16), row(g[1]), row(g[2]), row(g[3]),
                   mlp_w_up[layer].astype(BF16), mlp_w_down[layer].astype(BF16))
    return xf.reshape(batch, seq, D_MODEL)
```

```python
import functools
import math

import jax
import jax.numpy as jnp
from jax import lax
from jax.experimental import pallas as pl
from jax.experimental.pallas import tpu as pltpu

F32 = jnp.float32
BF16 = jnp.bfloat16

D_MODEL = 1024
D_FF = 4 * D_MODEL
NORM_EPS = 1e-6

M_HEADS = 4
M_V_DIM = D_MODEL // M_HEADS
M_QK_DIM = M_V_DIM // 2
M_QK_COLS = M_HEADS * M_QK_DIM
GATE_SOFTCAP = 15.0
M_CHUNK = 256

A_HEADS = 8
A_HEAD_DIM = D_MODEL // (2 * A_HEADS)
A_V_DIM = 2 * A_HEAD_DIM
ROPE_THETA = 10000.0

LANES = 128
VMEM_LIMIT_BYTES = 56 * 1024 * 1024

ROW_TILE = 512
ATT_TQ = 256
ATT_TK = 512
V_ROWS = A_V_DIM + 16

_NT = (((1,), (1,)), ((), ()))
_TN = (((0,), (0,)), ((), ()))


def _params(*sem):
    return pltpu.CompilerParams(dimension_semantics=sem, vmem_limit_bytes=VMEM_LIMIT_BYTES)


def _rms(x, g):
    ms = jnp.mean(x * x, axis=-1, keepdims=True)
    return x * lax.rsqrt(ms + NORM_EPS) * g


def _const_spec(shape):
    nd = len(shape)
    return pl.BlockSpec(shape, lambda *_: (0,) * nd, pipeline_mode=pl.Buffered(1))


def _rope_table_kernel(pos_ref, inv_ref, sign_ref, cos_ref, sin_ref):
    ang = pos_ref[...].astype(F32) * inv_ref[...]
    cos_ref[...] = jnp.cos(ang)
    sin_ref[...] = jnp.sin(ang) * sign_ref[...]


def _rope_tables(pos):
    t = pos.shape[0]
    half = A_HEAD_DIM // 2
    inv = 1.0 / (ROPE_THETA ** (jnp.arange(0, A_HEAD_DIM, 2, dtype=F32) / A_HEAD_DIM))
    inv = jnp.tile(inv, LANES // half).reshape(1, LANES)
    sign = jnp.where(jnp.arange(LANES) < LANES // 2, -1.0, 1.0).astype(F32).reshape(1, LANES)
    tm = ROW_TILE
    return pl.pallas_call(
        _rope_table_kernel,
        grid=(t // tm,),
        in_specs=[pl.BlockSpec((tm, 1), lambda i: (i, 0)),
                  _const_spec((1, LANES)), _const_spec((1, LANES))],
        out_specs=[pl.BlockSpec((tm, LANES), lambda i: (i, 0))] * 2,
        out_shape=[jax.ShapeDtypeStruct((t, LANES), F32)] * 2,
        compiler_params=_params("parallel"),
        name="rope_tables",
    )(pos, inv, sign)


def _rope(t, cos, sin):
    return t * cos + pltpu.roll(t, LANES // 2, axis=1) * sin


def _a_proj_kernel(x_ref, g_ref, w_ref, wgt_ref, bg_ref, qkvo_ref, gates_ref):
    hb = _rms(x_ref[...], g_ref[...]).astype(BF16)
    n_out = qkvo_ref.shape[1]
    for n in range(0, n_out, 512):
        qkvo_ref[:, n:n + 512] = jnp.dot(
            hb, w_ref[:, n:n + 512], preferred_element_type=F32).astype(BF16)
    gr = lax.dot_general(wgt_ref[...], hb, _NT, preferred_element_type=F32) + bg_ref[...]
    gr = GATE_SOFTCAP * jnp.tanh(gr / GATE_SOFTCAP)
    log_sig = jnp.minimum(gr, 0.0) - jnp.log1p(jnp.exp(-jnp.abs(gr)))
    row = lax.broadcasted_iota(jnp.int32, gr.shape, 0)
    gates_ref[...] = jnp.where(row < M_HEADS, gr, log_sig)


def _a_proj(x, g, w_qkvo, w_gt, b_g):
    t = x.shape[0]
    tm = ROW_TILE
    n_out = w_qkvo.shape[1]
    return pl.pallas_call(
        _a_proj_kernel,
        grid=(t // tm,),
        in_specs=[pl.BlockSpec((tm, D_MODEL), lambda i: (i, 0)),
                  _const_spec((1, D_MODEL)),
                  _const_spec((D_MODEL, n_out)),
                  _const_spec((2 * M_HEADS, D_MODEL)),
                  _const_spec((2 * M_HEADS, 1))],
        out_specs=[pl.BlockSpec((tm, n_out), lambda i: (i, 0)),
                   pl.BlockSpec((2 * M_HEADS, tm), lambda i: (0, i))],
        out_shape=[jax.ShapeDtypeStruct((t, n_out), BF16),
                   jax.ShapeDtypeStruct((2 * M_HEADS, t), F32)],
        compiler_params=_params("parallel"),
        name="mlstm_proj",
    )(x, g, w_qkvo, w_gt, b_g)


def _mlstm_kernel(q_ref, k_ref, v_ref, o_ref, li_ref, lf_ref, gh_ref, y_ref,
                  c_ref, n_ref, m_ref):
    L = q_ref.shape[0]

    @pl.when(pl.program_id(2) == 0)
    def _():
        c_ref[...] = jnp.zeros_like(c_ref)
        n_ref[...] = jnp.zeros_like(n_ref)
        m_ref[...] = jnp.zeros_like(m_ref)

    q = q_ref[...]
    k = k_ref[...]
    v = v_ref[...]
    li_row = li_ref[0]
    lf_row = lf_ref[0]
    m_prev = m_ref[...]
    c_prev = c_ref[...]
    n_prev = n_ref[...]

    jj = lax.broadcasted_iota(jnp.int32, (L, L), 0)
    ss = lax.broadcasted_iota(jnp.int32, (L, L), 1)
    causal = ss <= jj
    upper = (jj <= ss).astype(F32)
    b_row = jnp.dot(jnp.broadcast_to(lf_row, (8, L)), upper,
                    preferred_element_type=F32, precision=lax.Precision.HIGHEST)[0:1]
    b_col = jnp.sum(jnp.where(causal, lf_row, 0.0), axis=1, keepdims=True)
    li_col = jnp.sum(jnp.where(jj == ss, li_row, 0.0), axis=1, keepdims=True)

    r_row = li_row - b_row
    r_mat = jnp.where(causal, r_row, -jnp.inf)
    m_col = jnp.maximum(jnp.max(r_mat, axis=1, keepdims=True), m_prev)
    w_intra = jnp.exp(r_mat - m_col)
    s_inter = jnp.exp(m_prev - m_col)

    s_qk = lax.dot_general(q, k, _NT, preferred_element_type=F32) * w_intra
    q_c = jnp.dot(q, c_prev.astype(BF16), preferred_element_type=F32)
    num = s_inter * q_c + jnp.dot(s_qk.astype(BF16), v, preferred_element_type=F32)
    q_n = jnp.sum(q.astype(F32) * n_prev, axis=1, keepdims=True)
    den = s_inter * q_n + jnp.sum(s_qk, axis=1, keepdims=True)
    floor = jnp.exp(-(b_col + m_col))
    h_out = num * (1.0 / jnp.maximum(jnp.abs(den), floor))

    b_last = b_row[:, L - 1:L]
    m_last = jnp.maximum(jnp.max(r_row, axis=1, keepdims=True), m_prev)
    decay = jnp.exp(m_prev - m_last)
    w_col = jnp.exp(li_col - b_col - m_last)
    kw = k.astype(F32) * w_col
    c_ref[...] = decay * c_prev + lax.dot_general(kw.astype(BF16), v, _TN,
                                                  preferred_element_type=F32)
    n_ref[...] = decay * n_prev + jnp.sum(kw, axis=0, keepdims=True)
    m_ref[...] = b_last + m_last

    hn = _rms(h_out, gh_ref[...])
    og = jax.nn.sigmoid(o_ref[...].astype(F32))
    y_ref[...] = (og * hn).astype(BF16)


def _mlstm(qkvo, gates, g_head, batch, seq):
    L = M_CHUNK
    nc = seq // L
    t = batch * seq
    qk_blocks = M_QK_COLS // M_QK_DIM
    v_off = 2 * M_QK_COLS // M_V_DIM
    o_off = v_off + M_HEADS
    row = lambda b, h, c: b * nc + c
    return pl.pallas_call(
        _mlstm_kernel,
        grid=(batch, M_HEADS, nc),
        in_specs=[pl.BlockSpec((L, M_QK_DIM), lambda b, h, c: (row(b, h, c), h)),
                  pl.BlockSpec((L, M_QK_DIM), lambda b, h, c: (row(b, h, c), qk_blocks + h)),
                  pl.BlockSpec((L, M_V_DIM), lambda b, h, c: (row(b, h, c), v_off + h)),
                  pl.BlockSpec((L, M_V_DIM), lambda b, h, c: (row(b, h, c), o_off + h)),
                  pl.BlockSpec((1, 1, L), lambda b, h, c: (h, 0, row(b, h, c))),
                  pl.BlockSpec((1, 1, L), lambda b, h, c: (M_HEADS + h, 0, row(b, h, c))),
                  pl.BlockSpec((1, M_V_DIM), lambda b, h, c: (0, h))],
        out_specs=pl.BlockSpec((L, M_V_DIM), lambda b, h, c: (row(b, h, c), h)),
        out_shape=jax.ShapeDtypeStruct((t, D_MODEL), BF16),
        scratch_shapes=[pltpu.VMEM((M_QK_DIM, M_V_DIM), F32),
                        pltpu.VMEM((1, M_QK_DIM), F32),
                        pltpu.VMEM((1, 1), F32)],
        compiler_params=_params("parallel", "parallel", "arbitrary"),
        name="mlstm_mixer",
    )(qkvo, qkvo, qkvo, qkvo, gates, gates, g_head)


def _post_kernel(x_ref, y_ref, wo_ref, g1_ref, g2_ref, g3_ref, wu_ref, wd_ref, out_ref):
    z = jnp.dot(y_ref[...], wo_ref[...], preferred_element_type=F32)
    x1 = x_ref[...] + _rms(z, g1_ref[...])
    hb = _rms(x1, g2_ref[...]).astype(BF16)
    acc = jnp.zeros(x1.shape, F32)
    for f in range(0, D_FF, 1024):
        u = jnp.maximum(jnp.dot(hb, wu_ref[:, f:f + 1024], preferred_element_type=F32), 0.0)
        acc = acc + jnp.dot((u * u).astype(BF16), wd_ref[f:f + 1024, :],
                            preferred_element_type=F32)
    out_ref[...] = x1 + _rms(acc, g3_ref[...])


def _post(x, y, w_out, g1, g2, g3, w_up, w_down):
    t = x.shape[0]
    tm = ROW_TILE
    return pl.pallas_call(
        _post_kernel,
        grid=(t // tm,),
        in_specs=[pl.BlockSpec((tm, D_MODEL), lambda i: (i, 0)),
                  pl.BlockSpec((tm, D_MODEL), lambda i: (i, 0)),
                  _const_spec((D_MODEL, D_MODEL)),
                  _const_spec((1, D_MODEL)), _const_spec((1, D_MODEL)), _const_spec((1, D_MODEL)),
                  _const_spec((D_MODEL, D_FF)),
                  _const_spec((D_FF, D_MODEL))],
        out_specs=pl.BlockSpec((tm, D_MODEL), lambda i: (i, 0)),
        out_shape=jax.ShapeDtypeStruct((t, D_MODEL), F32),
        compiler_params=_params("parallel"),
        name="outproj_mlp",
    )(x, y, w_out, g1, g2, g3, w_up, w_down)


def _kv_proj_kernel(x_ref, g_ref, wk_ref, wvt_ref, cos_ref, sin_ref, k_ref, vt_ref):
    hb = _rms(x_ref[...], g_ref[...]).astype(BF16)
    cos = cos_ref[...]
    sin = sin_ref[...]
    kk = jnp.dot(hb, wk_ref[...], preferred_element_type=F32)
    for h in range(A_HEADS):
        sl = slice(h * LANES, (h + 1) * LANES)
        k_ref[:, sl] = _rope(kk[:, sl], cos, sin).astype(BF16)
    vt = lax.dot_general(wvt_ref[...], hb, _NT, preferred_element_type=F32)
    ones = jnp.ones((V_ROWS - A_V_DIM, vt.shape[1]), BF16)
    for h in range(A_HEADS):
        vt_ref[0, 0, h, :A_V_DIM, :] = vt[h * A_V_DIM:(h + 1) * A_V_DIM].astype(BF16)
        vt_ref[0, 0, h, A_V_DIM:, :] = ones


def _kv_proj(x, g, w_k, w_vt, cos, sin, batch, seq):
    t = x.shape[0]
    tm = ATT_TK
    nk = seq // tm
    return pl.pallas_call(
        _kv_proj_kernel,
        grid=(t // tm,),
        in_specs=[pl.BlockSpec((tm, D_MODEL), lambda i: (i, 0)),
                  _const_spec((1, D_MODEL)),
                  _const_spec((D_MODEL, D_MODEL)),
                  _const_spec((D_MODEL, D_MODEL)),
                  pl.BlockSpec((tm, LANES), lambda i: (i, 0)),
                  pl.BlockSpec((tm, LANES), lambda i: (i, 0))],
        out_specs=[pl.BlockSpec((tm, D_MODEL), lambda i: (i, 0)),
                   pl.BlockSpec((1, 1, A_HEADS, V_ROWS, tm),
                                lambda i: (i // nk, i % nk, 0, 0, 0))],
        out_shape=[jax.ShapeDtypeStruct((t, D_MODEL), BF16),
                   jax.ShapeDtypeStruct((batch, nk, A_HEADS, V_ROWS, tm), BF16)],
        compiler_params=_params("parallel"),
        name="kv_proj",
    )(x, g, w_k, w_vt, cos, sin)


def _q_proj_kernel(x_ref, g_ref, wq_ref, cos_ref, sin_ref, q_ref):
    hb = _rms(x_ref[...], g_ref[...]).astype(BF16)
    cos = cos_ref[...]
    sin = sin_ref[...]
    qq = jnp.dot(hb, wq_ref[...], preferred_element_type=F32)
    for h in range(A_HEADS):
        sl = slice(h * LANES, (h + 1) * LANES)
        q_ref[:, sl] = _rope(qq[:, sl], cos, sin).astype(BF16)


def _q_proj(x, g, w_q, cos, sin):
    t = x.shape[0]
    tm = ROW_TILE
    return pl.pallas_call(
        _q_proj_kernel,
        grid=(t // tm,),
        in_specs=[pl.BlockSpec((tm, D_MODEL), lambda i: (i, 0)),
                  _const_spec((1, D_MODEL)),
                  _const_spec((D_MODEL, D_MODEL)),
                  pl.BlockSpec((tm, LANES), lambda i: (i, 0)),
                  pl.BlockSpec((tm, LANES), lambda i: (i, 0))],
        out_specs=pl.BlockSpec((tm, D_MODEL), lambda i: (i, 0)),
        out_shape=jax.ShapeDtypeStruct((t, D_MODEL), BF16),
        compiler_params=_params("parallel"),
        name="q_proj",
    )(x, g, w_q, cos, sin)


def _attn_kernel(lam_ref, q_ref, k_ref, vt_ref, g_ref, o_ref, sa_ref, sb_ref, acc_ref, *,
                 lam_init):
    tq = q_ref.shape[0]
    tk = vt_ref.shape[4]
    i = pl.program_id(2)

    lam = lam_ref[...]
    lam_full = (jnp.exp(jnp.sum(lam[0:1] * lam[1:2], keepdims=True))
                - jnp.exp(jnp.sum(lam[2:3] * lam[3:4], keepdims=True)) + lam_init)

    q_t = q_ref[...].astype(F32).T
    row_map = (lax.broadcasted_iota(jnp.int32, q_t.shape, 0) >> 5) & 1
    q_cat = jnp.concatenate([jnp.where(row_map == c, q_t, 0.0) for c in range(2)],
                            axis=1).astype(BF16)

    def scores(j, s_ref):
        kb = k_ref[pl.ds(pl.multiple_of(j * tk, tk), tk), :]
        s_ref[...] = jnp.dot(kb, q_cat, preferred_element_type=F32)

    def softmax_pv(j, s_ref, m_old, diagonal):
        s = s_ref[...]
        if diagonal:
            key_i = j * tk + lax.broadcasted_iota(jnp.int32, s.shape, 0)
            qry_i = i * tq + (lax.broadcasted_iota(jnp.int32, s.shape, 1) & (tq - 1))
            s = jnp.where(key_i <= qry_i, s, -jnp.inf)
        m_new = jnp.maximum(m_old, jnp.max(s, axis=0, keepdims=True))
        alpha = jnp.exp2(m_old - m_new)
        p = jnp.exp2(s - m_new).astype(BF16)
        acc_ref[...] = alpha * acc_ref[...] + jnp.dot(vt_ref[0, j, 0], p,
                                                      preferred_element_type=F32)
        return m_new

    acc_ref[...] = jnp.zeros_like(acc_ref)
    n_full = (i * tq) // tk
    scores(0, sa_ref)

    def pair(jj, m_old):
        j = 2 * jj
        scores(j + 1, sb_ref)
        m_mid = softmax_pv(j, sa_ref, m_old, False)
        scores(j + 2, sa_ref)
        return softmax_pv(j + 1, sb_ref, m_mid, False)

    m_run = lax.fori_loop(0, n_full // 2, pair, jnp.full((1, 2 * tq), -jnp.inf, F32))

    def tail_odd(m_old):
        scores(n_full, sb_ref)
        m_mid = softmax_pv(n_full - 1, sa_ref, m_old, False)
        return softmax_pv(n_full, sb_ref, m_mid, True)

    def tail_even(m_old):
        return softmax_pv(n_full, sa_ref, m_old, True)

    lax.cond(n_full % 2 == 1, tail_odd, tail_even, m_run)

    acc = acc_ref[...]
    inv_l = 1.0 / acc[A_V_DIM:A_V_DIM + 1]
    o = acc[:A_V_DIM, :tq] * inv_l[:, :tq] - lam_full * (acc[:A_V_DIM, tq:] * inv_l[:, tq:])
    ms = jnp.mean(o * o, axis=0, keepdims=True)
    o = o * lax.rsqrt(ms + NORM_EPS) * g_ref[...] * (1.0 - lam_init)
    o_ref[...] = o.T.astype(BF16)


def _attention(q, k, vt, lam, g_col, batch, seq, lam_init):
    tq, tk = ATT_TQ, ATT_TK
    assert tk % tq == 0 and tq & (tq - 1) == 0
    nq = seq // tq
    nk = seq // tk
    t = batch * seq
    return pl.pallas_call(
        functools.partial(_attn_kernel, lam_init=lam_init),
        grid=(batch, A_HEADS, nq),
        in_specs=[pl.BlockSpec((4, A_HEAD_DIM), lambda b, h, i: (0, 0)),
                  pl.BlockSpec((tq, LANES), lambda b, h, i: (b * nq + i, h)),
                  pl.BlockSpec((seq, LANES), lambda b, h, i: (b, h)),
                  pl.BlockSpec((1, nk, 1, V_ROWS, tk), lambda b, h, i: (b, 0, h, 0, 0)),
                  pl.BlockSpec((A_V_DIM, 1), lambda b, h, i: (h, 0))],
        out_specs=pl.BlockSpec((tq, LANES), lambda b, h, i: (b * nq + i, h)),
        out_shape=jax.ShapeDtypeStruct((t, D_MODEL), BF16),
        scratch_shapes=[pltpu.VMEM((tk, 2 * tq), F32),
                        pltpu.VMEM((tk, 2 * tq), F32),
                        pltpu.VMEM((V_ROWS, 2 * tq), F32)],
        compiler_params=_params("parallel", "parallel", "arbitrary"),
        name="diff_attention",
    )(lam, q, k, vt, g_col)


def _head_perm(w):
    half = A_HEAD_DIM // 2
    w = w.reshape(w.shape[0], A_HEADS, 2, 2, half)
    return w.transpose(0, 1, 3, 2, 4).reshape(w.shape[0], D_MODEL)


def kernel(x, positions, norm_g, a_w_in, a_b_gates, a_g_head, a_w_out, kv_norm_g, w_kv,
           b_w_q, b_lam, b_g_head, b_w_out, mlp_w_up, mlp_w_down):
    batch, seq, _ = x.shape
    t = batch * seq
    depth = norm_g.shape[0]
    n_a = a_w_in.shape[0]
    xf = x.reshape(t, D_MODEL)
    row = lambda v: v.reshape(1, -1).astype(F32)

    cos, sin = _rope_tables(positions.reshape(t, 1))
    k_sh = vt_sh = None
    for layer in range(depth):
        g = norm_g[layer]
        if layer < n_a:
            w_in = a_w_in[layer]
            qkvo_cols = 2 * M_QK_COLS + 2 * D_MODEL
            col_scale = jnp.where((jnp.arange(qkvo_cols) >= M_QK_COLS)
                                  & (jnp.arange(qkvo_cols) < 2 * M_QK_COLS),
                                  M_QK_DIM ** -0.5, 1.0).astype(F32)
            w_qkvo = (w_in[:, :qkvo_cols] * col_scale).astype(BF16)
            w_gt = w_in[:, qkvo_cols:].T.astype(BF16)
            qkvo, gates = _a_proj(xf, row(g[0]), w_qkvo, w_gt,
                                  a_b_gates[layer].reshape(-1, 1).astype(F32))
            y = _mlstm(qkvo, gates.reshape(2 * M_HEADS, 1, t), row(a_g_head[layer]),
                       batch, seq)
            w_out = a_w_out[layer]
        else:
            j = layer - n_a
            if j == 0:
                w_k = _head_perm(w_kv[:, :D_MODEL]).astype(BF16)
                w_vt = w_kv[:, D_MODEL:].T.astype(BF16)
                k_sh, vt_sh = _kv_proj(xf, row(kv_norm_g), w_k, w_vt, cos, sin, batch, seq)
            lam_init = 0.8 - 0.6 * math.exp(-0.3 * layer)
            w_q = (_head_perm(b_w_q[j]) * (A_HEAD_DIM ** -0.5 * math.log2(math.e))).astype(BF16)
            q = _q_proj(xf, row(g[0]), w_q, cos, sin)
            y = _attention(q, k_sh, vt_sh, b_lam[j].astype(F32),
                           b_g_head[j].reshape(-1, 1).astype(F32), batch, seq, lam_init)
            w_out = b_w_out[j]
        xf = _post(xf, y, w_out.astype(BF16), row(g[1]), row(g[2]), row(g[3]),
                   mlp_w_up[layer].astype(BF16), mlp_w_down[layer].astype(BF16))
    return xf.reshape(batch, seq, D_MODEL)
```

```python
import functools
import math

import jax
import jax.numpy as jnp
from jax import lax
from jax.experimental import pallas as pl
from jax.experimental.pallas import tpu as pltpu

F32 = jnp.float32
BF16 = jnp.bfloat16

D_MODEL = 1024
D_FF = 4 * D_MODEL
NORM_EPS = 1e-6

M_HEADS = 4
M_V_DIM = D_MODEL // M_HEADS
M_QK_DIM = M_V_DIM // 2
M_QK_COLS = M_HEADS * M_QK_DIM
GATE_SOFTCAP = 15.0
M_CHUNK = 256

A_HEADS = 8
A_HEAD_DIM = D_MODEL // (2 * A_HEADS)
A_V_DIM = 2 * A_HEAD_DIM
ROPE_THETA = 10000.0

LANES = 128
VMEM_LIMIT_BYTES = 56 * 1024 * 1024

ROW_TILE = 512
ATT_TQ = 512
ATT_TK = 512
V_ROWS = A_V_DIM + 16

_NT = (((1,), (1,)), ((), ()))
_TN = (((0,), (0,)), ((), ()))


def _params(*sem):
    return pltpu.CompilerParams(dimension_semantics=sem, vmem_limit_bytes=VMEM_LIMIT_BYTES)


def _rms(x, g):
    ms = jnp.mean(x * x, axis=-1, keepdims=True)
    return x * lax.rsqrt(ms + NORM_EPS) * g


def _const_spec(shape):
    nd = len(shape)
    return pl.BlockSpec(shape, lambda *_: (0,) * nd, pipeline_mode=pl.Buffered(1))


def _rope_table_kernel(pos_ref, inv_ref, sign_ref, cos_ref, sin_ref):
    ang = pos_ref[...].astype(F32) * inv_ref[...]
    cos_ref[...] = jnp.cos(ang)
    sin_ref[...] = jnp.sin(ang) * sign_ref[...]


def _rope_tables(pos):
    t = pos.shape[0]
    half = A_HEAD_DIM // 2
    inv = 1.0 / (ROPE_THETA ** (jnp.arange(0, A_HEAD_DIM, 2, dtype=F32) / A_HEAD_DIM))
    inv = jnp.tile(inv, LANES // half).reshape(1, LANES)
    sign = jnp.where(jnp.arange(LANES) < LANES // 2, -1.0, 1.0).astype(F32).reshape(1, LANES)
    tm = ROW_TILE
    return pl.pallas_call(
        _rope_table_kernel,
        grid=(t // tm,),
        in_specs=[pl.BlockSpec((tm, 1), lambda i: (i, 0)),
                  _const_spec((1, LANES)), _const_spec((1, LANES))],
        out_specs=[pl.BlockSpec((tm, LANES), lambda i: (i, 0))] * 2,
        out_shape=[jax.ShapeDtypeStruct((t, LANES), F32)] * 2,
        compiler_params=_params("parallel"),
        name="rope_tables",
    )(pos, inv, sign)


def _rope(t, cos, sin):
    return t * cos + pltpu.roll(t, LANES // 2, axis=1) * sin


def _a_proj_kernel(x_ref, g_ref, w_ref, wgt_ref, bg_ref, qkvo_ref, gates_ref):
    hb = _rms(x_ref[...], g_ref[...]).astype(BF16)
    n_out = qkvo_ref.shape[1]
    for n in range(0, n_out, 512):
        qkvo_ref[:, n:n + 512] = jnp.dot(
            hb, w_ref[:, n:n + 512], preferred_element_type=F32).astype(BF16)
    gr = lax.dot_general(wgt_ref[...], hb, _NT, preferred_element_type=F32) + bg_ref[...]
    gr = GATE_SOFTCAP * jnp.tanh(gr / GATE_SOFTCAP)
    log_sig = jnp.minimum(gr, 0.0) - jnp.log1p(jnp.exp(-jnp.abs(gr)))
    row = lax.broadcasted_iota(jnp.int32, gr.shape, 0)
    gates_ref[...] = jnp.where(row < M_HEADS, gr, log_sig)


def _a_proj(x, g, w_qkvo, w_gt, b_g):
    t = x.shape[0]
    tm = ROW_TILE
    n_out = w_qkvo.shape[1]
    return pl.pallas_call(
        _a_proj_kernel,
        grid=(t // tm,),
        in_specs=[pl.BlockSpec((tm, D_MODEL), lambda i: (i, 0)),
                  _const_spec((1, D_MODEL)),
                  _const_spec((D_MODEL, n_out)),
                  _const_spec((2 * M_HEADS, D_MODEL)),
                  _const_spec((2 * M_HEADS, 1))],
        out_specs=[pl.BlockSpec((tm, n_out), lambda i: (i, 0)),
                   pl.BlockSpec((2 * M_HEADS, tm), lambda i: (0, i))],
        out_shape=[jax.ShapeDtypeStruct((t, n_out), BF16),
                   jax.ShapeDtypeStruct((2 * M_HEADS, t), F32)],
        compiler_params=_params("parallel"),
        name="mlstm_proj",
    )(x, g, w_qkvo, w_gt, b_g)


def _mlstm_kernel(q_ref, k_ref, v_ref, o_ref, li_ref, lf_ref, gh_ref, y_ref,
                  c_ref, n_ref, m_ref):
    L = q_ref.shape[0]

    @pl.when(pl.program_id(2) == 0)
    def _():
        c_ref[...] = jnp.zeros_like(c_ref)
        n_ref[...] = jnp.zeros_like(n_ref)
        m_ref[...] = jnp.zeros_like(m_ref)

    q = q_ref[...]
    k = k_ref[...]
    v = v_ref[...]
    li_row = li_ref[0]
    lf_row = lf_ref[0]
    m_prev = m_ref[...]
    c_prev = c_ref[...]
    n_prev = n_ref[...]

    jj = lax.broadcasted_iota(jnp.int32, (L, L), 0)
    ss = lax.broadcasted_iota(jnp.int32, (L, L), 1)
    causal = ss <= jj
    upper = (jj <= ss).astype(F32)
    b_row = jnp.dot(jnp.broadcast_to(lf_row, (8, L)), upper,
                    preferred_element_type=F32, precision=lax.Precision.HIGHEST)[0:1]
    b_col = jnp.sum(jnp.where(causal, lf_row, 0.0), axis=1, keepdims=True)
    li_col = jnp.sum(jnp.where(jj == ss, li_row, 0.0), axis=1, keepdims=True)

    r_row = li_row - b_row
    r_mat = jnp.where(causal, r_row, -jnp.inf)
    m_col = jnp.maximum(jnp.max(r_mat, axis=1, keepdims=True), m_prev)
    w_intra = jnp.exp(r_mat - m_col)
    s_inter = jnp.exp(m_prev - m_col)

    s_qk = lax.dot_general(q, k, _NT, preferred_element_type=F32) * w_intra
    q_c = jnp.dot(q, c_prev.astype(BF16), preferred_element_type=F32)
    num = s_inter * q_c + jnp.dot(s_qk.astype(BF16), v, preferred_element_type=F32)
    q_n = jnp.sum(q.astype(F32) * n_prev, axis=1, keepdims=True)
    den = s_inter * q_n + jnp.sum(s_qk, axis=1, keepdims=True)
    floor = jnp.exp(-(b_col + m_col))
    h_out = num * (1.0 / jnp.maximum(jnp.abs(den), floor))

    b_last = b_row[:, L - 1:L]
    m_last = jnp.maximum(jnp.max(r_row, axis=1, keepdims=True), m_prev)
    decay = jnp.exp(m_prev - m_last)
    w_col = jnp.exp(li_col - b_col - m_last)
    kw = k.astype(F32) * w_col
    c_ref[...] = decay * c_prev + lax.dot_general(kw.astype(BF16), v, _TN,
                                                  preferred_element_type=F32)
    n_ref[...] = decay * n_prev + jnp.sum(kw, axis=0, keepdims=True)
    m_ref[...] = b_last + m_last

    hn = _rms(h_out, gh_ref[...])
    og = jax.nn.sigmoid(o_ref[...].astype(F32))
    y_ref[...] = (og * hn).astype(BF16)


def _mlstm(qkvo, gates, g_head, batch, seq):
    L = M_CHUNK
    nc = seq // L
    t = batch * seq
    qk_blocks = M_QK_COLS // M_QK_DIM
    v_off = 2 * M_QK_COLS // M_V_DIM
    o_off = v_off + M_HEADS
    row = lambda b, h, c: b * nc + c
    return pl.pallas_call(
        _mlstm_kernel,
        grid=(batch, M_HEADS, nc),
        in_specs=[pl.BlockSpec((L, M_QK_DIM), lambda b, h, c: (row(b, h, c), h)),
                  pl.BlockSpec((L, M_QK_DIM), lambda b, h, c: (row(b, h, c), qk_blocks + h)),
                  pl.BlockSpec((L, M_V_DIM), lambda b, h, c: (row(b, h, c), v_off + h)),
                  pl.BlockSpec((L, M_V_DIM), lambda b, h, c: (row(b, h, c), o_off + h)),
                  pl.BlockSpec((1, 1, L), lambda b, h, c: (h, 0, row(b, h, c))),
                  pl.BlockSpec((1, 1, L), lambda b, h, c: (M_HEADS + h, 0, row(b, h, c))),
                  pl.BlockSpec((1, M_V_DIM), lambda b, h, c: (0, h))],
        out_specs=pl.BlockSpec((L, M_V_DIM), lambda b, h, c: (row(b, h, c), h)),
        out_shape=jax.ShapeDtypeStruct((t, D_MODEL), BF16),
        scratch_shapes=[pltpu.VMEM((M_QK_DIM, M_V_DIM), F32),
                        pltpu.VMEM((1, M_QK_DIM), F32),
                        pltpu.VMEM((1, 1), F32)],
        compiler_params=_params("parallel", "parallel", "arbitrary"),
        name="mlstm_mixer",
    )(qkvo, qkvo, qkvo, qkvo, gates, gates, g_head)


def _post_kernel(x_ref, y_ref, wo_ref, g1_ref, g2_ref, g3_ref, wu_ref, wd_ref, out_ref):
    z = jnp.dot(y_ref[...], wo_ref[...], preferred_element_type=F32)
    x1 = x_ref[...] + _rms(z, g1_ref[...])
    hb = _rms(x1, g2_ref[...]).astype(BF16)
    acc = jnp.zeros(x1.shape, F32)
    for f in range(0, D_FF, 1024):
        u = jnp.maximum(jnp.dot(hb, wu_ref[:, f:f + 1024], preferred_element_type=F32), 0.0)
        acc = acc + jnp.dot((u * u).astype(BF16), wd_ref[f:f + 1024, :],
                            preferred_element_type=F32)
    out_ref[...] = x1 + _rms(acc, g3_ref[...])


def _post(x, y, w_out, g1, g2, g3, w_up, w_down):
    t = x.shape[0]
    tm = ROW_TILE
    return pl.pallas_call(
        _post_kernel,
        grid=(t // tm,),
        in_specs=[pl.BlockSpec((tm, D_MODEL), lambda i: (i, 0)),
                  pl.BlockSpec((tm, D_MODEL), lambda i: (i, 0)),
                  _const_spec((D_MODEL, D_MODEL)),
                  _const_spec((1, D_MODEL)), _const_spec((1, D_MODEL)), _const_spec((1, D_MODEL)),
                  _const_spec((D_MODEL, D_FF)),
                  _const_spec((D_FF, D_MODEL))],
        out_specs=pl.BlockSpec((tm, D_MODEL), lambda i: (i, 0)),
        out_shape=jax.ShapeDtypeStruct((t, D_MODEL), F32),
        compiler_params=_params("parallel"),
        name="outproj_mlp",
    )(x, y, w_out, g1, g2, g3, w_up, w_down)


def _kv_proj_kernel(x_ref, g_ref, wk_ref, wvt_ref, cos_ref, sin_ref, k_ref, vt_ref):
    hb = _rms(x_ref[...], g_ref[...]).astype(BF16)
    cos = cos_ref[...]
    sin = sin_ref[...]
    kk = jnp.dot(hb, wk_ref[...], preferred_element_type=F32)
    for h in range(A_HEADS):
        sl = slice(h * LANES, (h + 1) * LANES)
        k_ref[:, sl] = _rope(kk[:, sl], cos, sin).astype(BF16)
    vt = lax.dot_general(wvt_ref[...], hb, _NT, preferred_element_type=F32)
    ones = jnp.ones((V_ROWS - A_V_DIM, vt.shape[1]), BF16)
    for h in range(A_HEADS):
        vt_ref[0, 0, h, :A_V_DIM, :] = vt[h * A_V_DIM:(h + 1) * A_V_DIM].astype(BF16)
        vt_ref[0, 0, h, A_V_DIM:, :] = ones


def _kv_proj(x, g, w_k, w_vt, cos, sin, batch, seq):
    t = x.shape[0]
    tm = ATT_TK
    nk = seq // tm
    return pl.pallas_call(
        _kv_proj_kernel,
        grid=(t // tm,),
        in_specs=[pl.BlockSpec((tm, D_MODEL), lambda i: (i, 0)),
                  _const_spec((1, D_MODEL)),
                  _const_spec((D_MODEL, D_MODEL)),
                  _const_spec((D_MODEL, D_MODEL)),
                  pl.BlockSpec((tm, LANES), lambda i: (i, 0)),
                  pl.BlockSpec((tm, LANES), lambda i: (i, 0))],
        out_specs=[pl.BlockSpec((tm, D_MODEL), lambda i: (i, 0)),
                   pl.BlockSpec((1, 1, A_HEADS, V_ROWS, tm),
                                lambda i: (i // nk, i % nk, 0, 0, 0))],
        out_shape=[jax.ShapeDtypeStruct((t, D_MODEL), BF16),
                   jax.ShapeDtypeStruct((batch, nk, A_HEADS, V_ROWS, tm), BF16)],
        compiler_params=_params("parallel"),
        name="kv_proj",
    )(x, g, w_k, w_vt, cos, sin)


def _q_proj_kernel(x_ref, g_ref, wq_ref, cos_ref, sin_ref, q_ref):
    hb = _rms(x_ref[...], g_ref[...]).astype(BF16)
    cos = cos_ref[...]
    sin = sin_ref[...]
    qq = jnp.dot(hb, wq_ref[...], preferred_element_type=F32)
    for h in range(A_HEADS):
        sl = slice(h * LANES, (h + 1) * LANES)
        q_ref[:, sl] = _rope(qq[:, sl], cos, sin).astype(BF16)


def _q_proj(x, g, w_q, cos, sin):
    t = x.shape[0]
    tm = ROW_TILE
    return pl.pallas_call(
        _q_proj_kernel,
        grid=(t // tm,),
        in_specs=[pl.BlockSpec((tm, D_MODEL), lambda i: (i, 0)),
                  _const_spec((1, D_MODEL)),
                  _const_spec((D_MODEL, D_MODEL)),
                  pl.BlockSpec((tm, LANES), lambda i: (i, 0)),
                  pl.BlockSpec((tm, LANES), lambda i: (i, 0))],
        out_specs=pl.BlockSpec((tm, D_MODEL), lambda i: (i, 0)),
        out_shape=jax.ShapeDtypeStruct((t, D_MODEL), BF16),
        compiler_params=_params("parallel"),
        name="q_proj",
    )(x, g, w_q, cos, sin)


def _attn_kernel(lam_ref, q_ref, k_ref, vt_ref, g_ref, o_ref, sa_ref, sb_ref, acc_ref, *,
                 lam_init):
    tq = q_ref.shape[0]
    tk = vt_ref.shape[4]
    i = pl.program_id(2)

    lam = lam_ref[...]
    lam_full = (jnp.exp(jnp.sum(lam[0:1] * lam[1:2], keepdims=True))
                - jnp.exp(jnp.sum(lam[2:3] * lam[3:4], keepdims=True)) + lam_init)

    q_t = q_ref[...].astype(F32).T
    row_map = (lax.broadcasted_iota(jnp.int32, q_t.shape, 0) >> 5) & 1
    q_cat = jnp.concatenate([jnp.where(row_map == c, q_t, 0.0) for c in range(2)],
                            axis=1).astype(BF16)

    def scores(j, s_ref):
        kb = k_ref[pl.ds(pl.multiple_of(j * tk, tk), tk), :]
        s_ref[...] = jnp.dot(kb, q_cat, preferred_element_type=F32)

    def softmax_pv(j, s_ref, m_old, diagonal):
        s = s_ref[...]
        if diagonal:
            key_i = j * tk + lax.broadcasted_iota(jnp.int32, s.shape, 0)
            qry_i = i * tq + (lax.broadcasted_iota(jnp.int32, s.shape, 1) & (tq - 1))
            s = jnp.where(key_i <= qry_i, s, -jnp.inf)
        m_new = jnp.maximum(m_old, jnp.max(s, axis=0, keepdims=True))
        alpha = jnp.exp2(m_old - m_new)
        p = jnp.exp2(s - m_new).astype(BF16)
        acc_ref[...] = alpha * acc_ref[...] + jnp.dot(vt_ref[0, j, 0], p,
                                                      preferred_element_type=F32)
        return m_new

    acc_ref[...] = jnp.zeros_like(acc_ref)
    n_full = (i * tq) // tk
    scores(0, sa_ref)

    def pair(jj, m_old):
        j = 2 * jj
        scores(j + 1, sb_ref)
        m_mid = softmax_pv(j, sa_ref, m_old, False)
        scores(j + 2, sa_ref)
        return softmax_pv(j + 1, sb_ref, m_mid, False)

    m_run = lax.fori_loop(0, n_full // 2, pair, jnp.full((1, 2 * tq), -jnp.inf, F32))

    def tail_odd(m_old):
        scores(n_full, sb_ref)
        m_mid = softmax_pv(n_full - 1, sa_ref, m_old, False)
        return softmax_pv(n_full, sb_ref, m_mid, True)

    def tail_even(m_old):
        return softmax_pv(n_full, sa_ref, m_old, True)

    lax.cond(n_full % 2 == 1, tail_odd, tail_even, m_run)

    acc = acc_ref[...]
    inv_l = 1.0 / acc[A_V_DIM:A_V_DIM + 1]
    o = acc[:A_V_DIM, :tq] * inv_l[:, :tq] - lam_full * (acc[:A_V_DIM, tq:] * inv_l[:, tq:])
    ms = jnp.mean(o * o, axis=0, keepdims=True)
    o = o * lax.rsqrt(ms + NORM_EPS) * g_ref[...] * (1.0 - lam_init)
    o_ref[...] = o.T.astype(BF16)


def _attention(q, k, vt, lam, g_col, batch, seq, lam_init):
    tq, tk = ATT_TQ, ATT_TK
    assert tk % tq == 0 and tq & (tq - 1) == 0
    nq = seq // tq
    nk = seq // tk
    t = batch * seq
    return pl.pallas_call(
        functools.partial(_attn_kernel, lam_init=lam_init),
        grid=(batch, A_HEADS, nq),
        in_specs=[pl.BlockSpec((4, A_HEAD_DIM), lambda b, h, i: (0, 0)),
                  pl.BlockSpec((tq, LANES), lambda b, h, i: (b * nq + i, h)),
                  pl.BlockSpec((seq, LANES), lambda b, h, i: (b, h)),
                  pl.BlockSpec((1, nk, 1, V_ROWS, tk), lambda b, h, i: (b, 0, h, 0, 0)),
                  pl.BlockSpec((A_V_DIM, 1), lambda b, h, i: (h, 0))],
        out_specs=pl.BlockSpec((tq, LANES), lambda b, h, i: (b * nq + i, h)),
        out_shape=jax.ShapeDtypeStruct((t, D_MODEL), BF16),
        scratch_shapes=[pltpu.VMEM((tk, 2 * tq), F32),
                        pltpu.VMEM((tk, 2 * tq), F32),
                        pltpu.VMEM((V_ROWS, 2 * tq), F32)],
        compiler_params=_params("parallel", "parallel", "arbitrary"),
        name="diff_attention",
    )(lam, q, k, vt, g_col)


def _head_perm(w):
    half = A_HEAD_DIM // 2
    w = w.reshape(w.shape[0], A_HEADS, 2, 2, half)
    return w.transpose(0, 1, 3, 2, 4).reshape(w.shape[0], D_MODEL)


def kernel(x, positions, norm_g, a_w_in, a_b_gates, a_g_head, a_w_out, kv_norm_g, w_kv,
           b_w_q, b_lam, b_g_head, b_w_out, mlp_w_up, mlp_w_down):
    batch, seq, _ = x.shape
    t = batch * seq
    depth = norm_g.shape[0]
    n_a = a_w_in.shape[0]
    xf = x.reshape(t, D_MODEL)
    row = lambda v: v.reshape(1, -1).astype(F32)

    cos, sin = _rope_tables(positions.reshape(t, 1))
    k_sh = vt_sh = None
    for layer in range(depth):
        g = norm_g[layer]
        if layer < n_a:
            w_in = a_w_in[layer]
            qkvo_cols = 2 * M_QK_COLS + 2 * D_MODEL
            col_scale = jnp.where((jnp.arange(qkvo_cols) >= M_QK_COLS)
                                  & (jnp.arange(qkvo_cols) < 2 * M_QK_COLS),
                                  M_QK_DIM ** -0.5, 1.0).astype(F32)
            w_qkvo = (w_in[:, :qkvo_cols] * col_scale).astype(BF16)
            w_gt = w_in[:, qkvo_cols:].T.astype(BF16)
            qkvo, gates = _a_proj(xf, row(g[0]), w_qkvo, w_gt,
                                  a_b_gates[layer].reshape(-1, 1).astype(F32))
            y = _mlstm(qkvo, gates.reshape(2 * M_HEADS, 1, t), row(a_g_head[layer]),
                       batch, seq)
            w_out = a_w_out[layer]
        else:
            j = layer - n_a
            if j == 0:
                w_k = _head_perm(w_kv[:, :D_MODEL]).astype(BF16)
                w_vt = w_kv[:, D_MODEL:].T.astype(BF16)
                k_sh, vt_sh = _kv_proj(xf, row(kv_norm_g), w_k, w_vt, cos, sin, batch, seq)
            lam_init = 0.8 - 0.6 * math.exp(-0.3 * layer)
            w_q = (_head_perm(b_w_q[j]) * (A_HEAD_DIM ** -0.5 * math.log2(math.e))).astype(BF16)
            q = _q_proj(xf, row(g[0]), w_q, cos, sin)
            y = _attention(q, k_sh, vt_sh, b_lam[j].astype(F32),
                           b_g_head[j].reshape(-1, 1).astype(F32), batch, seq, lam_init)
            w_out = b_w_out[j]
        xf = _post(xf, y, w_out.astype(BF16), row(g[1]), row(g[2]), row(g[3]),
                   mlp_w_up[layer].astype(BF16), mlp_w_down[layer].astype(BF16))
    return xf.reshape(batch, seq, D_MODEL)
```

```python
import functools
import math

import jax
import jax.numpy as jnp
from jax import lax
from jax.experimental import pallas as pl
from jax.experimental.pallas import tpu as pltpu

F32 = jnp.float32
BF16 = jnp.bfloat16

D_MODEL = 1024
D_FF = 4 * D_MODEL
NORM_EPS = 1e-6

M_HEADS = 4
M_V_DIM = D_MODEL // M_HEADS
M_QK_DIM = M_V_DIM // 2
M_QK_COLS = M_HEADS * M_QK_DIM
GATE_SOFTCAP = 15.0
M_CHUNK = 256

A_HEADS = 8
A_HEAD_DIM = D_MODEL // (2 * A_HEADS)
A_V_DIM = 2 * A_HEAD_DIM
ROPE_THETA = 10000.0

LANES = 128
VMEM_LIMIT_BYTES = 56 * 1024 * 1024

ROW_TILE = 512
ATT_TQ = 512
ATT_TK = 512
ATT_TRIP_TILES = 4
V_ROWS = A_V_DIM + 16

_NT = (((1,), (1,)), ((), ()))
_TN = (((0,), (0,)), ((), ()))


def _params(*sem):
    return pltpu.CompilerParams(dimension_semantics=sem, vmem_limit_bytes=VMEM_LIMIT_BYTES)


def _rms(x, g):
    ms = jnp.mean(x * x, axis=-1, keepdims=True)
    return x * lax.rsqrt(ms + NORM_EPS) * g


def _const_spec(shape):
    nd = len(shape)
    return pl.BlockSpec(shape, lambda *_: (0,) * nd, pipeline_mode=pl.Buffered(1))


def _rope_table_kernel(pos_ref, inv_ref, sign_ref, cos_ref, sin_ref):
    ang = pos_ref[...].astype(F32) * inv_ref[...]
    cos_ref[...] = jnp.cos(ang)
    sin_ref[...] = jnp.sin(ang) * sign_ref[...]


def _rope_tables(pos):
    t = pos.shape[0]
    half = A_HEAD_DIM // 2
    inv = 1.0 / (ROPE_THETA ** (jnp.arange(0, A_HEAD_DIM, 2, dtype=F32) / A_HEAD_DIM))
    inv = jnp.tile(inv, LANES // half).reshape(1, LANES)
    sign = jnp.where(jnp.arange(LANES) < LANES // 2, -1.0, 1.0).astype(F32).reshape(1, LANES)
    tm = ROW_TILE
    return pl.pallas_call(
        _rope_table_kernel,
        grid=(t // tm,),
        in_specs=[pl.BlockSpec((tm, 1), lambda i: (i, 0)),
                  _const_spec((1, LANES)), _const_spec((1, LANES))],
        out_specs=[pl.BlockSpec((tm, LANES), lambda i: (i, 0))] * 2,
        out_shape=[jax.ShapeDtypeStruct((t, LANES), F32)] * 2,
        compiler_params=_params("parallel"),
        name="rope_tables",
    )(pos, inv, sign)


def _rope(t, cos, sin):
    return t * cos + pltpu.roll(t, LANES // 2, axis=1) * sin


def _a_proj_kernel(x_ref, g_ref, w_ref, wgt_ref, bg_ref, qkvo_ref, gates_ref):
    hb = _rms(x_ref[...], g_ref[...]).astype(BF16)
    n_out = qkvo_ref.shape[1]
    for n in range(0, n_out, 512):
        qkvo_ref[:, n:n + 512] = jnp.dot(
            hb, w_ref[:, n:n + 512], preferred_element_type=F32).astype(BF16)
    gr = lax.dot_general(wgt_ref[...], hb, _NT, preferred_element_type=F32) + bg_ref[...]
    gr = GATE_SOFTCAP * jnp.tanh(gr / GATE_SOFTCAP)
    log_sig = jnp.minimum(gr, 0.0) - jnp.log1p(jnp.exp(-jnp.abs(gr)))
    row = lax.broadcasted_iota(jnp.int32, gr.shape, 0)
    gates_ref[...] = jnp.where(row < M_HEADS, gr, log_sig)


def _a_proj(x, g, w_qkvo, w_gt, b_g):
    t = x.shape[0]
    tm = ROW_TILE
    n_out = w_qkvo.shape[1]
    return pl.pallas_call(
        _a_proj_kernel,
        grid=(t // tm,),
        in_specs=[pl.BlockSpec((tm, D_MODEL), lambda i: (i, 0)),
                  _const_spec((1, D_MODEL)),
                  _const_spec((D_MODEL, n_out)),
                  _const_spec((2 * M_HEADS, D_MODEL)),
                  _const_spec((2 * M_HEADS, 1))],
        out_specs=[pl.BlockSpec((tm, n_out), lambda i: (i, 0)),
                   pl.BlockSpec((2 * M_HEADS, tm), lambda i: (0, i))],
        out_shape=[jax.ShapeDtypeStruct((t, n_out), BF16),
                   jax.ShapeDtypeStruct((2 * M_HEADS, t), F32)],
        compiler_params=_params("parallel"),
        name="mlstm_proj",
    )(x, g, w_qkvo, w_gt, b_g)


def _mlstm_kernel(q_ref, k_ref, v_ref, o_ref, li_ref, lf_ref, gh_ref, y_ref,
                  c_ref, n_ref, m_ref):
    L = q_ref.shape[0]

    @pl.when(pl.program_id(2) == 0)
    def _():
        c_ref[...] = jnp.zeros_like(c_ref)
        n_ref[...] = jnp.zeros_like(n_ref)
        m_ref[...] = jnp.zeros_like(m_ref)

    q = q_ref[...]
    k = k_ref[...]
    v = v_ref[...]
    li_row = li_ref[0]
    lf_row = lf_ref[0]
    m_prev = m_ref[...]
    c_prev = c_ref[...]
    n_prev = n_ref[...]

    jj = lax.broadcasted_iota(jnp.int32, (L, L), 0)
    ss = lax.broadcasted_iota(jnp.int32, (L, L), 1)
    causal = ss <= jj
    upper = (jj <= ss).astype(F32)
    b_row = jnp.dot(jnp.broadcast_to(lf_row, (8, L)), upper,
                    preferred_element_type=F32, precision=lax.Precision.HIGHEST)[0:1]
    b_col = jnp.sum(jnp.where(causal, lf_row, 0.0), axis=1, keepdims=True)
    li_col = jnp.sum(jnp.where(jj == ss, li_row, 0.0), axis=1, keepdims=True)

    r_row = li_row - b_row
    r_mat = jnp.where(causal, r_row, -jnp.inf)
    m_col = jnp.maximum(jnp.max(r_mat, axis=1, keepdims=True), m_prev)
    w_intra = jnp.exp(r_mat - m_col)
    s_inter = jnp.exp(m_prev - m_col)

    s_qk = lax.dot_general(q, k, _NT, preferred_element_type=F32) * w_intra
    q_c = jnp.dot(q, c_prev.astype(BF16), preferred_element_type=F32)
    num = s_inter * q_c + jnp.dot(s_qk.astype(BF16), v, preferred_element_type=F32)
    q_n = jnp.sum(q.astype(F32) * n_prev, axis=1, keepdims=True)
    den = s_inter * q_n + jnp.sum(s_qk, axis=1, keepdims=True)
    floor = jnp.exp(-(b_col + m_col))
    h_out = num * (1.0 / jnp.maximum(jnp.abs(den), floor))

    b_last = b_row[:, L - 1:L]
    m_last = jnp.maximum(jnp.max(r_row, axis=1, keepdims=True), m_prev)
    decay = jnp.exp(m_prev - m_last)
    w_col = jnp.exp(li_col - b_col - m_last)
    kw = k.astype(F32) * w_col
    c_ref[...] = decay * c_prev + lax.dot_general(kw.astype(BF16), v, _TN,
                                                  preferred_element_type=F32)
    n_ref[...] = decay * n_prev + jnp.sum(kw, axis=0, keepdims=True)
    m_ref[...] = b_last + m_last

    hn = _rms(h_out, gh_ref[...])
    og = jax.nn.sigmoid(o_ref[...].astype(F32))
    y_ref[...] = (og * hn).astype(BF16)


def _mlstm(qkvo, gates, g_head, batch, seq):
    L = M_CHUNK
    nc = seq // L
    t = batch * seq
    qk_blocks = M_QK_COLS // M_QK_DIM
    v_off = 2 * M_QK_COLS // M_V_DIM
    o_off = v_off + M_HEADS
    row = lambda b, h, c: b * nc + c
    return pl.pallas_call(
        _mlstm_kernel,
        grid=(batch, M_HEADS, nc),
        in_specs=[pl.BlockSpec((L, M_QK_DIM), lambda b, h, c: (row(b, h, c), h)),
                  pl.BlockSpec((L, M_QK_DIM), lambda b, h, c: (row(b, h, c), qk_blocks + h)),
                  pl.BlockSpec((L, M_V_DIM), lambda b, h, c: (row(b, h, c), v_off + h)),
                  pl.BlockSpec((L, M_V_DIM), lambda b, h, c: (row(b, h, c), o_off + h)),
                  pl.BlockSpec((1, 1, L), lambda b, h, c: (h, 0, row(b, h, c))),
                  pl.BlockSpec((1, 1, L), lambda b, h, c: (M_HEADS + h, 0, row(b, h, c))),
                  pl.BlockSpec((1, M_V_DIM), lambda b, h, c: (0, h))],
        out_specs=pl.BlockSpec((L, M_V_DIM), lambda b, h, c: (row(b, h, c), h)),
        out_shape=jax.ShapeDtypeStruct((t, D_MODEL), BF16),
        scratch_shapes=[pltpu.VMEM((M_QK_DIM, M_V_DIM), F32),
                        pltpu.VMEM((1, M_QK_DIM), F32),
                        pltpu.VMEM((1, 1), F32)],
        compiler_params=_params("parallel", "parallel", "arbitrary"),
        name="mlstm_mixer",
    )(qkvo, qkvo, qkvo, qkvo, gates, gates, g_head)


def _post_kernel(x_ref, y_ref, wo_ref, g1_ref, g2_ref, g3_ref, wu_ref, wd_ref, out_ref):
    z = jnp.dot(y_ref[...], wo_ref[...], preferred_element_type=F32)
    x1 = x_ref[...] + _rms(z, g1_ref[...])
    hb = _rms(x1, g2_ref[...]).astype(BF16)
    acc = jnp.zeros(x1.shape, F32)
    for f in range(0, D_FF, 1024):
        u = jnp.maximum(jnp.dot(hb, wu_ref[:, f:f + 1024], preferred_element_type=F32), 0.0)
        acc = acc + jnp.dot((u * u).astype(BF16), wd_ref[f:f + 1024, :],
                            preferred_element_type=F32)
    out_ref[...] = x1 + _rms(acc, g3_ref[...])


def _post(x, y, w_out, g1, g2, g3, w_up, w_down):
    t = x.shape[0]
    tm = ROW_TILE
    return pl.pallas_call(
        _post_kernel,
        grid=(t // tm,),
        in_specs=[pl.BlockSpec((tm, D_MODEL), lambda i: (i, 0)),
                  pl.BlockSpec((tm, D_MODEL), lambda i: (i, 0)),
                  _const_spec((D_MODEL, D_MODEL)),
                  _const_spec((1, D_MODEL)), _const_spec((1, D_MODEL)), _const_spec((1, D_MODEL)),
                  _const_spec((D_MODEL, D_FF)),
                  _const_spec((D_FF, D_MODEL))],
        out_specs=pl.BlockSpec((tm, D_MODEL), lambda i: (i, 0)),
        out_shape=jax.ShapeDtypeStruct((t, D_MODEL), F32),
        compiler_params=_params("parallel"),
        name="outproj_mlp",
    )(x, y, w_out, g1, g2, g3, w_up, w_down)


def _kv_proj_kernel(x_ref, g_ref, wk_ref, wvt_ref, cos_ref, sin_ref, k_ref, vt_ref):
    hb = _rms(x_ref[...], g_ref[...]).astype(BF16)
    cos = cos_ref[...]
    sin = sin_ref[...]
    kk = jnp.dot(hb, wk_ref[...], preferred_element_type=F32)
    for h in range(A_HEADS):
        sl = slice(h * LANES, (h + 1) * LANES)
        k_ref[:, sl] = _rope(kk[:, sl], cos, sin).astype(BF16)
    vt = lax.dot_general(wvt_ref[...], hb, _NT, preferred_element_type=F32)
    ones = jnp.ones((V_ROWS - A_V_DIM, vt.shape[1]), BF16)
    for h in range(A_HEADS):
        vt_ref[0, 0, h, :A_V_DIM, :] = vt[h * A_V_DIM:(h + 1) * A_V_DIM].astype(BF16)
        vt_ref[0, 0, h, A_V_DIM:, :] = ones


def _kv_proj(x, g, w_k, w_vt, cos, sin, batch, seq):
    t = x.shape[0]
    tm = ATT_TK
    nk = seq // tm
    return pl.pallas_call(
        _kv_proj_kernel,
        grid=(t // tm,),
        in_specs=[pl.BlockSpec((tm, D_MODEL), lambda i: (i, 0)),
                  _const_spec((1, D_MODEL)),
                  _const_spec((D_MODEL, D_MODEL)),
                  _const_spec((D_MODEL, D_MODEL)),
                  pl.BlockSpec((tm, LANES), lambda i: (i, 0)),
                  pl.BlockSpec((tm, LANES), lambda i: (i, 0))],
        out_specs=[pl.BlockSpec((tm, D_MODEL), lambda i: (i, 0)),
                   pl.BlockSpec((1, 1, A_HEADS, V_ROWS, tm),
                                lambda i: (i // nk, i % nk, 0, 0, 0))],
        out_shape=[jax.ShapeDtypeStruct((t, D_MODEL), BF16),
                   jax.ShapeDtypeStruct((batch, nk, A_HEADS, V_ROWS, tm), BF16)],
        compiler_params=_params("parallel"),
        name="kv_proj",
    )(x, g, w_k, w_vt, cos, sin)


def _q_proj_kernel(x_ref, g_ref, wq_ref, cos_ref, sin_ref, q_ref):
    hb = _rms(x_ref[...], g_ref[...]).astype(BF16)
    cos = cos_ref[...]
    sin = sin_ref[...]
    qq = jnp.dot(hb, wq_ref[...], preferred_element_type=F32)
    for h in range(A_HEADS):
        sl = slice(h * LANES, (h + 1) * LANES)
        q_ref[:, sl] = _rope(qq[:, sl], cos, sin).astype(BF16)


def _q_proj(x, g, w_q, cos, sin):
    t = x.shape[0]
    tm = ROW_TILE
    return pl.pallas_call(
        _q_proj_kernel,
        grid=(t // tm,),
        in_specs=[pl.BlockSpec((tm, D_MODEL), lambda i: (i, 0)),
                  _const_spec((1, D_MODEL)),
                  _const_spec((D_MODEL, D_MODEL)),
                  pl.BlockSpec((tm, LANES), lambda i: (i, 0)),
                  pl.BlockSpec((tm, LANES), lambda i: (i, 0))],
        out_specs=pl.BlockSpec((tm, D_MODEL), lambda i: (i, 0)),
        out_shape=jax.ShapeDtypeStruct((t, D_MODEL), BF16),
        compiler_params=_params("parallel"),
        name="q_proj",
    )(x, g, w_q, cos, sin)


def _attn_kernel(lam_ref, q_ref, k_ref, vt_ref, g_ref, o_ref, sa_ref, sb_ref, acc_ref, *,
                 lam_init):
    tq = q_ref.shape[0]
    tk = vt_ref.shape[4]
    i = pl.program_id(2)

    lam = lam_ref[...]
    lam_full = (jnp.exp(jnp.sum(lam[0:1] * lam[1:2], keepdims=True))
                - jnp.exp(jnp.sum(lam[2:3] * lam[3:4], keepdims=True)) + lam_init)

    q_t = q_ref[...].astype(F32).T
    row_map = (lax.broadcasted_iota(jnp.int32, q_t.shape, 0) >> 5) & 1
    q_cat = jnp.concatenate([jnp.where(row_map == c, q_t, 0.0) for c in range(2)],
                            axis=1).astype(BF16)

    def scores(j, s_ref):
        kb = k_ref[pl.ds(pl.multiple_of(j * tk, tk), tk), :]
        s_ref[...] = jnp.dot(kb, q_cat, preferred_element_type=F32)

    def softmax_pv(j, s_ref, m_old, diagonal):
        s = s_ref[...]
        if diagonal:
            key_i = j * tk + lax.broadcasted_iota(jnp.int32, s.shape, 0)
            qry_i = i * tq + (lax.broadcasted_iota(jnp.int32, s.shape, 1) & (tq - 1))
            s = jnp.where(key_i <= qry_i, s, -jnp.inf)
        m_new = jnp.maximum(m_old, jnp.max(s, axis=0, keepdims=True))
        alpha = jnp.exp2(m_old - m_new)
        p = jnp.exp2(s - m_new).astype(BF16)
        acc_ref[...] = alpha * acc_ref[...] + jnp.dot(vt_ref[0, j, 0], p,
                                                      preferred_element_type=F32)
        return m_new

    acc_ref[...] = jnp.zeros_like(acc_ref)
    n_full = (i * tq) // tk
    scores(0, sa_ref)

    def trip(n_tiles, first):
        def body(jj, m_run):
            j = first + n_tiles * jj
            for t in range(0, n_tiles, 2):
                scores(j + t + 1, sb_ref)
                m_run = softmax_pv(j + t, sa_ref, m_run, False)
                scores(j + t + 2, sa_ref)
                m_run = softmax_pv(j + t + 1, sb_ref, m_run, False)
            return m_run
        return body

    n_long = n_full // ATT_TRIP_TILES
    m_run = lax.fori_loop(0, n_long, trip(ATT_TRIP_TILES, 0),
                          jnp.full((1, 2 * tq), -jnp.inf, F32))
    done = n_long * ATT_TRIP_TILES
    m_run = lax.fori_loop(0, (n_full - done) // 2, trip(2, done), m_run)

    def tail_odd(m_old):
        scores(n_full, sb_ref)
        m_mid = softmax_pv(n_full - 1, sa_ref, m_old, False)
        return softmax_pv(n_full, sb_ref, m_mid, True)

    def tail_even(m_old):
        return softmax_pv(n_full, sa_ref, m_old, True)

    lax.cond(n_full % 2 == 1, tail_odd, tail_even, m_run)

    acc = acc_ref[...]
    inv_l = 1.0 / acc[A_V_DIM:A_V_DIM + 1]
    o = acc[:A_V_DIM, :tq] * inv_l[:, :tq] - lam_full * (acc[:A_V_DIM, tq:] * inv_l[:, tq:])
    ms = jnp.mean(o * o, axis=0, keepdims=True)
    o = o * lax.rsqrt(ms + NORM_EPS) * g_ref[...] * (1.0 - lam_init)
    o_ref[...] = o.T.astype(BF16)


def _attention(q, k, vt, lam, g_col, batch, seq, lam_init):
    tq, tk = ATT_TQ, ATT_TK
    assert tk % tq == 0 and tq & (tq - 1) == 0
    nq = seq // tq
    nk = seq // tk
    t = batch * seq
    return pl.pallas_call(
        functools.partial(_attn_kernel, lam_init=lam_init),
        grid=(batch, A_HEADS, nq),
        in_specs=[pl.BlockSpec((4, A_HEAD_DIM), lambda b, h, i: (0, 0)),
                  pl.BlockSpec((tq, LANES), lambda b, h, i: (b * nq + i, h)),
                  pl.BlockSpec((seq, LANES), lambda b, h, i: (b, h)),
                  pl.BlockSpec((1, nk, 1, V_ROWS, tk), lambda b, h, i: (b, 0, h, 0, 0)),
                  pl.BlockSpec((A_V_DIM, 1), lambda b, h, i: (h, 0))],
        out_specs=pl.BlockSpec((tq, LANES), lambda b, h, i: (b * nq + i, h)),
        out_shape=jax.ShapeDtypeStruct((t, D_MODEL), BF16),
        scratch_shapes=[pltpu.VMEM((tk, 2 * tq), F32),
                        pltpu.VMEM((tk, 2 * tq), F32),
                        pltpu.VMEM((V_ROWS, 2 * tq), F32)],
        compiler_params=_params("parallel", "parallel", "arbitrary"),
        name="diff_attention",
    )(lam, q, k, vt, g_col)


def _head_perm(w):
    half = A_HEAD_DIM // 2
    w = w.reshape(w.shape[0], A_HEADS, 2, 2, half)
    return w.transpose(0, 1, 3, 2, 4).reshape(w.shape[0], D_MODEL)


def kernel(x, positions, norm_g, a_w_in, a_b_gates, a_g_head, a_w_out, kv_norm_g, w_kv,
           b_w_q, b_lam, b_g_head, b_w_out, mlp_w_up, mlp_w_down):
    batch, seq, _ = x.shape
    t = batch * seq
    depth = norm_g.shape[0]
    n_a = a_w_in.shape[0]
    xf = x.reshape(t, D_MODEL)
    row = lambda v: v.reshape(1, -1).astype(F32)

    cos, sin = _rope_tables(positions.reshape(t, 1))
    k_sh = vt_sh = None
    for layer in range(depth):
        g = norm_g[layer]
        if layer < n_a:
            w_in = a_w_in[layer]
            qkvo_cols = 2 * M_QK_COLS + 2 * D_MODEL
            col_scale = jnp.where((jnp.arange(qkvo_cols) >= M_QK_COLS)
                                  & (jnp.arange(qkvo_cols) < 2 * M_QK_COLS),
                                  M_QK_DIM ** -0.5, 1.0).astype(F32)
            w_qkvo = (w_in[:, :qkvo_cols] * col_scale).astype(BF16)
            w_gt = w_in[:, qkvo_cols:].T.astype(BF16)
            qkvo, gates = _a_proj(xf, row(g[0]), w_qkvo, w_gt,
                                  a_b_gates[layer].reshape(-1, 1).astype(F32))
            y = _mlstm(qkvo, gates.reshape(2 * M_HEADS, 1, t), row(a_g_head[layer]),
                       batch, seq)
            w_out = a_w_out[layer]
        else:
            j = layer - n_a
            if j == 0:
                w_k = _head_perm(w_kv[:, :D_MODEL]).astype(BF16)
                w_vt = w_kv[:, D_MODEL:].T.astype(BF16)
                k_sh, vt_sh = _kv_proj(xf, row(kv_norm_g), w_k, w_vt, cos, sin, batch, seq)
            lam_init = 0.8 - 0.6 * math.exp(-0.3 * layer)
            w_q = (_head_perm(b_w_q[j]) * (A_HEAD_DIM ** -0.5 * math.log2(math.e))).astype(BF16)
            q = _q_proj(xf, row(g[0]), w_q, cos, sin)
            y = _attention(q, k_sh, vt_sh, b_lam[j].astype(F32),
                           b_g_head[j].reshape(-1, 1).astype(F32), batch, seq, lam_init)
            w_out = b_w_out[j]
        xf = _post(xf, y, w_out.astype(BF16), row(g[1]), row(g[2]), row(g[3]),
                   mlp_w_up[layer].astype(BF16), mlp_w_down[layer].astype(BF16))
    return xf.reshape(batch, seq, D_MODEL)
```

```python
import functools
import math

import jax
import jax.numpy as jnp
from jax import lax
from jax.experimental import pallas as pl
from jax.experimental.pallas import tpu as pltpu

F32 = jnp.float32
BF16 = jnp.bfloat16

D_MODEL = 1024
D_FF = 4 * D_MODEL
NORM_EPS = 1e-6

M_HEADS = 4
M_V_DIM = D_MODEL // M_HEADS
M_QK_DIM = M_V_DIM // 2
M_QK_COLS = M_HEADS * M_QK_DIM
GATE_SOFTCAP = 15.0
M_CHUNK = 256

A_HEADS = 8
A_HEAD_DIM = D_MODEL // (2 * A_HEADS)
A_V_DIM = 2 * A_HEAD_DIM
ROPE_THETA = 10000.0

LANES = 128
VMEM_LIMIT_BYTES = 56 * 1024 * 1024

ROW_TILE = 512
ATT_TQ = 512
ATT_TK = 512
ATT_TRIP_TILES = (8, 4, 2)
V_ROWS = A_V_DIM + 16

_NT = (((1,), (1,)), ((), ()))
_TN = (((0,), (0,)), ((), ()))


def _params(*sem):
    return pltpu.CompilerParams(dimension_semantics=sem, vmem_limit_bytes=VMEM_LIMIT_BYTES)


def _rms(x, g):
    ms = jnp.mean(x * x, axis=-1, keepdims=True)
    return x * lax.rsqrt(ms + NORM_EPS) * g


def _const_spec(shape):
    nd = len(shape)
    return pl.BlockSpec(shape, lambda *_: (0,) * nd, pipeline_mode=pl.Buffered(1))


def _rope_table_kernel(pos_ref, inv_ref, sign_ref, cos_ref, sin_ref):
    ang = pos_ref[...].astype(F32) * inv_ref[...]
    cos_ref[...] = jnp.cos(ang)
    sin_ref[...] = jnp.sin(ang) * sign_ref[...]


def _rope_tables(pos):
    t = pos.shape[0]
    half = A_HEAD_DIM // 2
    inv = 1.0 / (ROPE_THETA ** (jnp.arange(0, A_HEAD_DIM, 2, dtype=F32) / A_HEAD_DIM))
    inv = jnp.tile(inv, LANES // half).reshape(1, LANES)
    sign = jnp.where(jnp.arange(LANES) < LANES // 2, -1.0, 1.0).astype(F32).reshape(1, LANES)
    tm = ROW_TILE
    return pl.pallas_call(
        _rope_table_kernel,
        grid=(t // tm,),
        in_specs=[pl.BlockSpec((tm, 1), lambda i: (i, 0)),
                  _const_spec((1, LANES)), _const_spec((1, LANES))],
        out_specs=[pl.BlockSpec((tm, LANES), lambda i: (i, 0))] * 2,
        out_shape=[jax.ShapeDtypeStruct((t, LANES), F32)] * 2,
        compiler_params=_params("parallel"),
        name="rope_tables",
    )(pos, inv, sign)


def _rope(t, cos, sin):
    return t * cos + pltpu.roll(t, LANES // 2, axis=1) * sin


def _a_proj_kernel(x_ref, g_ref, w_ref, wgt_ref, bg_ref, qkvo_ref, gates_ref):
    hb = _rms(x_ref[...], g_ref[...]).astype(BF16)
    n_out = qkvo_ref.shape[1]
    for n in range(0, n_out, 512):
        qkvo_ref[:, n:n + 512] = jnp.dot(
            hb, w_ref[:, n:n + 512], preferred_element_type=F32).astype(BF16)
    gr = lax.dot_general(wgt_ref[...], hb, _NT, preferred_element_type=F32) + bg_ref[...]
    gr = GATE_SOFTCAP * jnp.tanh(gr / GATE_SOFTCAP)
    log_sig = jnp.minimum(gr, 0.0) - jnp.log1p(jnp.exp(-jnp.abs(gr)))
    row = lax.broadcasted_iota(jnp.int32, gr.shape, 0)
    gates_ref[...] = jnp.where(row < M_HEADS, gr, log_sig)


def _a_proj(x, g, w_qkvo, w_gt, b_g):
    t = x.shape[0]
    tm = ROW_TILE
    n_out = w_qkvo.shape[1]
    return pl.pallas_call(
        _a_proj_kernel,
        grid=(t // tm,),
        in_specs=[pl.BlockSpec((tm, D_MODEL), lambda i: (i, 0)),
                  _const_spec((1, D_MODEL)),
                  _const_spec((D_MODEL, n_out)),
                  _const_spec((2 * M_HEADS, D_MODEL)),
                  _const_spec((2 * M_HEADS, 1))],
        out_specs=[pl.BlockSpec((tm, n_out), lambda i: (i, 0)),
                   pl.BlockSpec((2 * M_HEADS, tm), lambda i: (0, i))],
        out_shape=[jax.ShapeDtypeStruct((t, n_out), BF16),
                   jax.ShapeDtypeStruct((2 * M_HEADS, t), F32)],
        compiler_params=_params("parallel"),
        name="mlstm_proj",
    )(x, g, w_qkvo, w_gt, b_g)


def _mlstm_kernel(q_ref, k_ref, v_ref, o_ref, li_ref, lf_ref, gh_ref, y_ref,
                  c_ref, n_ref, m_ref):
    L = q_ref.shape[0]

    @pl.when(pl.program_id(2) == 0)
    def _():
        c_ref[...] = jnp.zeros_like(c_ref)
        n_ref[...] = jnp.zeros_like(n_ref)
        m_ref[...] = jnp.zeros_like(m_ref)

    q = q_ref[...]
    k = k_ref[...]
    v = v_ref[...]
    li_row = li_ref[0]
    lf_row = lf_ref[0]
    m_prev = m_ref[...]
    c_prev = c_ref[...]
    n_prev = n_ref[...]

    jj = lax.broadcasted_iota(jnp.int32, (L, L), 0)
    ss = lax.broadcasted_iota(jnp.int32, (L, L), 1)
    causal = ss <= jj
    upper = (jj <= ss).astype(F32)
    b_row = jnp.dot(jnp.broadcast_to(lf_row, (8, L)), upper,
                    preferred_element_type=F32, precision=lax.Precision.HIGHEST)[0:1]
    b_col = jnp.sum(jnp.where(causal, lf_row, 0.0), axis=1, keepdims=True)
    li_col = jnp.sum(jnp.where(jj == ss, li_row, 0.0), axis=1, keepdims=True)

    r_row = li_row - b_row
    r_mat = jnp.where(causal, r_row, -jnp.inf)
    m_col = jnp.maximum(jnp.max(r_mat, axis=1, keepdims=True), m_prev)
    w_intra = jnp.exp(r_mat - m_col)
    s_inter = jnp.exp(m_prev - m_col)

    s_qk = lax.dot_general(q, k, _NT, preferred_element_type=F32) * w_intra
    q_c = jnp.dot(q, c_prev.astype(BF16), preferred_element_type=F32)
    num = s_inter * q_c + jnp.dot(s_qk.astype(BF16), v, preferred_element_type=F32)
    q_n = jnp.sum(q.astype(F32) * n_prev, axis=1, keepdims=True)
    den = s_inter * q_n + jnp.sum(s_qk, axis=1, keepdims=True)
    floor = jnp.exp(-(b_col + m_col))
    h_out = num * (1.0 / jnp.maximum(jnp.abs(den), floor))

    b_last = b_row[:, L - 1:L]
    m_last = jnp.maximum(jnp.max(r_row, axis=1, keepdims=True), m_prev)
    decay = jnp.exp(m_prev - m_last)
    w_col = jnp.exp(li_col - b_col - m_last)
    kw = k.astype(F32) * w_col
    c_ref[...] = decay * c_prev + lax.dot_general(kw.astype(BF16), v, _TN,
                                                  preferred_element_type=F32)
    n_ref[...] = decay * n_prev + jnp.sum(kw, axis=0, keepdims=True)
    m_ref[...] = b_last + m_last

    hn = _rms(h_out, gh_ref[...])
    og = jax.nn.sigmoid(o_ref[...].astype(F32))
    y_ref[...] = (og * hn).astype(BF16)


def _mlstm(qkvo, gates, g_head, batch, seq):
    L = M_CHUNK
    nc = seq // L
    t = batch * seq
    qk_blocks = M_QK_COLS // M_QK_DIM
    v_off = 2 * M_QK_COLS // M_V_DIM
    o_off = v_off + M_HEADS
    row = lambda b, h, c: b * nc + c
    return pl.pallas_call(
        _mlstm_kernel,
        grid=(batch, M_HEADS, nc),
        in_specs=[pl.BlockSpec((L, M_QK_DIM), lambda b, h, c: (row(b, h, c), h)),
                  pl.BlockSpec((L, M_QK_DIM), lambda b, h, c: (row(b, h, c), qk_blocks + h)),
                  pl.BlockSpec((L, M_V_DIM), lambda b, h, c: (row(b, h, c), v_off + h)),
                  pl.BlockSpec((L, M_V_DIM), lambda b, h, c: (row(b, h, c), o_off + h)),
                  pl.BlockSpec((1, 1, L), lambda b, h, c: (h, 0, row(b, h, c))),
                  pl.BlockSpec((1, 1, L), lambda b, h, c: (M_HEADS + h, 0, row(b, h, c))),
                  pl.BlockSpec((1, M_V_DIM), lambda b, h, c: (0, h))],
        out_specs=pl.BlockSpec((L, M_V_DIM), lambda b, h, c: (row(b, h, c), h)),
        out_shape=jax.ShapeDtypeStruct((t, D_MODEL), BF16),
        scratch_shapes=[pltpu.VMEM((M_QK_DIM, M_V_DIM), F32),
                        pltpu.VMEM((1, M_QK_DIM), F32),
                        pltpu.VMEM((1, 1), F32)],
        compiler_params=_params("parallel", "parallel", "arbitrary"),
        name="mlstm_mixer",
    )(qkvo, qkvo, qkvo, qkvo, gates, gates, g_head)


def _post_kernel(x_ref, y_ref, wo_ref, g1_ref, g2_ref, g3_ref, wu_ref, wd_ref, out_ref):
    z = jnp.dot(y_ref[...], wo_ref[...], preferred_element_type=F32)
    x1 = x_ref[...] + _rms(z, g1_ref[...])
    hb = _rms(x1, g2_ref[...]).astype(BF16)
    acc = jnp.zeros(x1.shape, F32)
    for f in range(0, D_FF, 1024):
        u = jnp.maximum(jnp.dot(hb, wu_ref[:, f:f + 1024], preferred_element_type=F32), 0.0)
        acc = acc + jnp.dot((u * u).astype(BF16), wd_ref[f:f + 1024, :],
                            preferred_element_type=F32)
    out_ref[...] = x1 + _rms(acc, g3_ref[...])


def _post(x, y, w_out, g1, g2, g3, w_up, w_down):
    t = x.shape[0]
    tm = ROW_TILE
    return pl.pallas_call(
        _post_kernel,
        grid=(t // tm,),
        in_specs=[pl.BlockSpec((tm, D_MODEL), lambda i: (i, 0)),
                  pl.BlockSpec((tm, D_MODEL), lambda i: (i, 0)),
                  _const_spec((D_MODEL, D_MODEL)),
                  _const_spec((1, D_MODEL)), _const_spec((1, D_MODEL)), _const_spec((1, D_MODEL)),
                  _const_spec((D_MODEL, D_FF)),
                  _const_spec((D_FF, D_MODEL))],
        out_specs=pl.BlockSpec((tm, D_MODEL), lambda i: (i, 0)),
        out_shape=jax.ShapeDtypeStruct((t, D_MODEL), F32),
        compiler_params=_params("parallel"),
        name="outproj_mlp",
    )(x, y, w_out, g1, g2, g3, w_up, w_down)


def _kv_proj_kernel(x_ref, g_ref, wk_ref, wvt_ref, cos_ref, sin_ref, k_ref, vt_ref):
    hb = _rms(x_ref[...], g_ref[...]).astype(BF16)
    cos = cos_ref[...]
    sin = sin_ref[...]
    kk = jnp.dot(hb, wk_ref[...], preferred_element_type=F32)
    for h in range(A_HEADS):
        sl = slice(h * LANES, (h + 1) * LANES)
        k_ref[:, sl] = _rope(kk[:, sl], cos, sin).astype(BF16)
    vt = lax.dot_general(wvt_ref[...], hb, _NT, preferred_element_type=F32)
    ones = jnp.ones((V_ROWS - A_V_DIM, vt.shape[1]), BF16)
    for h in range(A_HEADS):
        vt_ref[0, 0, h, :A_V_DIM, :] = vt[h * A_V_DIM:(h + 1) * A_V_DIM].astype(BF16)
        vt_ref[0, 0, h, A_V_DIM:, :] = ones


def _kv_proj(x, g, w_k, w_vt, cos, sin, batch, seq):
    t = x.shape[0]
    tm = ATT_TK
    nk = seq // tm
    return pl.pallas_call(
        _kv_proj_kernel,
        grid=(t // tm,),
        in_specs=[pl.BlockSpec((tm, D_MODEL), lambda i: (i, 0)),
                  _const_spec((1, D_MODEL)),
                  _const_spec((D_MODEL, D_MODEL)),
                  _const_spec((D_MODEL, D_MODEL)),
                  pl.BlockSpec((tm, LANES), lambda i: (i, 0)),
                  pl.BlockSpec((tm, LANES), lambda i: (i, 0))],
        out_specs=[pl.BlockSpec((tm, D_MODEL), lambda i: (i, 0)),
                   pl.BlockSpec((1, 1, A_HEADS, V_ROWS, tm),
                                lambda i: (i // nk, i % nk, 0, 0, 0))],
        out_shape=[jax.ShapeDtypeStruct((t, D_MODEL), BF16),
                   jax.ShapeDtypeStruct((batch, nk, A_HEADS, V_ROWS, tm), BF16)],
        compiler_params=_params("parallel"),
        name="kv_proj",
    )(x, g, w_k, w_vt, cos, sin)


def _q_proj_kernel(x_ref, g_ref, wq_ref, cos_ref, sin_ref, q_ref):
    hb = _rms(x_ref[...], g_ref[...]).astype(BF16)
    cos = cos_ref[...]
    sin = sin_ref[...]
    qq = jnp.dot(hb, wq_ref[...], preferred_element_type=F32)
    for h in range(A_HEADS):
        sl = slice(h * LANES, (h + 1) * LANES)
        q_ref[:, sl] = _rope(qq[:, sl], cos, sin).astype(BF16)


def _q_proj(x, g, w_q, cos, sin):
    t = x.shape[0]
    tm = ROW_TILE
    return pl.pallas_call(
        _q_proj_kernel,
        grid=(t // tm,),
        in_specs=[pl.BlockSpec((tm, D_MODEL), lambda i: (i, 0)),
                  _const_spec((1, D_MODEL)),
                  _const_spec((D_MODEL, D_MODEL)),
                  pl.BlockSpec((tm, LANES), lambda i: (i, 0)),
                  pl.BlockSpec((tm, LANES), lambda i: (i, 0))],
        out_specs=pl.BlockSpec((tm, D_MODEL), lambda i: (i, 0)),
        out_shape=jax.ShapeDtypeStruct((t, D_MODEL), BF16),
        compiler_params=_params("parallel"),
        name="q_proj",
    )(x, g, w_q, cos, sin)


def _attn_kernel(lam_ref, q_ref, k_ref, vt_ref, g_ref, o_ref, sa_ref, sb_ref, acc_ref, *,
                 lam_init):
    tq = q_ref.shape[0]
    tk = vt_ref.shape[4]
    i = pl.program_id(2)

    lam = lam_ref[...]
    lam_full = (jnp.exp(jnp.sum(lam[0:1] * lam[1:2], keepdims=True))
                - jnp.exp(jnp.sum(lam[2:3] * lam[3:4], keepdims=True)) + lam_init)

    q_t = q_ref[...].astype(F32).T
    row_map = (lax.broadcasted_iota(jnp.int32, q_t.shape, 0) >> 5) & 1
    q_cat = jnp.concatenate([jnp.where(row_map == c, q_t, 0.0) for c in range(2)],
                            axis=1).astype(BF16)

    def scores(j, s_ref):
        kb = k_ref[pl.ds(pl.multiple_of(j * tk, tk), tk), :]
        s_ref[...] = jnp.dot(kb, q_cat, preferred_element_type=F32)

    def softmax_pv(j, s_ref, m_old, diagonal):
        s = s_ref[...]
        if diagonal:
            key_i = j * tk + lax.broadcasted_iota(jnp.int32, s.shape, 0)
            qry_i = i * tq + (lax.broadcasted_iota(jnp.int32, s.shape, 1) & (tq - 1))
            s = jnp.where(key_i <= qry_i, s, -jnp.inf)
        m_new = jnp.maximum(m_old, jnp.max(s, axis=0, keepdims=True))
        alpha = jnp.exp2(m_old - m_new)
        p = jnp.exp2(s - m_new).astype(BF16)
        acc_ref[...] = alpha * acc_ref[...] + jnp.dot(vt_ref[0, j, 0], p,
                                                      preferred_element_type=F32)
        return m_new

    acc_ref[...] = jnp.zeros_like(acc_ref)
    n_full = (i * tq) // tk
    scores(0, sa_ref)

    def trip(n_tiles, first):
        def body(jj, m_run):
            j = first + n_tiles * jj
            for t in range(0, n_tiles, 2):
                scores(j + t + 1, sb_ref)
                m_run = softmax_pv(j + t, sa_ref, m_run, False)
                scores(j + t + 2, sa_ref)
                m_run = softmax_pv(j + t + 1, sb_ref, m_run, False)
            return m_run
        return body

    m_run = jnp.full((1, 2 * tq), -jnp.inf, F32)
    done = 0
    for n_tiles in ATT_TRIP_TILES:
        n_trips = (n_full - done) // n_tiles
        m_run = lax.fori_loop(0, n_trips, trip(n_tiles, done), m_run)
        done = done + n_trips * n_tiles

    def tail_odd(m_old):
        scores(n_full, sb_ref)
        m_mid = softmax_pv(n_full - 1, sa_ref, m_old, False)
        return softmax_pv(n_full, sb_ref, m_mid, True)

    def tail_even(m_old):
        return softmax_pv(n_full, sa_ref, m_old, True)

    lax.cond(n_full % 2 == 1, tail_odd, tail_even, m_run)

    acc = acc_ref[...]
    inv_l = 1.0 / acc[A_V_DIM:A_V_DIM + 1]
    o = acc[:A_V_DIM, :tq] * inv_l[:, :tq] - lam_full * (acc[:A_V_DIM, tq:] * inv_l[:, tq:])
    ms = jnp.mean(o * o, axis=0, keepdims=True)
    o = o * lax.rsqrt(ms + NORM_EPS) * g_ref[...] * (1.0 - lam_init)
    o_ref[...] = o.T.astype(BF16)


def _attention(q, k, vt, lam, g_col, batch, seq, lam_init):
    tq, tk = ATT_TQ, ATT_TK
    assert tk % tq == 0 and tq & (tq - 1) == 0
    nq = seq // tq
    nk = seq // tk
    t = batch * seq
    return pl.pallas_call(
        functools.partial(_attn_kernel, lam_init=lam_init),
        grid=(batch, A_HEADS, nq),
        in_specs=[pl.BlockSpec((4, A_HEAD_DIM), lambda b, h, i: (0, 0)),
                  pl.BlockSpec((tq, LANES), lambda b, h, i: (b * nq + i, h)),
                  pl.BlockSpec((seq, LANES), lambda b, h, i: (b, h)),
                  pl.BlockSpec((1, nk, 1, V_ROWS, tk), lambda b, h, i: (b, 0, h, 0, 0)),
                  pl.BlockSpec((A_V_DIM, 1), lambda b, h, i: (h, 0))],
        out_specs=pl.BlockSpec((tq, LANES), lambda b, h, i: (b * nq + i, h)),
        out_shape=jax.ShapeDtypeStruct((t, D_MODEL), BF16),
        scratch_shapes=[pltpu.VMEM((tk, 2 * tq), F32),
                        pltpu.VMEM((tk, 2 * tq), F32),
                        pltpu.VMEM((V_ROWS, 2 * tq), F32)],
        compiler_params=_params("parallel", "parallel", "arbitrary"),
        name="diff_attention",
    )(lam, q, k, vt, g_col)


def _head_perm(w):
    half = A_HEAD_DIM // 2
    w = w.reshape(w.shape[0], A_HEADS, 2, 2, half)
    return w.transpose(0, 1, 3, 2, 4).reshape(w.shape[0], D_MODEL)


def kernel(x, positions, norm_g, a_w_in, a_b_gates, a_g_head, a_w_out, kv_norm_g, w_kv,
           b_w_q, b_lam, b_g_head, b_w_out, mlp_w_up, mlp_w_down):
    batch, seq, _ = x.shape
    t = batch * seq
    depth = norm_g.shape[0]
    n_a = a_w_in.shape[0]
    xf = x.reshape(t, D_MODEL)
    row = lambda v: v.reshape(1, -1).astype(F32)

    cos, sin = _rope_tables(positions.reshape(t, 1))
    k_sh = vt_sh = None
    for layer in range(depth):
        g = norm_g[layer]
        if layer < n_a:
            w_in = a_w_in[layer]
            qkvo_cols = 2 * M_QK_COLS + 2 * D_MODEL
            col_scale = jnp.where((jnp.arange(qkvo_cols) >= M_QK_COLS)
                                  & (jnp.arange(qkvo_cols) < 2 * M_QK_COLS),
                                  M_QK_DIM ** -0.5, 1.0).astype(F32)
            w_qkvo = (w_in[:, :qkvo_cols] * col_scale).astype(BF16)
            w_gt = w_in[:, qkvo_cols:].T.astype(BF16)
            qkvo, gates = _a_proj(xf, row(g[0]), w_qkvo, w_gt,
                                  a_b_gates[layer].reshape(-1, 1).astype(F32))
            y = _mlstm(qkvo, gates.reshape(2 * M_HEADS, 1, t), row(a_g_head[layer]),
                       batch, seq)
            w_out = a_w_out[layer]
        else:
            j = layer - n_a
            if j == 0:
                w_k = _head_perm(w_kv[:, :D_MODEL]).astype(BF16)
                w_vt = w_kv[:, D_MODEL:].T.astype(BF16)
                k_sh, vt_sh = _kv_proj(xf, row(kv_norm_g), w_k, w_vt, cos, sin, batch, seq)
            lam_init = 0.8 - 0.6 * math.exp(-0.3 * layer)
            w_q = (_head_perm(b_w_q[j]) * (A_HEAD_DIM ** -0.5 * math.log2(math.e))).astype(BF16)
            q = _q_proj(xf, row(g[0]), w_q, cos, sin)
            y = _attention(q, k_sh, vt_sh, b_lam[j].astype(F32),
                           b_g_head[j].reshape(-1, 1).astype(F32), batch, seq, lam_init)
            w_out = b_w_out[j]
        xf = _post(xf, y, w_out.astype(BF16), row(g[1]), row(g[2]), row(g[3]),
                   mlp_w_up[layer].astype(BF16), mlp_w_down[layer].astype(BF16))
    return xf.reshape(batch, seq, D_MODEL)
```

```python
import functools
import math

import jax
import jax.numpy as jnp
from jax import lax
from jax.experimental import pallas as pl
from jax.experimental.pallas import tpu as pltpu

F32 = jnp.float32
BF16 = jnp.bfloat16

D_MODEL = 1024
D_FF = 4 * D_MODEL
NORM_EPS = 1e-6

M_HEADS = 4
M_V_DIM = D_MODEL // M_HEADS
M_QK_DIM = M_V_DIM // 2
M_QK_COLS = M_HEADS * M_QK_DIM
GATE_SOFTCAP = 15.0
M_CHUNK = 256

A_HEADS = 8
A_HEAD_DIM = D_MODEL // (2 * A_HEADS)
A_V_DIM = 2 * A_HEAD_DIM
ROPE_THETA = 10000.0

LANES = 128
VMEM_LIMIT_BYTES = 56 * 1024 * 1024

ROW_TILE = 512
ATT_TQ = 512
ATT_TK = 512
ATT_TRIP_TILES = (8, 4, 2)
V_ROWS = A_V_DIM + 16

_NT = (((1,), (1,)), ((), ()))
_TN = (((0,), (0,)), ((), ()))


def _params(*sem):
    return pltpu.CompilerParams(dimension_semantics=sem, vmem_limit_bytes=VMEM_LIMIT_BYTES)


def _rms(x, g):
    ms = jnp.mean(x * x, axis=-1, keepdims=True)
    return x * lax.rsqrt(ms + NORM_EPS) * g


def _const_spec(shape):
    nd = len(shape)
    return pl.BlockSpec(shape, lambda *_: (0,) * nd, pipeline_mode=pl.Buffered(1))


def _rope_table_kernel(pos_ref, inv_ref, sign_ref, cos_ref, sin_ref):
    ang = pos_ref[...].astype(F32) * inv_ref[...]
    cos_ref[...] = jnp.cos(ang)
    sin_ref[...] = jnp.sin(ang) * sign_ref[...]


def _rope_tables(pos):
    t = pos.shape[0]
    half = A_HEAD_DIM // 2
    inv = 1.0 / (ROPE_THETA ** (jnp.arange(0, A_HEAD_DIM, 2, dtype=F32) / A_HEAD_DIM))
    inv = jnp.tile(inv, LANES // half).reshape(1, LANES)
    sign = jnp.where(jnp.arange(LANES) < LANES // 2, -1.0, 1.0).astype(F32).reshape(1, LANES)
    tm = ROW_TILE
    return pl.pallas_call(
        _rope_table_kernel,
        grid=(t // tm,),
        in_specs=[pl.BlockSpec((tm, 1), lambda i: (i, 0)),
                  _const_spec((1, LANES)), _const_spec((1, LANES))],
        out_specs=[pl.BlockSpec((tm, LANES), lambda i: (i, 0))] * 2,
        out_shape=[jax.ShapeDtypeStruct((t, LANES), F32)] * 2,
        compiler_params=_params("parallel"),
        name="rope_tables",
    )(pos, inv, sign)


def _rope(t, cos, sin):
    return t * cos + pltpu.roll(t, LANES // 2, axis=1) * sin


def _a_proj_kernel(x_ref, g_ref, w_ref, wkt_ref, wgt_ref, bg_ref, voq_ref, kt_ref, gates_ref):
    hb = _rms(x_ref[...], g_ref[...]).astype(BF16)
    n_out = voq_ref.shape[1]
    for n in range(0, n_out, 512):
        voq_ref[:, n:n + 512] = jnp.dot(
            hb, w_ref[:, n:n + 512], preferred_element_type=F32).astype(BF16)
    kt_ref[...] = lax.dot_general(wkt_ref[...], hb, _NT,
                                  preferred_element_type=F32).astype(BF16)
    gr = lax.dot_general(wgt_ref[...], hb, _NT, preferred_element_type=F32) + bg_ref[...]
    gr = GATE_SOFTCAP * jnp.tanh(gr / GATE_SOFTCAP)
    log_sig = jnp.minimum(gr, 0.0) - jnp.log1p(jnp.exp(-jnp.abs(gr)))
    row = lax.broadcasted_iota(jnp.int32, gr.shape, 0)
    gates_ref[...] = jnp.where(row < M_HEADS, gr, log_sig)


def _a_proj(x, g, w_voq, w_kt, w_gt, b_g):
    t = x.shape[0]
    tm = ROW_TILE
    n_out = w_voq.shape[1]
    return pl.pallas_call(
        _a_proj_kernel,
        grid=(t // tm,),
        in_specs=[pl.BlockSpec((tm, D_MODEL), lambda i: (i, 0)),
                  _const_spec((1, D_MODEL)),
                  _const_spec((D_MODEL, n_out)),
                  _const_spec((M_QK_COLS, D_MODEL)),
                  _const_spec((2 * M_HEADS, D_MODEL)),
                  _const_spec((2 * M_HEADS, 1))],
        out_specs=[pl.BlockSpec((tm, n_out), lambda i: (i, 0)),
                   pl.BlockSpec((M_QK_COLS, tm), lambda i: (0, i)),
                   pl.BlockSpec((2 * M_HEADS, tm), lambda i: (0, i))],
        out_shape=[jax.ShapeDtypeStruct((t, n_out), BF16),
                   jax.ShapeDtypeStruct((M_QK_COLS, t), BF16),
                   jax.ShapeDtypeStruct((2 * M_HEADS, t), F32)],
        compiler_params=_params("parallel"),
        name="mlstm_proj",
    )(x, g, w_voq, w_kt, w_gt, b_g)


def _mlstm_kernel(q_ref, kt_ref, v_ref, o_ref, g_ref, gh_ref, y_ref, c_ref, m_ref):
    L = q_ref.shape[0]

    @pl.when(pl.program_id(1) == 0)
    def _():
        c_ref[...] = jnp.zeros_like(c_ref)
        m_ref[...] = jnp.zeros_like(m_ref)

    jj = lax.broadcasted_iota(jnp.int32, (L, L), 0)
    ss = lax.broadcasted_iota(jnp.int32, (L, L), 1)
    causal = ss <= jj
    upper = (jj <= ss).astype(F32)

    gates = g_ref[...]
    cums = jnp.dot(gates, upper, preferred_element_type=F32, precision=lax.Precision.HIGHEST)
    b8 = pltpu.roll(cums, M_HEADS, axis=0)
    r8 = gates - b8
    lane = lax.broadcasted_iota(jnp.int32, r8.shape, 1)
    cm8 = r8
    shift = 1
    while shift < L:
        cm8 = jnp.maximum(cm8, jnp.where(lane >= shift, pltpu.roll(cm8, shift, axis=1), -jnp.inf))
        shift *= 2
    stack = jnp.concatenate([cm8, b8, jnp.zeros((LANES - 16, L), F32)], axis=0)
    cols = stack.T
    lane1 = lax.broadcasted_iota(jnp.int32, (1, LANES), 1)
    m_prev = [m_ref[h] for h in range(M_HEADS)]
    m_lanes = sum(jnp.where(lane1 == h, m_prev[h], 0.0) for h in range(M_HEADS))
    m_cols = jnp.maximum(cols, m_lanes)
    b_cols = pltpu.roll(cols, LANES - 8, axis=1)
    is_head = lane1 < M_HEADS
    s_inter = jnp.where(is_head, jnp.exp(m_lanes - m_cols), 0.0).astype(BF16)
    floor = jnp.where(is_head, jnp.exp(-(b_cols + m_cols)), 0.0).astype(BF16)
    spread = (lax.broadcasted_iota(jnp.int32, (LANES, M_HEADS * LANES), 0)
              == lax.broadcasted_iota(jnp.int32, (LANES, M_HEADS * LANES), 1) // LANES
              ).astype(BF16)
    s_inter = jnp.dot(s_inter, spread, preferred_element_type=F32)
    floor = jnp.dot(floor, spread, preferred_element_type=F32)
    ones_blk = jnp.ones((L, LANES), BF16)
    mean_blk = jnp.full((M_V_DIM, LANES), 1.0 / M_V_DIM, BF16)

    for h in range(M_HEADS):
        lanes_h = slice(h * LANES, (h + 1) * LANES)
        q = q_ref[:, lanes_h]
        kt = kt_ref[lanes_h, :]
        v_ext = jnp.concatenate([v_ref[:, h * M_V_DIM:(h + 1) * M_V_DIM], ones_blk], axis=1)
        c_prev = c_ref[h]
        r_row = r8[h:h + 1]

        w_intra = jnp.exp(jnp.where(causal, r_row, -jnp.inf) - m_cols[:, h:h + 1])
        s_qk = jnp.dot(q, kt, preferred_element_type=F32) * w_intra
        s_in = s_inter[:, lanes_h]
        ext = (jnp.concatenate([s_in, s_in, s_in], axis=1)
               * jnp.dot(q, c_prev.astype(BF16), preferred_element_type=F32)
               + jnp.dot(s_qk.astype(BF16), v_ext, preferred_element_type=F32))
        den = ext[:, M_V_DIM:]
        inv = 1.0 / jnp.maximum(jnp.abs(den), floor[:, lanes_h])
        h_out = ext[:, :M_V_DIM] * jnp.concatenate([inv, inv], axis=1)

        m_last = jnp.maximum(cm8[h:h + 1, L - 1:L], m_prev[h])
        decay = jnp.exp(m_prev[h] - m_last)
        kwt = (kt.astype(F32) * jnp.exp(r_row - m_last)).astype(BF16)
        c_ref[h] = decay * c_prev + jnp.dot(kwt, v_ext, preferred_element_type=F32)
        m_ref[h] = b8[h:h + 1, L - 1:L] + m_last

        sl = slice(h * M_V_DIM, (h + 1) * M_V_DIM)
        ms = jnp.dot((h_out * h_out).astype(BF16), mean_blk, preferred_element_type=F32)
        rs = lax.rsqrt(ms + NORM_EPS)
        og = jax.nn.sigmoid(o_ref[:, sl].astype(F32))
        y_ref[:, sl] = (og * h_out * jnp.concatenate([rs, rs], axis=1)
                        * gh_ref[:, sl]).astype(BF16)


def _mlstm(voq, kt, gates, g_head, batch, seq):
    L = M_CHUNK
    nc = seq // L
    t = batch * seq
    row = lambda b, c: b * nc + c
    return pl.pallas_call(
        _mlstm_kernel,
        grid=(batch, nc),
        in_specs=[pl.BlockSpec((L, M_QK_COLS), lambda b, c: (row(b, c), 2 * D_MODEL // M_QK_COLS)),
                  pl.BlockSpec((M_QK_COLS, L), lambda b, c: (0, row(b, c))),
                  pl.BlockSpec((L, D_MODEL), lambda b, c: (row(b, c), 0)),
                  pl.BlockSpec((L, D_MODEL), lambda b, c: (row(b, c), 1)),
                  pl.BlockSpec((2 * M_HEADS, L), lambda b, c: (0, row(b, c))),
                  pl.BlockSpec((1, D_MODEL), lambda b, c: (0, 0))],
        out_specs=pl.BlockSpec((L, D_MODEL), lambda b, c: (row(b, c), 0)),
        out_shape=jax.ShapeDtypeStruct((t, D_MODEL), BF16),
        scratch_shapes=[pltpu.VMEM((M_HEADS, M_QK_DIM, M_V_DIM + LANES), F32),
                        pltpu.VMEM((M_HEADS, 1, 1), F32)],
        compiler_params=_params("parallel", "arbitrary"),
        name="mlstm_mixer",
    )(voq, kt, voq, voq, gates, g_head)


def _post_kernel(x_ref, y_ref, wo_ref, g1_ref, g2_ref, g3_ref, wu_ref, wd_ref, out_ref):
    z = jnp.dot(y_ref[...], wo_ref[...], preferred_element_type=F32)
    x1 = x_ref[...] + _rms(z, g1_ref[...])
    hb = _rms(x1, g2_ref[...]).astype(BF16)
    acc = jnp.zeros(x1.shape, F32)
    for f in range(0, D_FF, 1024):
        u = jnp.maximum(jnp.dot(hb, wu_ref[:, f:f + 1024], preferred_element_type=F32), 0.0)
        acc = acc + jnp.dot((u * u).astype(BF16), wd_ref[f:f + 1024, :],
                            preferred_element_type=F32)
    out_ref[...] = x1 + _rms(acc, g3_ref[...])


def _post(x, y, w_out, g1, g2, g3, w_up, w_down):
    t = x.shape[0]
    tm = ROW_TILE
    return pl.pallas_call(
        _post_kernel,
        grid=(t // tm,),
        in_specs=[pl.BlockSpec((tm, D_MODEL), lambda i: (i, 0)),
                  pl.BlockSpec((tm, D_MODEL), lambda i: (i, 0)),
                  _const_spec((D_MODEL, D_MODEL)),
                  _const_spec((1, D_MODEL)), _const_spec((1, D_MODEL)), _const_spec((1, D_MODEL)),
                  _const_spec((D_MODEL, D_FF)),
                  _const_spec((D_FF, D_MODEL))],
        out_specs=pl.BlockSpec((tm, D_MODEL), lambda i: (i, 0)),
        out_shape=jax.ShapeDtypeStruct((t, D_MODEL), F32),
        compiler_params=_params("parallel"),
        name="outproj_mlp",
    )(x, y, w_out, g1, g2, g3, w_up, w_down)


def _kv_proj_kernel(x_ref, g_ref, wk_ref, wvt_ref, cos_ref, sin_ref, k_ref, vt_ref):
    hb = _rms(x_ref[...], g_ref[...]).astype(BF16)
    cos = cos_ref[...]
    sin = sin_ref[...]
    kk = jnp.dot(hb, wk_ref[...], preferred_element_type=F32)
    for h in range(A_HEADS):
        sl = slice(h * LANES, (h + 1) * LANES)
        k_ref[:, sl] = _rope(kk[:, sl], cos, sin).astype(BF16)
    vt = lax.dot_general(wvt_ref[...], hb, _NT, preferred_element_type=F32)
    ones = jnp.ones((V_ROWS - A_V_DIM, vt.shape[1]), BF16)
    for h in range(A_HEADS):
        vt_ref[0, 0, h, :A_V_DIM, :] = vt[h * A_V_DIM:(h + 1) * A_V_DIM].astype(BF16)
        vt_ref[0, 0, h, A_V_DIM:, :] = ones


def _kv_proj(x, g, w_k, w_vt, cos, sin, batch, seq):
    t = x.shape[0]
    tm = ATT_TK
    nk = seq // tm
    return pl.pallas_call(
        _kv_proj_kernel,
        grid=(t // tm,),
        in_specs=[pl.BlockSpec((tm, D_MODEL), lambda i: (i, 0)),
                  _const_spec((1, D_MODEL)),
                  _const_spec((D_MODEL, D_MODEL)),
                  _const_spec((D_MODEL, D_MODEL)),
                  pl.BlockSpec((tm, LANES), lambda i: (i, 0)),
                  pl.BlockSpec((tm, LANES), lambda i: (i, 0))],
        out_specs=[pl.BlockSpec((tm, D_MODEL), lambda i: (i, 0)),
                   pl.BlockSpec((1, 1, A_HEADS, V_ROWS, tm),
                                lambda i: (i // nk, i % nk, 0, 0, 0))],
        out_shape=[jax.ShapeDtypeStruct((t, D_MODEL), BF16),
                   jax.ShapeDtypeStruct((batch, nk, A_HEADS, V_ROWS, tm), BF16)],
        compiler_params=_params("parallel"),
        name="kv_proj",
    )(x, g, w_k, w_vt, cos, sin)


def _q_proj_kernel(x_ref, g_ref, wq_ref, cos_ref, sin_ref, q_ref):
    hb = _rms(x_ref[...], g_ref[...]).astype(BF16)
    cos = cos_ref[...]
    sin = sin_ref[...]
    qq = jnp.dot(hb, wq_ref[...], preferred_element_type=F32)
    for h in range(A_HEADS):
        sl = slice(h * LANES, (h + 1) * LANES)
        q_ref[:, sl] = _rope(qq[:, sl], cos, sin).astype(BF16)


def _q_proj(x, g, w_q, cos, sin):
    t = x.shape[0]
    tm = ROW_TILE
    return pl.pallas_call(
        _q_proj_kernel,
        grid=(t // tm,),
        in_specs=[pl.BlockSpec((tm, D_MODEL), lambda i: (i, 0)),
                  _const_spec((1, D_MODEL)),
                  _const_spec((D_MODEL, D_MODEL)),
                  pl.BlockSpec((tm, LANES), lambda i: (i, 0)),
                  pl.BlockSpec((tm, LANES), lambda i: (i, 0))],
        out_specs=pl.BlockSpec((tm, D_MODEL), lambda i: (i, 0)),
        out_shape=jax.ShapeDtypeStruct((t, D_MODEL), BF16),
        compiler_params=_params("parallel"),
        name="q_proj",
    )(x, g, w_q, cos, sin)


def _attn_kernel(lam_ref, q_ref, k_ref, vt_ref, g_ref, o_ref, sa_ref, sb_ref, acc_ref, *,
                 lam_init):
    tq = q_ref.shape[0]
    tk = vt_ref.shape[4]
    i = pl.program_id(2)

    lam = lam_ref[...]
    lam_full = (jnp.exp(jnp.sum(lam[0:1] * lam[1:2], keepdims=True))
                - jnp.exp(jnp.sum(lam[2:3] * lam[3:4], keepdims=True)) + lam_init)

    q_t = q_ref[...].astype(F32).T
    row_map = (lax.broadcasted_iota(jnp.int32, q_t.shape, 0) >> 5) & 1
    q_cat = jnp.concatenate([jnp.where(row_map == c, q_t, 0.0) for c in range(2)],
                            axis=1).astype(BF16)

    def scores(j, s_ref):
        kb = k_ref[pl.ds(pl.multiple_of(j * tk, tk), tk), :]
        s_ref[...] = jnp.dot(kb, q_cat, preferred_element_type=F32)

    def softmax_pv(j, s_ref, m_old, diagonal):
        s = s_ref[...]
        if diagonal:
            key_i = j * tk + lax.broadcasted_iota(jnp.int32, s.shape, 0)
            qry_i = i * tq + (lax.broadcasted_iota(jnp.int32, s.shape, 1) & (tq - 1))
            s = jnp.where(key_i <= qry_i, s, -jnp.inf)
        m_new = jnp.maximum(m_old, jnp.max(s, axis=0, keepdims=True))
        alpha = jnp.exp2(m_old - m_new)
        p = jnp.exp2(s - m_new).astype(BF16)
        acc_ref[...] = alpha * acc_ref[...] + jnp.dot(vt_ref[0, j, 0], p,
                                                      preferred_element_type=F32)
        return m_new

    acc_ref[...] = jnp.zeros_like(acc_ref)
    n_full = (i * tq) // tk
    scores(0, sa_ref)

    def trip(n_tiles, first):
        def body(jj, m_run):
            j = first + n_tiles * jj
            for t in range(0, n_tiles, 2):
                scores(j + t + 1, sb_ref)
                m_run = softmax_pv(j + t, sa_ref, m_run, False)
                scores(j + t + 2, sa_ref)
                m_run = softmax_pv(j + t + 1, sb_ref, m_run, False)
            return m_run
        return body

    m_run = jnp.full((1, 2 * tq), -jnp.inf, F32)
    done = 0
    for n_tiles in ATT_TRIP_TILES:
        n_trips = (n_full - done) // n_tiles
        m_run = lax.fori_loop(0, n_trips, trip(n_tiles, done), m_run)
        done = done + n_trips * n_tiles

    def tail_odd(m_old):
        scores(n_full, sb_ref)
        m_mid = softmax_pv(n_full - 1, sa_ref, m_old, False)
        return softmax_pv(n_full, sb_ref, m_mid, True)

    def tail_even(m_old):
        return softmax_pv(n_full, sa_ref, m_old, True)

    lax.cond(n_full % 2 == 1, tail_odd, tail_even, m_run)

    acc = acc_ref[...]
    inv_l = 1.0 / acc[A_V_DIM:A_V_DIM + 1]
    o = acc[:A_V_DIM, :tq] * inv_l[:, :tq] - lam_full * (acc[:A_V_DIM, tq:] * inv_l[:, tq:])
    ms = jnp.mean(o * o, axis=0, keepdims=True)
    o = o * lax.rsqrt(ms + NORM_EPS) * g_ref[...] * (1.0 - lam_init)
    o_ref[...] = o.T.astype(BF16)


def _attention(q, k, vt, lam, g_col, batch, seq, lam_init):
    tq, tk = ATT_TQ, ATT_TK
    assert tk % tq == 0 and tq & (tq - 1) == 0
    nq = seq // tq
    nk = seq // tk
    t = batch * seq
    return pl.pallas_call(
        functools.partial(_attn_kernel, lam_init=lam_init),
        grid=(batch, A_HEADS, nq),
        in_specs=[pl.BlockSpec((4, A_HEAD_DIM), lambda b, h, i: (0, 0)),
                  pl.BlockSpec((tq, LANES), lambda b, h, i: (b * nq + i, h)),
                  pl.BlockSpec((seq, LANES), lambda b, h, i: (b, h)),
                  pl.BlockSpec((1, nk, 1, V_ROWS, tk), lambda b, h, i: (b, 0, h, 0, 0)),
                  pl.BlockSpec((A_V_DIM, 1), lambda b, h, i: (h, 0))],
        out_specs=pl.BlockSpec((tq, LANES), lambda b, h, i: (b * nq + i, h)),
        out_shape=jax.ShapeDtypeStruct((t, D_MODEL), BF16),
        scratch_shapes=[pltpu.VMEM((tk, 2 * tq), F32),
                        pltpu.VMEM((tk, 2 * tq), F32),
                        pltpu.VMEM((V_ROWS, 2 * tq), F32)],
        compiler_params=_params("parallel", "parallel", "arbitrary"),
        name="diff_attention",
    )(lam, q, k, vt, g_col)


def _head_perm(w):
    half = A_HEAD_DIM // 2
    w = w.reshape(w.shape[0], A_HEADS, 2, 2, half)
    return w.transpose(0, 1, 3, 2, 4).reshape(w.shape[0], D_MODEL)


def kernel(x, positions, norm_g, a_w_in, a_b_gates, a_g_head, a_w_out, kv_norm_g, w_kv,
           b_w_q, b_lam, b_g_head, b_w_out, mlp_w_up, mlp_w_down):
    batch, seq, _ = x.shape
    t = batch * seq
    depth = norm_g.shape[0]
    n_a = a_w_in.shape[0]
    xf = x.reshape(t, D_MODEL)
    row = lambda v: v.reshape(1, -1).astype(F32)

    cos, sin = _rope_tables(positions.reshape(t, 1))
    k_sh = vt_sh = None
    for layer in range(depth):
        g = norm_g[layer]
        if layer < n_a:
            w_in = a_w_in[layer]
            vo_end = 2 * M_QK_COLS + 2 * D_MODEL
            w_voq = jnp.concatenate([w_in[:, 2 * M_QK_COLS:vo_end], w_in[:, :M_QK_COLS]],
                                    axis=1).astype(BF16)
            w_kt = (w_in[:, M_QK_COLS:2 * M_QK_COLS] * M_QK_DIM ** -0.5).T.astype(BF16)
            w_gt = w_in[:, vo_end:].T.astype(BF16)
            voq, kt, gates = _a_proj(xf, row(g[0]), w_voq, w_kt, w_gt,
                                     a_b_gates[layer].reshape(-1, 1).astype(F32))
            y = _mlstm(voq, kt, gates, row(a_g_head[layer]), batch, seq)
            w_out = a_w_out[layer]
        else:
            j = layer - n_a
            if j == 0:
                w_k = _head_perm(w_kv[:, :D_MODEL]).astype(BF16)
                w_vt = w_kv[:, D_MODEL:].T.astype(BF16)
                k_sh, vt_sh = _kv_proj(xf, row(kv_norm_g), w_k, w_vt, cos, sin, batch, seq)
            lam_init = 0.8 - 0.6 * math.exp(-0.3 * layer)
            w_q = (_head_perm(b_w_q[j]) * (A_HEAD_DIM ** -0.5 * math.log2(math.e))).astype(BF16)
            q = _q_proj(xf, row(g[0]), w_q, cos, sin)
            y = _attention(q, k_sh, vt_sh, b_lam[j].astype(F32),
                           b_g_head[j].reshape(-1, 1).astype(F32), batch, seq, lam_init)
            w_out = b_w_out[j]
        xf = _post(xf, y, w_out.astype(BF16), row(g[1]), row(g[2]), row(g[3]),
                   mlp_w_up[layer].astype(BF16), mlp_w_down[layer].astype(BF16))
    return xf.reshape(batch, seq, D_MODEL)
```

```python
import functools
import math

import jax
import jax.numpy as jnp
from jax import lax
from jax.experimental import pallas as pl
from jax.experimental.pallas import tpu as pltpu

F32 = jnp.float32
BF16 = jnp.bfloat16

D_MODEL = 1024
D_FF = 4 * D_MODEL
NORM_EPS = 1e-6

M_HEADS = 4
M_V_DIM = D_MODEL // M_HEADS
M_QK_DIM = M_V_DIM // 2
M_QK_COLS = M_HEADS * M_QK_DIM
GATE_SOFTCAP = 15.0
M_CHUNK = 256

A_HEADS = 8
A_HEAD_DIM = D_MODEL // (2 * A_HEADS)
A_V_DIM = 2 * A_HEAD_DIM
ROPE_THETA = 10000.0

LANES = 128
VMEM_LIMIT_BYTES = 56 * 1024 * 1024

ROW_TILE = 512
ATT_TQ = 512
ATT_TK = 512
ATT_TRIP_TILES = (8, 4, 2, 1)
BOUND_MAX = 40.0
BOUND_SLACK = 1.03
V_ROWS = A_V_DIM + 16

_NT = (((1,), (1,)), ((), ()))


def _params(*sem):
    return pltpu.CompilerParams(dimension_semantics=sem, vmem_limit_bytes=VMEM_LIMIT_BYTES)


def _rms(x, g):
    ms = jnp.mean(x * x, axis=-1, keepdims=True)
    return x * lax.rsqrt(ms + NORM_EPS) * g


def _const_spec(shape):
    nd = len(shape)
    return pl.BlockSpec(shape, lambda *_: (0,) * nd, pipeline_mode=pl.Buffered(1))


def _rope_table_kernel(pos_ref, inv_ref, sign_ref, cos_ref, sin_ref):
    ang = pos_ref[...].astype(F32) * inv_ref[...]
    cos_ref[...] = jnp.cos(ang)
    sin_ref[...] = jnp.sin(ang) * sign_ref[...]


def _rope_tables(pos):
    t = pos.shape[0]
    half = A_HEAD_DIM // 2
    inv = 1.0 / (ROPE_THETA ** (jnp.arange(0, A_HEAD_DIM, 2, dtype=F32) / A_HEAD_DIM))
    inv = jnp.tile(inv, LANES // half).reshape(1, LANES)
    sign = jnp.where(jnp.arange(LANES) < LANES // 2, -1.0, 1.0).astype(F32).reshape(1, LANES)
    tm = ROW_TILE
    return pl.pallas_call(
        _rope_table_kernel,
        grid=(t // tm,),
        in_specs=[pl.BlockSpec((tm, 1), lambda i: (i, 0)),
                  _const_spec((1, LANES)), _const_spec((1, LANES))],
        out_specs=[pl.BlockSpec((tm, LANES), lambda i: (i, 0))] * 2,
        out_shape=[jax.ShapeDtypeStruct((t, LANES), F32)] * 2,
        compiler_params=_params("parallel"),
        name="rope_tables",
    )(pos, inv, sign)


def _rope(t, cos, sin):
    return t * cos + pltpu.roll(t, LANES // 2, axis=1) * sin


def _a_proj_kernel(x_ref, g_ref, w_ref, wkt_ref, wgt_ref, bg_ref, voq_ref, kt_ref, gates_ref):
    hb = _rms(x_ref[...], g_ref[...]).astype(BF16)
    n_out = voq_ref.shape[1]
    for n in range(0, n_out, 512):
        voq_ref[:, n:n + 512] = jnp.dot(
            hb, w_ref[:, n:n + 512], preferred_element_type=F32).astype(BF16)
    kt_ref[...] = lax.dot_general(wkt_ref[...], hb, _NT,
                                  preferred_element_type=F32).astype(BF16)
    gr = lax.dot_general(wgt_ref[...], hb, _NT, preferred_element_type=F32) + bg_ref[...]
    gr = GATE_SOFTCAP * jnp.tanh(gr / GATE_SOFTCAP)
    log_sig = jnp.minimum(gr, 0.0) - jnp.log1p(jnp.exp(-jnp.abs(gr)))
    row = lax.broadcasted_iota(jnp.int32, gr.shape, 0)
    gates_ref[...] = jnp.where(row < M_HEADS, gr, log_sig)


def _a_proj(x, g, w_voq, w_kt, w_gt, b_g):
    t = x.shape[0]
    tm = ROW_TILE
    n_out = w_voq.shape[1]
    return pl.pallas_call(
        _a_proj_kernel,
        grid=(t // tm,),
        in_specs=[pl.BlockSpec((tm, D_MODEL), lambda i: (i, 0)),
                  _const_spec((1, D_MODEL)),
                  _const_spec((D_MODEL, n_out)),
                  _const_spec((M_QK_COLS, D_MODEL)),
                  _const_spec((2 * M_HEADS, D_MODEL)),
                  _const_spec((2 * M_HEADS, 1))],
        out_specs=[pl.BlockSpec((tm, n_out), lambda i: (i, 0)),
                   pl.BlockSpec((M_QK_COLS, tm), lambda i: (0, i)),
                   pl.BlockSpec((2 * M_HEADS, tm), lambda i: (0, i))],
        out_shape=[jax.ShapeDtypeStruct((t, n_out), BF16),
                   jax.ShapeDtypeStruct((M_QK_COLS, t), BF16),
                   jax.ShapeDtypeStruct((2 * M_HEADS, t), F32)],
        compiler_params=_params("parallel"),
        name="mlstm_proj",
    )(x, g, w_voq, w_kt, w_gt, b_g)


def _mlstm_kernel(q_ref, kt_ref, v_ref, o_ref, g_ref, gh_ref, y_ref, c_ref, m_ref):
    L = q_ref.shape[0]

    @pl.when(pl.program_id(1) == 0)
    def _():
        c_ref[...] = jnp.zeros_like(c_ref)
        m_ref[...] = jnp.zeros_like(m_ref)

    jj = lax.broadcasted_iota(jnp.int32, (L, L), 0)
    ss = lax.broadcasted_iota(jnp.int32, (L, L), 1)
    causal = ss <= jj
    upper = (jj <= ss).astype(F32)

    gates = g_ref[...]
    cums = jnp.dot(gates, upper, preferred_element_type=F32, precision=lax.Precision.HIGHEST)
    b8 = pltpu.roll(cums, M_HEADS, axis=0)
    r8 = gates - b8
    lane = lax.broadcasted_iota(jnp.int32, r8.shape, 1)
    cm8 = r8
    shift = 1
    while shift < L:
        cm8 = jnp.maximum(cm8, jnp.where(lane >= shift, pltpu.roll(cm8, shift, axis=1), -jnp.inf))
        shift *= 2
    stack = jnp.concatenate([cm8, b8, jnp.zeros((LANES - 16, L), F32)], axis=0)
    cols = stack.T
    lane1 = lax.broadcasted_iota(jnp.int32, (1, LANES), 1)
    m_prev = [m_ref[h] for h in range(M_HEADS)]
    m_lanes = sum(jnp.where(lane1 == h, m_prev[h], 0.0) for h in range(M_HEADS))
    m_cols = jnp.maximum(cols, m_lanes)
    b_cols = pltpu.roll(cols, LANES - 8, axis=1)
    is_head = lane1 < M_HEADS
    s_inter = jnp.where(is_head, jnp.exp(m_lanes - m_cols), 0.0).astype(BF16)
    floor = jnp.where(is_head, jnp.exp(-(b_cols + m_cols)), 0.0).astype(BF16)
    spread = (lax.broadcasted_iota(jnp.int32, (LANES, M_HEADS * LANES), 0)
              == lax.broadcasted_iota(jnp.int32, (LANES, M_HEADS * LANES), 1) // LANES
              ).astype(BF16)
    s_inter = jnp.dot(s_inter, spread, preferred_element_type=F32)
    floor = jnp.dot(floor, spread, preferred_element_type=F32)
    ones_blk = jnp.ones((L, LANES), BF16)
    mean_blk = jnp.full((M_V_DIM, LANES), 1.0 / M_V_DIM, BF16)

    for h in range(M_HEADS):
        lanes_h = slice(h * LANES, (h + 1) * LANES)
        q = q_ref[:, lanes_h]
        kt = kt_ref[lanes_h, :]
        v_ext = jnp.concatenate([v_ref[:, h * M_V_DIM:(h + 1) * M_V_DIM], ones_blk], axis=1)
        c_prev = c_ref[h]
        r_row = r8[h:h + 1]

        w_intra = jnp.exp(jnp.where(causal, r_row, -jnp.inf) - m_cols[:, h:h + 1])
        s_qk = jnp.dot(q, kt, preferred_element_type=F32) * w_intra
        s_in = s_inter[:, lanes_h]
        ext = (jnp.concatenate([s_in, s_in, s_in], axis=1)
               * jnp.dot(q, c_prev.astype(BF16), preferred_element_type=F32)
               + jnp.dot(s_qk.astype(BF16), v_ext, preferred_element_type=F32))
        den = ext[:, M_V_DIM:]
        inv = 1.0 / jnp.maximum(jnp.abs(den), floor[:, lanes_h])
        h_out = ext[:, :M_V_DIM] * jnp.concatenate([inv, inv], axis=1)

        m_last = jnp.maximum(cm8[h:h + 1, L - 1:L], m_prev[h])
        decay = jnp.exp(m_prev[h] - m_last)
        kwt = (kt.astype(F32) * jnp.exp(r_row - m_last)).astype(BF16)
        c_ref[h] = decay * c_prev + jnp.dot(kwt, v_ext, preferred_element_type=F32)
        m_ref[h] = b8[h:h + 1, L - 1:L] + m_last

        sl = slice(h * M_V_DIM, (h + 1) * M_V_DIM)
        ms = jnp.dot((h_out * h_out).astype(BF16), mean_blk, preferred_element_type=F32)
        rs = lax.rsqrt(ms + NORM_EPS)
        og = jax.nn.sigmoid(o_ref[:, sl].astype(F32))
        y_ref[:, sl] = (og * h_out * jnp.concatenate([rs, rs], axis=1)
                        * gh_ref[:, sl]).astype(BF16)


def _mlstm(voq, kt, gates, g_head, batch, seq):
    L = M_CHUNK
    nc = seq // L
    t = batch * seq
    row = lambda b, c: b * nc + c
    return pl.pallas_call(
        _mlstm_kernel,
        grid=(batch, nc),
        in_specs=[pl.BlockSpec((L, M_QK_COLS), lambda b, c: (row(b, c), 2 * D_MODEL // M_QK_COLS)),
                  pl.BlockSpec((M_QK_COLS, L), lambda b, c: (0, row(b, c))),
                  pl.BlockSpec((L, D_MODEL), lambda b, c: (row(b, c), 0)),
                  pl.BlockSpec((L, D_MODEL), lambda b, c: (row(b, c), 1)),
                  pl.BlockSpec((2 * M_HEADS, L), lambda b, c: (0, row(b, c))),
                  pl.BlockSpec((1, D_MODEL), lambda b, c: (0, 0))],
        out_specs=pl.BlockSpec((L, D_MODEL), lambda b, c: (row(b, c), 0)),
        out_shape=jax.ShapeDtypeStruct((t, D_MODEL), BF16),
        scratch_shapes=[pltpu.VMEM((M_HEADS, M_QK_DIM, M_V_DIM + LANES), F32),
                        pltpu.VMEM((M_HEADS, 1, 1), F32)],
        compiler_params=_params("parallel", "arbitrary"),
        name="mlstm_mixer",
    )(voq, kt, voq, voq, gates, g_head)


def _post_kernel(x_ref, y_ref, wo_ref, g1_ref, g2_ref, g3_ref, wu_ref, wd_ref, out_ref):
    z = jnp.dot(y_ref[...], wo_ref[...], preferred_element_type=F32)
    x1 = x_ref[...] + _rms(z, g1_ref[...])
    hb = _rms(x1, g2_ref[...]).astype(BF16)
    acc = jnp.zeros(x1.shape, F32)
    for f in range(0, D_FF, 1024):
        u = jnp.maximum(jnp.dot(hb, wu_ref[:, f:f + 1024], preferred_element_type=F32), 0.0)
        acc = acc + jnp.dot((u * u).astype(BF16), wd_ref[f:f + 1024, :],
                            preferred_element_type=F32)
    out_ref[...] = x1 + _rms(acc, g3_ref[...])


def _post(x, y, w_out, g1, g2, g3, w_up, w_down):
    t = x.shape[0]
    tm = ROW_TILE
    return pl.pallas_call(
        _post_kernel,
        grid=(t // tm,),
        in_specs=[pl.BlockSpec((tm, D_MODEL), lambda i: (i, 0)),
                  pl.BlockSpec((tm, D_MODEL), lambda i: (i, 0)),
                  _const_spec((D_MODEL, D_MODEL)),
                  _const_spec((1, D_MODEL)), _const_spec((1, D_MODEL)), _const_spec((1, D_MODEL)),
                  _const_spec((D_MODEL, D_FF)),
                  _const_spec((D_FF, D_MODEL))],
        out_specs=pl.BlockSpec((tm, D_MODEL), lambda i: (i, 0)),
        out_shape=jax.ShapeDtypeStruct((t, D_MODEL), F32),
        compiler_params=_params("parallel"),
        name="outproj_mlp",
    )(x, y, w_out, g1, g2, g3, w_up, w_down)


def _kv_proj_kernel(x_ref, g_ref, wk_ref, wvt_ref, cos_ref, sin_ref, k_ref, vt_ref, kn_ref):
    hb = _rms(x_ref[...], g_ref[...]).astype(BF16)
    cos = cos_ref[...]
    sin = sin_ref[...]
    kk = jnp.dot(hb, wk_ref[...], preferred_element_type=F32)
    col = lax.broadcasted_iota(jnp.int32, (D_MODEL, LANES), 0)
    out = lax.broadcasted_iota(jnp.int32, (D_MODEL, LANES), 1)
    pick = (out == 2 * (col >> 7) + ((col >> 5) & 1)).astype(BF16)
    k_sq = jnp.dot((kk * kk).astype(BF16), pick, preferred_element_type=F32)
    kn_ref[0] = jnp.max(k_sq, axis=0, keepdims=True)
    for h in range(A_HEADS):
        sl = slice(h * LANES, (h + 1) * LANES)
        k_ref[:, sl] = _rope(kk[:, sl], cos, sin).astype(BF16)
    vt = lax.dot_general(wvt_ref[...], hb, _NT, preferred_element_type=F32)
    ones = jnp.ones((V_ROWS - A_V_DIM, vt.shape[1]), BF16)
    for h in range(A_HEADS):
        vt_ref[0, 0, h, :A_V_DIM, :] = vt[h * A_V_DIM:(h + 1) * A_V_DIM].astype(BF16)
        vt_ref[0, 0, h, A_V_DIM:, :] = ones


def _kv_proj(x, g, w_k, w_vt, cos, sin, batch, seq):
    t = x.shape[0]
    tm = ATT_TK
    nk = seq // tm
    return pl.pallas_call(
        _kv_proj_kernel,
        grid=(t // tm,),
        in_specs=[pl.BlockSpec((tm, D_MODEL), lambda i: (i, 0)),
                  _const_spec((1, D_MODEL)),
                  _const_spec((D_MODEL, D_MODEL)),
                  _const_spec((D_MODEL, D_MODEL)),
                  pl.BlockSpec((tm, LANES), lambda i: (i, 0)),
                  pl.BlockSpec((tm, LANES), lambda i: (i, 0))],
        out_specs=[pl.BlockSpec((tm, D_MODEL), lambda i: (i, 0)),
                   pl.BlockSpec((1, 1, A_HEADS, V_ROWS, tm),
                                lambda i: (i // nk, i % nk, 0, 0, 0)),
                   pl.BlockSpec((1, 1, LANES), lambda i: (i, 0, 0))],
        out_shape=[jax.ShapeDtypeStruct((t, D_MODEL), BF16),
                   jax.ShapeDtypeStruct((batch, nk, A_HEADS, V_ROWS, tm), BF16),
                   jax.ShapeDtypeStruct((t // tm, 1, LANES), F32)],
        compiler_params=_params("parallel"),
        name="kv_proj",
    )(x, g, w_k, w_vt, cos, sin)


def _q_proj_kernel(x_ref, g_ref, wq_ref, cos_ref, sin_ref, q_ref):
    hb = _rms(x_ref[...], g_ref[...]).astype(BF16)
    cos = cos_ref[...]
    sin = sin_ref[...]
    qq = jnp.dot(hb, wq_ref[...], preferred_element_type=F32)
    for h in range(A_HEADS):
        sl = slice(h * LANES, (h + 1) * LANES)
        q_ref[:, sl] = _rope(qq[:, sl], cos, sin).astype(BF16)


def _q_proj(x, g, w_q, cos, sin):
    t = x.shape[0]
    tm = ROW_TILE
    return pl.pallas_call(
        _q_proj_kernel,
        grid=(t // tm,),
        in_specs=[pl.BlockSpec((tm, D_MODEL), lambda i: (i, 0)),
                  _const_spec((1, D_MODEL)),
                  _const_spec((D_MODEL, D_MODEL)),
                  pl.BlockSpec((tm, LANES), lambda i: (i, 0)),
                  pl.BlockSpec((tm, LANES), lambda i: (i, 0))],
        out_specs=pl.BlockSpec((tm, D_MODEL), lambda i: (i, 0)),
        out_shape=jax.ShapeDtypeStruct((t, D_MODEL), BF16),
        compiler_params=_params("parallel"),
        name="q_proj",
    )(x, g, w_q, cos, sin)


def _attn_kernel(lam_ref, kn_ref, q_ref, k_ref, vt_ref, g_ref, o_ref, sa_ref, sb_ref, acc_ref, *,
                 lam_init):
    tq = q_ref.shape[0]
    tk = vt_ref.shape[4]
    head = pl.program_id(1)
    i = pl.program_id(2)

    lam = lam_ref[...]
    lam_full = (jnp.exp(jnp.sum(lam[0:1] * lam[1:2], keepdims=True))
                - jnp.exp(jnp.sum(lam[2:3] * lam[3:4], keepdims=True)) + lam_init)

    q_t = q_ref[...].astype(F32).T
    row_map = (lax.broadcasted_iota(jnp.int32, q_t.shape, 0) >> 5) & 1
    q_cat = jnp.concatenate([jnp.where(row_map == c, q_t, 0.0) for c in range(2)],
                            axis=1).astype(BF16)

    k_sq = jnp.max(kn_ref[...], axis=0)
    lane = lax.broadcasted_iota(jnp.int32, k_sq.shape, 1)
    q_sq = q_t * q_t
    bound = jnp.concatenate(
        [jnp.sqrt(jnp.sum(jnp.where(row_map == c, q_sq, 0.0), axis=0, keepdims=True)
                  * jnp.max(jnp.where(lane == 2 * head + c, k_sq, 0.0), axis=1, keepdims=True))
         for c in range(2)], axis=1) * BOUND_SLACK

    def causal(j, s):
        key_i = j * tk + lax.broadcasted_iota(jnp.int32, s.shape, 0)
        qry_i = i * tq + (lax.broadcasted_iota(jnp.int32, s.shape, 1) & (tq - 1))
        return jnp.where(key_i <= qry_i, s, -jnp.inf)

    acc_ref[...] = jnp.zeros_like(acc_ref)
    n_full = (i * tq) // tk

    def bounded():
        def tile(j, diagonal):
            kb = k_ref[pl.ds(pl.multiple_of(j * tk, tk), tk), :]
            s = jnp.dot(kb, q_cat, preferred_element_type=F32) - bound
            if diagonal:
                s = causal(j, s)
            return jnp.dot(vt_ref[0, j, 0], jnp.exp2(s).astype(BF16),
                           preferred_element_type=F32)

        def trip(n_tiles, first):
            def body(jj, carry):
                j = first + n_tiles * jj
                total = tile(j, False)
                for t in range(1, n_tiles):
                    total = total + tile(j + t, False)
                acc_ref[...] += total
                return carry
            return body

        done = 0
        for n_tiles in ATT_TRIP_TILES:
            n_trips = (n_full - done) // n_tiles
            lax.fori_loop(0, n_trips, trip(n_tiles, done), 0)
            done = done + n_trips * n_tiles
        acc_ref[...] += tile(n_full, True)

    def online():
        def scores(j, s_ref):
            kb = k_ref[pl.ds(pl.multiple_of(j * tk, tk), tk), :]
            s_ref[...] = jnp.dot(kb, q_cat, preferred_element_type=F32)

        def softmax_pv(j, s_ref, m_old, diagonal):
            s = s_ref[...]
            if diagonal:
                s = causal(j, s)
            m_new = jnp.maximum(m_old, jnp.max(s, axis=0, keepdims=True))
            alpha = jnp.exp2(m_old - m_new)
            p = jnp.exp2(s - m_new).astype(BF16)
            acc_ref[...] = alpha * acc_ref[...] + jnp.dot(vt_ref[0, j, 0], p,
                                                          preferred_element_type=F32)
            return m_new

        scores(0, sa_ref)

        def pair(jj, m_run):
            j = 2 * jj
            scores(j + 1, sb_ref)
            m_run = softmax_pv(j, sa_ref, m_run, False)
            scores(j + 2, sa_ref)
            return softmax_pv(j + 1, sb_ref, m_run, False)

        m_run = lax.fori_loop(0, n_full // 2, pair, jnp.full((1, 2 * tq), -jnp.inf, F32))

        def tail_odd(m_old):
            scores(n_full, sb_ref)
            m_mid = softmax_pv(n_full - 1, sa_ref, m_old, False)
            return softmax_pv(n_full, sb_ref, m_mid, True)

        def tail_even(m_old):
            return softmax_pv(n_full, sa_ref, m_old, True)

        lax.cond(n_full % 2 == 1, tail_odd, tail_even, m_run)

    lax.cond(jnp.max(bound) <= BOUND_MAX, bounded, online)

    acc = acc_ref[...]
    inv_l = 1.0 / acc[A_V_DIM:A_V_DIM + 1]
    o = acc[:A_V_DIM, :tq] * inv_l[:, :tq] - lam_full * (acc[:A_V_DIM, tq:] * inv_l[:, tq:])
    ms = jnp.mean(o * o, axis=0, keepdims=True)
    o = o * lax.rsqrt(ms + NORM_EPS) * g_ref[...] * (1.0 - lam_init)
    o_ref[...] = o.T.astype(BF16)


def _attention(q, k, vt, k_norms, lam, g_col, batch, seq, lam_init):
    tq, tk = ATT_TQ, ATT_TK
    assert tk % tq == 0 and tq & (tq - 1) == 0
    nq = seq // tq
    nk = seq // tk
    t = batch * seq
    return pl.pallas_call(
        functools.partial(_attn_kernel, lam_init=lam_init),
        grid=(batch, A_HEADS, nq),
        in_specs=[pl.BlockSpec((4, A_HEAD_DIM), lambda b, h, i: (0, 0)),
                  pl.BlockSpec((nk, 1, LANES), lambda b, h, i: (b, 0, 0)),
                  pl.BlockSpec((tq, LANES), lambda b, h, i: (b * nq + i, h)),
                  pl.BlockSpec((seq, LANES), lambda b, h, i: (b, h)),
                  pl.BlockSpec((1, nk, 1, V_ROWS, tk), lambda b, h, i: (b, 0, h, 0, 0)),
                  pl.BlockSpec((A_V_DIM, 1), lambda b, h, i: (h, 0))],
        out_specs=pl.BlockSpec((tq, LANES), lambda b, h, i: (b * nq + i, h)),
        out_shape=jax.ShapeDtypeStruct((t, D_MODEL), BF16),
        scratch_shapes=[pltpu.VMEM((tk, 2 * tq), F32),
                        pltpu.VMEM((tk, 2 * tq), F32),
                        pltpu.VMEM((V_ROWS, 2 * tq), F32)],
        compiler_params=_params("parallel", "parallel", "arbitrary"),
        name="diff_attention",
    )(lam, k_norms, q, k, vt, g_col)


def _head_perm(w):
    half = A_HEAD_DIM // 2
    w = w.reshape(w.shape[0], A_HEADS, 2, 2, half)
    return w.transpose(0, 1, 3, 2, 4).reshape(w.shape[0], D_MODEL)


def kernel(x, positions, norm_g, a_w_in, a_b_gates, a_g_head, a_w_out, kv_norm_g, w_kv,
           b_w_q, b_lam, b_g_head, b_w_out, mlp_w_up, mlp_w_down):
    batch, seq, _ = x.shape
    t = batch * seq
    depth = norm_g.shape[0]
    n_a = a_w_in.shape[0]
    xf = x.reshape(t, D_MODEL)
    row = lambda v: v.reshape(1, -1).astype(F32)

    cos, sin = _rope_tables(positions.reshape(t, 1))
    k_sh = vt_sh = k_norms = None
    for layer in range(depth):
        g = norm_g[layer]
        if layer < n_a:
            w_in = a_w_in[layer]
            vo_end = 2 * M_QK_COLS + 2 * D_MODEL
            w_voq = jnp.concatenate([w_in[:, 2 * M_QK_COLS:vo_end], w_in[:, :M_QK_COLS]],
                                    axis=1).astype(BF16)
            w_kt = (w_in[:, M_QK_COLS:2 * M_QK_COLS] * M_QK_DIM ** -0.5).T.astype(BF16)
            w_gt = w_in[:, vo_end:].T.astype(BF16)
            voq, kt, gates = _a_proj(xf, row(g[0]), w_voq, w_kt, w_gt,
                                     a_b_gates[layer].reshape(-1, 1).astype(F32))
            y = _mlstm(voq, kt, gates, row(a_g_head[layer]), batch, seq)
            w_out = a_w_out[layer]
        else:
            j = layer - n_a
            if j == 0:
                w_k = _head_perm(w_kv[:, :D_MODEL]).astype(BF16)
                w_vt = w_kv[:, D_MODEL:].T.astype(BF16)
                k_sh, vt_sh, k_norms = _kv_proj(xf, row(kv_norm_g), w_k, w_vt, cos, sin,
                                                batch, seq)
            lam_init = 0.8 - 0.6 * math.exp(-0.3 * layer)
            w_q = (_head_perm(b_w_q[j]) * (A_HEAD_DIM ** -0.5 * math.log2(math.e))).astype(BF16)
            q = _q_proj(xf, row(g[0]), w_q, cos, sin)
            y = _attention(q, k_sh, vt_sh, k_norms, b_lam[j].astype(F32),
                           b_g_head[j].reshape(-1, 1).astype(F32), batch, seq, lam_init)
            w_out = b_w_out[j]
        xf = _post(xf, y, w_out.astype(BF16), row(g[1]), row(g[2]), row(g[3]),
                   mlp_w_up[layer].astype(BF16), mlp_w_down[layer].astype(BF16))
    return xf.reshape(batch, seq, D_MODEL)
```

```python
import functools
import math

import jax
import jax.numpy as jnp
from jax import lax
from jax.experimental import pallas as pl
from jax.experimental.pallas import tpu as pltpu

F32 = jnp.float32
BF16 = jnp.bfloat16

D_MODEL = 1024
D_FF = 4 * D_MODEL
NORM_EPS = 1e-6

M_HEADS = 4
M_V_DIM = D_MODEL // M_HEADS
M_QK_DIM = M_V_DIM // 2
M_QK_COLS = M_HEADS * M_QK_DIM
GATE_SOFTCAP = 15.0
M_CHUNK = 256

A_HEADS = 8
A_HEAD_DIM = D_MODEL // (2 * A_HEADS)
A_V_DIM = 2 * A_HEAD_DIM
ROPE_THETA = 10000.0

LANES = 128
VMEM_LIMIT_BYTES = 56 * 1024 * 1024

ROW_TILE = 512
ATT_TQ = 512
ATT_TK = 512
ATT_TRIP_TILES = (8, 4, 2, 1)
BOUND_MAX = 40.0
BOUND_SLACK = 1.03
V_ROWS = A_V_DIM + 16

_NN = (((1,), (0,)), ((), ()))
_NT = (((1,), (1,)), ((), ()))
_TN = (((0,), (0,)), ((), ()))


def _params(*sem):
    return pltpu.CompilerParams(dimension_semantics=sem, vmem_limit_bytes=VMEM_LIMIT_BYTES)


def _rms(x, g):
    ms = jnp.mean(x * x, axis=-1, keepdims=True)
    return x * lax.rsqrt(ms + NORM_EPS) * g


def _const_spec(shape):
    nd = len(shape)
    return pl.BlockSpec(shape, lambda *_: (0,) * nd, pipeline_mode=pl.Buffered(1))


def _rope_table_kernel(pos_ref, post_ref, inv_ref, invt_ref, sign_ref,
                       cos_ref, sin_ref, cost_ref, sint_ref):
    ang = pos_ref[...].astype(F32) * inv_ref[...]
    cos_ref[...] = jnp.cos(ang)
    sin_ref[...] = jnp.sin(ang) * sign_ref[...]
    ang_t = invt_ref[...] * post_ref[...].astype(F32)
    cost_ref[...] = jnp.cos(ang_t)
    sint_ref[...] = jnp.sin(ang_t)


def _rope_tables(pos, pos_row):
    t = pos.shape[0]
    half = A_HEAD_DIM // 2
    inv = 1.0 / (ROPE_THETA ** (jnp.arange(0, A_HEAD_DIM, 2, dtype=F32) / A_HEAD_DIM))
    inv_t = jnp.tile(inv, 2).reshape(A_HEAD_DIM, 1)
    inv = jnp.tile(inv, LANES // half).reshape(1, LANES)
    sign = jnp.where(jnp.arange(LANES) < LANES // 2, -1.0, 1.0).astype(F32).reshape(1, LANES)
    tm = ROW_TILE
    return pl.pallas_call(
        _rope_table_kernel,
        grid=(t // tm,),
        in_specs=[pl.BlockSpec((tm, 1), lambda i: (i, 0)),
                  pl.BlockSpec((1, tm), lambda i: (0, i)),
                  _const_spec((1, LANES)), _const_spec((A_HEAD_DIM, 1)), _const_spec((1, LANES))],
        out_specs=[pl.BlockSpec((tm, LANES), lambda i: (i, 0))] * 2
        + [pl.BlockSpec((A_HEAD_DIM, tm), lambda i: (0, i))] * 2,
        out_shape=[jax.ShapeDtypeStruct((t, LANES), F32)] * 2
        + [jax.ShapeDtypeStruct((A_HEAD_DIM, t), F32)] * 2,
        compiler_params=_params("parallel"),
        name="rope_tables",
    )(pos, pos_row, inv, inv_t, sign)


def _rope(t, cos, sin):
    return t * cos + pltpu.roll(t, LANES // 2, axis=1) * sin


def _a_proj_kernel(x_ref, g_ref, w_ref, wkt_ref, wgt_ref, bg_ref, voq_ref, kt_ref, gates_ref):
    hb = _rms(x_ref[...], g_ref[...]).astype(BF16)
    n_out = voq_ref.shape[1]
    for n in range(0, n_out, 512):
        voq_ref[:, n:n + 512] = jnp.dot(
            hb, w_ref[:, n:n + 512], preferred_element_type=F32).astype(BF16)
    kt_ref[...] = lax.dot_general(wkt_ref[...], hb, _NT,
                                  preferred_element_type=F32).astype(BF16)
    gr = lax.dot_general(wgt_ref[...], hb, _NT, preferred_element_type=F32) + bg_ref[...]
    gr = GATE_SOFTCAP * jnp.tanh(gr / GATE_SOFTCAP)
    log_sig = jnp.minimum(gr, 0.0) - jnp.log1p(jnp.exp(-jnp.abs(gr)))
    row = lax.broadcasted_iota(jnp.int32, gr.shape, 0)
    gates_ref[...] = jnp.where(row < M_HEADS, gr, log_sig)


def _a_proj(x, g, w_voq, w_kt, w_gt, b_g):
    t = x.shape[0]
    tm = ROW_TILE
    n_out = w_voq.shape[1]
    return pl.pallas_call(
        _a_proj_kernel,
        grid=(t // tm,),
        in_specs=[pl.BlockSpec((tm, D_MODEL), lambda i: (i, 0)),
                  _const_spec((1, D_MODEL)),
                  _const_spec((D_MODEL, n_out)),
                  _const_spec((M_QK_COLS, D_MODEL)),
                  _const_spec((2 * M_HEADS, D_MODEL)),
                  _const_spec((2 * M_HEADS, 1))],
        out_specs=[pl.BlockSpec((tm, n_out), lambda i: (i, 0)),
                   pl.BlockSpec((M_QK_COLS, tm), lambda i: (0, i)),
                   pl.BlockSpec((2 * M_HEADS, tm), lambda i: (0, i))],
        out_shape=[jax.ShapeDtypeStruct((t, n_out), BF16),
                   jax.ShapeDtypeStruct((M_QK_COLS, t), BF16),
                   jax.ShapeDtypeStruct((2 * M_HEADS, t), F32)],
        compiler_params=_params("parallel"),
        name="mlstm_proj",
    )(x, g, w_voq, w_kt, w_gt, b_g)


def _mlstm_kernel(q_ref, kt_ref, v_ref, o_ref, g_ref, gh_ref, y_ref, c_ref, m_ref):
    L = q_ref.shape[0]

    @pl.when(pl.program_id(1) == 0)
    def _():
        c_ref[...] = jnp.zeros_like(c_ref)
        m_ref[...] = jnp.zeros_like(m_ref)

    jj = lax.broadcasted_iota(jnp.int32, (L, L), 0)
    ss = lax.broadcasted_iota(jnp.int32, (L, L), 1)
    causal = ss <= jj
    upper = (jj <= ss).astype(F32)

    gates = g_ref[...]
    cums = jnp.dot(gates, upper, preferred_element_type=F32, precision=lax.Precision.HIGHEST)
    b8 = pltpu.roll(cums, M_HEADS, axis=0)
    r8 = gates - b8
    lane = lax.broadcasted_iota(jnp.int32, r8.shape, 1)
    cm8 = r8
    shift = 1
    while shift < L:
        cm8 = jnp.maximum(cm8, jnp.where(lane >= shift, pltpu.roll(cm8, shift, axis=1), -jnp.inf))
        shift *= 2
    stack = jnp.concatenate([cm8, b8, jnp.zeros((LANES - 16, L), F32)], axis=0)
    cols = stack.T
    lane1 = lax.broadcasted_iota(jnp.int32, (1, LANES), 1)
    m_prev = [m_ref[h] for h in range(M_HEADS)]
    m_lanes = sum(jnp.where(lane1 == h, m_prev[h], 0.0) for h in range(M_HEADS))
    m_cols = jnp.maximum(cols, m_lanes)
    b_cols = pltpu.roll(cols, LANES - 8, axis=1)
    is_head = lane1 < M_HEADS
    s_inter = jnp.where(is_head, jnp.exp(m_lanes - m_cols), 0.0).astype(BF16)
    floor = jnp.where(is_head, jnp.exp(-(b_cols + m_cols)), 0.0).astype(BF16)
    spread = (lax.broadcasted_iota(jnp.int32, (LANES, M_HEADS * LANES), 0)
              == lax.broadcasted_iota(jnp.int32, (LANES, M_HEADS * LANES), 1) // LANES
              ).astype(BF16)
    s_inter = jnp.dot(s_inter, spread, preferred_element_type=F32)
    floor = jnp.dot(floor, spread, preferred_element_type=F32)
    ones_blk = jnp.ones((L, LANES), BF16)
    mean_blk = jnp.full((M_V_DIM, LANES), 1.0 / M_V_DIM, BF16)

    for h in range(M_HEADS):
        lanes_h = slice(h * LANES, (h + 1) * LANES)
        q = q_ref[:, lanes_h]
        kt = kt_ref[lanes_h, :]
        v_ext = jnp.concatenate([v_ref[:, h * M_V_DIM:(h + 1) * M_V_DIM], ones_blk], axis=1)
        c_prev = c_ref[h]
        r_row = r8[h:h + 1]

        w_intra = jnp.exp(jnp.where(causal, r_row, -jnp.inf) - m_cols[:, h:h + 1])
        s_qk = jnp.dot(q, kt, preferred_element_type=F32) * w_intra
        s_in = s_inter[:, lanes_h]
        ext = (jnp.concatenate([s_in, s_in, s_in], axis=1)
               * jnp.dot(q, c_prev.astype(BF16), preferred_element_type=F32)
               + jnp.dot(s_qk.astype(BF16), v_ext, preferred_element_type=F32))
        den = ext[:, M_V_DIM:]
        inv = 1.0 / jnp.maximum(jnp.abs(den), floor[:, lanes_h])
        h_out = ext[:, :M_V_DIM] * jnp.concatenate([inv, inv], axis=1)

        m_last = jnp.maximum(cm8[h:h + 1, L - 1:L], m_prev[h])
        decay = jnp.exp(m_prev[h] - m_last)
        kwt = (kt.astype(F32) * jnp.exp(r_row - m_last)).astype(BF16)
        c_ref[h] = decay * c_prev + jnp.dot(kwt, v_ext, preferred_element_type=F32)
        m_ref[h] = b8[h:h + 1, L - 1:L] + m_last

        sl = slice(h * M_V_DIM, (h + 1) * M_V_DIM)
        ms = jnp.dot((h_out * h_out).astype(BF16), mean_blk, preferred_element_type=F32)
        rs = lax.rsqrt(ms + NORM_EPS)
        og = jax.nn.sigmoid(o_ref[:, sl].astype(F32))
        y_ref[:, sl] = (og * h_out * jnp.concatenate([rs, rs], axis=1)
                        * gh_ref[:, sl]).astype(BF16)


def _mlstm(voq, kt, gates, g_head, batch, seq):
    L = M_CHUNK
    nc = seq // L
    t = batch * seq
    row = lambda b, c: b * nc + c
    return pl.pallas_call(
        _mlstm_kernel,
        grid=(batch, nc),
        in_specs=[pl.BlockSpec((L, M_QK_COLS), lambda b, c: (row(b, c), 2 * D_MODEL // M_QK_COLS)),
                  pl.BlockSpec((M_QK_COLS, L), lambda b, c: (0, row(b, c))),
                  pl.BlockSpec((L, D_MODEL), lambda b, c: (row(b, c), 0)),
                  pl.BlockSpec((L, D_MODEL), lambda b, c: (row(b, c), 1)),
                  pl.BlockSpec((2 * M_HEADS, L), lambda b, c: (0, row(b, c))),
                  pl.BlockSpec((1, D_MODEL), lambda b, c: (0, 0))],
        out_specs=pl.BlockSpec((L, D_MODEL), lambda b, c: (row(b, c), 0)),
        out_shape=jax.ShapeDtypeStruct((t, D_MODEL), BF16),
        scratch_shapes=[pltpu.VMEM((M_HEADS, M_QK_DIM, M_V_DIM + LANES), F32),
                        pltpu.VMEM((M_HEADS, 1, 1), F32)],
        compiler_params=_params("parallel", "arbitrary"),
        name="mlstm_mixer",
    )(voq, kt, voq, voq, gates, g_head)


def _post_kernel(x_ref, y_ref, wo_ref, g1_ref, g2_ref, g3_ref, wu_ref, wd_ref, out_ref, *,
                 y_feature_major):
    z = lax.dot_general(y_ref[...], wo_ref[...], _TN if y_feature_major else _NN,
                        preferred_element_type=F32)
    x1 = x_ref[...] + _rms(z, g1_ref[...])
    hb = _rms(x1, g2_ref[...]).astype(BF16)
    acc = jnp.zeros(x1.shape, F32)
    for f in range(0, D_FF, 1024):
        u = jnp.maximum(jnp.dot(hb, wu_ref[:, f:f + 1024], preferred_element_type=F32), 0.0)
        acc = acc + jnp.dot((u * u).astype(BF16), wd_ref[f:f + 1024, :],
                            preferred_element_type=F32)
    out_ref[...] = x1 + _rms(acc, g3_ref[...])


def _post(x, y, w_out, g1, g2, g3, w_up, w_down, y_feature_major):
    t = x.shape[0]
    tm = ROW_TILE
    y_spec = (pl.BlockSpec((D_MODEL, tm), lambda i: (0, i)) if y_feature_major
              else pl.BlockSpec((tm, D_MODEL), lambda i: (i, 0)))
    return pl.pallas_call(
        functools.partial(_post_kernel, y_feature_major=y_feature_major),
        grid=(t // tm,),
        in_specs=[pl.BlockSpec((tm, D_MODEL), lambda i: (i, 0)),
                  y_spec,
                  _const_spec((D_MODEL, D_MODEL)),
                  _const_spec((1, D_MODEL)), _const_spec((1, D_MODEL)), _const_spec((1, D_MODEL)),
                  _const_spec((D_MODEL, D_FF)),
                  _const_spec((D_FF, D_MODEL))],
        out_specs=pl.BlockSpec((tm, D_MODEL), lambda i: (i, 0)),
        out_shape=jax.ShapeDtypeStruct((t, D_MODEL), F32),
        compiler_params=_params("parallel"),
        name="outproj_mlp",
    )(x, y, w_out, g1, g2, g3, w_up, w_down)


def _kv_proj_kernel(x_ref, g_ref, wk_ref, wvt_ref, cos_ref, sin_ref, k_ref, vt_ref, kn_ref):
    hb = _rms(x_ref[...], g_ref[...]).astype(BF16)
    cos = cos_ref[...]
    sin = sin_ref[...]
    kk = jnp.dot(hb, wk_ref[...], preferred_element_type=F32)
    col = lax.broadcasted_iota(jnp.int32, (D_MODEL, LANES), 0)
    out = lax.broadcasted_iota(jnp.int32, (D_MODEL, LANES), 1)
    pick = (out == 2 * (col >> 7) + ((col >> 5) & 1)).astype(BF16)
    k_sq = jnp.dot((kk * kk).astype(BF16), pick, preferred_element_type=F32)
    kn_ref[0] = jnp.max(k_sq, axis=0, keepdims=True)
    for h in range(A_HEADS):
        sl = slice(h * LANES, (h + 1) * LANES)
        k_ref[:, sl] = _rope(kk[:, sl], cos, sin).astype(BF16)
    vt = lax.dot_general(wvt_ref[...], hb, _NT, preferred_element_type=F32)
    ones = jnp.ones((V_ROWS - A_V_DIM, vt.shape[1]), BF16)
    for h in range(A_HEADS):
        vt_ref[0, 0, h, :A_V_DIM, :] = vt[h * A_V_DIM:(h + 1) * A_V_DIM].astype(BF16)
        vt_ref[0, 0, h, A_V_DIM:, :] = ones


def _kv_proj(x, g, w_k, w_vt, cos, sin, batch, seq):
    t = x.shape[0]
    tm = ATT_TK
    nk = seq // tm
    return pl.pallas_call(
        _kv_proj_kernel,
        grid=(t // tm,),
        in_specs=[pl.BlockSpec((tm, D_MODEL), lambda i: (i, 0)),
                  _const_spec((1, D_MODEL)),
                  _const_spec((D_MODEL, D_MODEL)),
                  _const_spec((D_MODEL, D_MODEL)),
                  pl.BlockSpec((tm, LANES), lambda i: (i, 0)),
                  pl.BlockSpec((tm, LANES), lambda i: (i, 0))],
        out_specs=[pl.BlockSpec((tm, D_MODEL), lambda i: (i, 0)),
                   pl.BlockSpec((1, 1, A_HEADS, V_ROWS, tm),
                                lambda i: (i // nk, i % nk, 0, 0, 0)),
                   pl.BlockSpec((1, 1, LANES), lambda i: (i, 0, 0))],
        out_shape=[jax.ShapeDtypeStruct((t, D_MODEL), BF16),
                   jax.ShapeDtypeStruct((batch, nk, A_HEADS, V_ROWS, tm), BF16),
                   jax.ShapeDtypeStruct((t // tm, 1, LANES), F32)],
        compiler_params=_params("parallel"),
        name="kv_proj",
    )(x, g, w_k, w_vt, cos, sin)


def _q_proj_kernel(x_ref, g_ref, wqt_ref, cos_ref, sin_ref, qt_ref):
    hb = _rms(x_ref[...], g_ref[...]).astype(BF16)
    cos = cos_ref[...]
    sin = sin_ref[...]
    qq = lax.dot_general(wqt_ref[...], hb, _NT, preferred_element_type=F32)
    half = LANES // 2
    for h in range(A_HEADS):
        lo = qq[h * LANES:h * LANES + half]
        hi = qq[h * LANES + half:(h + 1) * LANES]
        qt_ref[h * LANES:h * LANES + half, :] = (lo * cos - hi * sin).astype(BF16)
        qt_ref[h * LANES + half:(h + 1) * LANES, :] = (hi * cos + lo * sin).astype(BF16)


def _q_proj(x, g, w_qt, cos_t, sin_t):
    t = x.shape[0]
    tm = ROW_TILE
    return pl.pallas_call(
        _q_proj_kernel,
        grid=(t // tm,),
        in_specs=[pl.BlockSpec((tm, D_MODEL), lambda i: (i, 0)),
                  _const_spec((1, D_MODEL)),
                  _const_spec((D_MODEL, D_MODEL)),
                  pl.BlockSpec((A_HEAD_DIM, tm), lambda i: (0, i)),
                  pl.BlockSpec((A_HEAD_DIM, tm), lambda i: (0, i))],
        out_specs=pl.BlockSpec((D_MODEL, tm), lambda i: (0, i)),
        out_shape=jax.ShapeDtypeStruct((D_MODEL, t), BF16),
        compiler_params=_params("parallel"),
        name="q_proj",
    )(x, g, w_qt, cos_t, sin_t)


def _attn_kernel(lam_ref, kn_ref, q_ref, k_ref, vt_ref, g_ref, o_ref, sa_ref, sb_ref, acc_ref, *,
                 lam_init):
    tq = q_ref.shape[1]
    tk = vt_ref.shape[4]
    head = pl.program_id(1)
    i = pl.program_id(2)

    lam = lam_ref[...]
    lam_full = (jnp.exp(jnp.sum(lam[0:1] * lam[1:2], keepdims=True))
                - jnp.exp(jnp.sum(lam[2:3] * lam[3:4], keepdims=True)) + lam_init)

    q_t = q_ref[...].astype(F32)
    row_map = (lax.broadcasted_iota(jnp.int32, q_t.shape, 0) >> 5) & 1
    q_cat = jnp.concatenate([jnp.where(row_map == c, q_t, 0.0) for c in range(2)],
                            axis=1).astype(BF16)

    k_sq = jnp.max(kn_ref[...], axis=0)
    lane = lax.broadcasted_iota(jnp.int32, k_sq.shape, 1)
    q_sq = q_t * q_t
    bound = jnp.concatenate(
        [jnp.sqrt(jnp.sum(jnp.where(row_map == c, q_sq, 0.0), axis=0, keepdims=True)
                  * jnp.max(jnp.where(lane == 2 * head + c, k_sq, 0.0), axis=1, keepdims=True))
         for c in range(2)], axis=1) * BOUND_SLACK

    def causal(j, s):
        key_i = j * tk + lax.broadcasted_iota(jnp.int32, s.shape, 0)
        qry_i = i * tq + (lax.broadcasted_iota(jnp.int32, s.shape, 1) & (tq - 1))
        return jnp.where(key_i <= qry_i, s, -jnp.inf)

    acc_ref[...] = jnp.zeros_like(acc_ref)
    n_full = (i * tq) // tk

    def bounded():
        def tile(j, diagonal):
            kb = k_ref[pl.ds(pl.multiple_of(j * tk, tk), tk), :]
            s = jnp.dot(kb, q_cat, preferred_element_type=F32) - bound
            if diagonal:
                s = causal(j, s)
            return jnp.dot(vt_ref[0, j, 0], jnp.exp2(s).astype(BF16),
                           preferred_element_type=F32)

        def trip(n_tiles, first):
            def body(jj, carry):
                j = first + n_tiles * jj
                total = tile(j, False)
                for t in range(1, n_tiles):
                    total = total + tile(j + t, False)
                acc_ref[...] += total
                return carry
            return body

        done = 0
        for n_tiles in ATT_TRIP_TILES:
            n_trips = (n_full - done) // n_tiles
            lax.fori_loop(0, n_trips, trip(n_tiles, done), 0)
            done = done + n_trips * n_tiles
        acc_ref[...] += tile(n_full, True)

    def online():
        def scores(j, s_ref):
            kb = k_ref[pl.ds(pl.multiple_of(j * tk, tk), tk), :]
            s_ref[...] = jnp.dot(kb, q_cat, preferred_element_type=F32)

        def softmax_pv(j, s_ref, m_old, diagonal):
            s = s_ref[...]
            if diagonal:
                s = causal(j, s)
            m_new = jnp.maximum(m_old, jnp.max(s, axis=0, keepdims=True))
            alpha = jnp.exp2(m_old - m_new)
            p = jnp.exp2(s - m_new).astype(BF16)
            acc_ref[...] = alpha * acc_ref[...] + jnp.dot(vt_ref[0, j, 0], p,
                                                          preferred_element_type=F32)
            return m_new

        scores(0, sa_ref)

        def pair(jj, m_run):
            j = 2 * jj
            scores(j + 1, sb_ref)
            m_run = softmax_pv(j, sa_ref, m_run, False)
            scores(j + 2, sa_ref)
            return softmax_pv(j + 1, sb_ref, m_run, False)

        m_run = lax.fori_loop(0, n_full // 2, pair, jnp.full((1, 2 * tq), -jnp.inf, F32))

        def tail_odd(m_old):
            scores(n_full, sb_ref)
            m_mid = softmax_pv(n_full - 1, sa_ref, m_old, False)
            return softmax_pv(n_full, sb_ref, m_mid, True)

        def tail_even(m_old):
            return softmax_pv(n_full, sa_ref, m_old, True)

        lax.cond(n_full % 2 == 1, tail_odd, tail_even, m_run)

    lax.cond(jnp.max(bound) <= BOUND_MAX, bounded, online)

    acc = acc_ref[...]
    inv_l = 1.0 / acc[A_V_DIM:A_V_DIM + 1]
    o = acc[:A_V_DIM, :tq] * inv_l[:, :tq] - lam_full * (acc[:A_V_DIM, tq:] * inv_l[:, tq:])
    ms = jnp.mean(o * o, axis=0, keepdims=True)
    o = o * lax.rsqrt(ms + NORM_EPS) * g_ref[...] * (1.0 - lam_init)
    o_ref[...] = o.astype(BF16)


def _attention(q, k, vt, k_norms, lam, g_col, batch, seq, lam_init):
    tq, tk = ATT_TQ, ATT_TK
    assert tk % tq == 0 and tq & (tq - 1) == 0
    nq = seq // tq
    nk = seq // tk
    t = batch * seq
    return pl.pallas_call(
        functools.partial(_attn_kernel, lam_init=lam_init),
        grid=(batch, A_HEADS, nq),
        in_specs=[pl.BlockSpec((4, A_HEAD_DIM), lambda b, h, i: (0, 0)),
                  pl.BlockSpec((nk, 1, LANES), lambda b, h, i: (b, 0, 0)),
                  pl.BlockSpec((LANES, tq), lambda b, h, i: (h, b * nq + i)),
                  pl.BlockSpec((seq, LANES), lambda b, h, i: (b, h)),
                  pl.BlockSpec((1, nk, 1, V_ROWS, tk), lambda b, h, i: (b, 0, h, 0, 0)),
                  pl.BlockSpec((A_V_DIM, 1), lambda b, h, i: (h, 0))],
        out_specs=pl.BlockSpec((A_V_DIM, tq), lambda b, h, i: (h, b * nq + i)),
        out_shape=jax.ShapeDtypeStruct((D_MODEL, t), BF16),
        scratch_shapes=[pltpu.VMEM((tk, 2 * tq), F32),
                        pltpu.VMEM((tk, 2 * tq), F32),
                        pltpu.VMEM((V_ROWS, 2 * tq), F32)],
        compiler_params=_params("parallel", "parallel", "arbitrary"),
        name="diff_attention",
    )(lam, k_norms, q, k, vt, g_col)


def _head_perm(w):
    half = A_HEAD_DIM // 2
    w = w.reshape(w.shape[0], A_HEADS, 2, 2, half)
    return w.transpose(0, 1, 3, 2, 4).reshape(w.shape[0], D_MODEL)


def kernel(x, positions, norm_g, a_w_in, a_b_gates, a_g_head, a_w_out, kv_norm_g, w_kv,
           b_w_q, b_lam, b_g_head, b_w_out, mlp_w_up, mlp_w_down):
    batch, seq, _ = x.shape
    t = batch * seq
    depth = norm_g.shape[0]
    n_a = a_w_in.shape[0]
    xf = x.reshape(t, D_MODEL)
    row = lambda v: v.reshape(1, -1).astype(F32)

    cos, sin, cos_t, sin_t = _rope_tables(positions.reshape(t, 1), positions.reshape(1, t))
    k_sh = vt_sh = k_norms = None
    for layer in range(depth):
        g = norm_g[layer]
        if layer < n_a:
            w_in = a_w_in[layer]
            vo_end = 2 * M_QK_COLS + 2 * D_MODEL
            w_voq = jnp.concatenate([w_in[:, 2 * M_QK_COLS:vo_end], w_in[:, :M_QK_COLS]],
                                    axis=1).astype(BF16)
            w_kt = (w_in[:, M_QK_COLS:2 * M_QK_COLS] * M_QK_DIM ** -0.5).T.astype(BF16)
            w_gt = w_in[:, vo_end:].T.astype(BF16)
            voq, kt, gates = _a_proj(xf, row(g[0]), w_voq, w_kt, w_gt,
                                     a_b_gates[layer].reshape(-1, 1).astype(F32))
            y = _mlstm(voq, kt, gates, row(a_g_head[layer]), batch, seq)
            w_out = a_w_out[layer]
        else:
            j = layer - n_a
            if j == 0:
                w_k = _head_perm(w_kv[:, :D_MODEL]).astype(BF16)
                w_vt = w_kv[:, D_MODEL:].T.astype(BF16)
                k_sh, vt_sh, k_norms = _kv_proj(xf, row(kv_norm_g), w_k, w_vt, cos, sin,
                                                batch, seq)
            lam_init = 0.8 - 0.6 * math.exp(-0.3 * layer)
            w_qt = (_head_perm(b_w_q[j]) * (A_HEAD_DIM ** -0.5 * math.log2(math.e))
                    ).T.astype(BF16)
            q_t = _q_proj(xf, row(g[0]), w_qt, cos_t, sin_t)
            y = _attention(q_t, k_sh, vt_sh, k_norms, b_lam[j].astype(F32),
                           b_g_head[j].reshape(-1, 1).astype(F32), batch, seq, lam_init)
            w_out = b_w_out[j]
        xf = _post(xf, y, w_out.astype(BF16), row(g[1]), row(g[2]), row(g[3]),
                   mlp_w_up[layer].astype(BF16), mlp_w_down[layer].astype(BF16),
                   y_feature_major=layer >= n_a)
    return xf.reshape(batch, seq, D_MODEL)
```

```python
import functools
import math

import jax
import jax.numpy as jnp
from jax import lax
from jax.experimental import pallas as pl
from jax.experimental.pallas import tpu as pltpu

F32 = jnp.float32
BF16 = jnp.bfloat16

D_MODEL = 1024
D_FF = 4 * D_MODEL
NORM_EPS = 1e-6

M_HEADS = 4
M_V_DIM = D_MODEL // M_HEADS
M_QK_DIM = M_V_DIM // 2
M_QK_COLS = M_HEADS * M_QK_DIM
GATE_SOFTCAP = 15.0
M_CHUNK = 512

A_HEADS = 8
A_HEAD_DIM = D_MODEL // (2 * A_HEADS)
A_V_DIM = 2 * A_HEAD_DIM
ROPE_THETA = 10000.0

LANES = 128
VMEM_LIMIT_BYTES = 56 * 1024 * 1024

ROW_TILE = 512
ATT_TQ = 512
ATT_TK = 512
ATT_TRIP_TILES = (8, 4, 2)
BOUND_MAX = 40.0
BOUND_SLACK = 1.03
V_ROWS = A_V_DIM + 16

_NN = (((1,), (0,)), ((), ()))
_NT = (((1,), (1,)), ((), ()))
_TN = (((0,), (0,)), ((), ()))


def _params(*sem):
    return pltpu.CompilerParams(dimension_semantics=sem, vmem_limit_bytes=VMEM_LIMIT_BYTES)


def _rms(x, g):
    ms = jnp.mean(x * x, axis=-1, keepdims=True)
    return x * lax.rsqrt(ms + NORM_EPS) * g


def _const_spec(shape):
    nd = len(shape)
    return pl.BlockSpec(shape, lambda *_: (0,) * nd, pipeline_mode=pl.Buffered(1))


def _rope_table_kernel(pos_ref, post_ref, inv_ref, invt_ref, sign_ref,
                       cos_ref, sin_ref, cost_ref, sint_ref):
    ang = pos_ref[...].astype(F32) * inv_ref[...]
    cos_ref[...] = jnp.cos(ang)
    sin_ref[...] = jnp.sin(ang) * sign_ref[...]
    ang_t = invt_ref[...] * post_ref[...].astype(F32)
    cost_ref[...] = jnp.cos(ang_t)
    sint_ref[...] = jnp.sin(ang_t)


def _rope_tables(pos, pos_row):
    t = pos.shape[0]
    half = A_HEAD_DIM // 2
    inv = 1.0 / (ROPE_THETA ** (jnp.arange(0, A_HEAD_DIM, 2, dtype=F32) / A_HEAD_DIM))
    inv_t = jnp.tile(inv, 2).reshape(A_HEAD_DIM, 1)
    inv = jnp.tile(inv, LANES // half).reshape(1, LANES)
    sign = jnp.where(jnp.arange(LANES) < LANES // 2, -1.0, 1.0).astype(F32).reshape(1, LANES)
    tm = ROW_TILE
    return pl.pallas_call(
        _rope_table_kernel,
        grid=(t // tm,),
        in_specs=[pl.BlockSpec((tm, 1), lambda i: (i, 0)),
                  pl.BlockSpec((1, tm), lambda i: (0, i)),
                  _const_spec((1, LANES)), _const_spec((A_HEAD_DIM, 1)), _const_spec((1, LANES))],
        out_specs=[pl.BlockSpec((tm, LANES), lambda i: (i, 0))] * 2
        + [pl.BlockSpec((A_HEAD_DIM, tm), lambda i: (0, i))] * 2,
        out_shape=[jax.ShapeDtypeStruct((t, LANES), F32)] * 2
        + [jax.ShapeDtypeStruct((A_HEAD_DIM, t), F32)] * 2,
        compiler_params=_params("parallel"),
        name="rope_tables",
    )(pos, pos_row, inv, inv_t, sign)


def _rope(t, cos, sin):
    return t * cos + pltpu.roll(t, LANES // 2, axis=1) * sin


def _a_proj_kernel(x_ref, g_ref, w_ref, wkt_ref, wgt_ref, bg_ref, voq_ref, kt_ref, gates_ref):
    hb = _rms(x_ref[...], g_ref[...]).astype(BF16)
    n_out = voq_ref.shape[1]
    for n in range(0, n_out, 512):
        voq_ref[:, n:n + 512] = jnp.dot(
            hb, w_ref[:, n:n + 512], preferred_element_type=F32).astype(BF16)
    kt_ref[...] = lax.dot_general(wkt_ref[...], hb, _NT,
                                  preferred_element_type=F32).astype(BF16)
    gr = lax.dot_general(wgt_ref[...], hb, _NT, preferred_element_type=F32) + bg_ref[...]
    gr = GATE_SOFTCAP * jnp.tanh(gr / GATE_SOFTCAP)
    log_sig = jnp.minimum(gr, 0.0) - jnp.log1p(jnp.exp(-jnp.abs(gr)))
    row = lax.broadcasted_iota(jnp.int32, gr.shape, 0)
    gates_ref[...] = jnp.where(row < M_HEADS, gr, log_sig)


def _a_proj(x, g, w_voq, w_kt, w_gt, b_g):
    t = x.shape[0]
    tm = ROW_TILE
    n_out = w_voq.shape[1]
    return pl.pallas_call(
        _a_proj_kernel,
        grid=(t // tm,),
        in_specs=[pl.BlockSpec((tm, D_MODEL), lambda i: (i, 0)),
                  _const_spec((1, D_MODEL)),
                  _const_spec((D_MODEL, n_out)),
                  _const_spec((M_QK_COLS, D_MODEL)),
                  _const_spec((2 * M_HEADS, D_MODEL)),
                  _const_spec((2 * M_HEADS, 1))],
        out_specs=[pl.BlockSpec((tm, n_out), lambda i: (i, 0)),
                   pl.BlockSpec((M_QK_COLS, tm), lambda i: (0, i)),
                   pl.BlockSpec((2 * M_HEADS, tm), lambda i: (0, i))],
        out_shape=[jax.ShapeDtypeStruct((t, n_out), BF16),
                   jax.ShapeDtypeStruct((M_QK_COLS, t), BF16),
                   jax.ShapeDtypeStruct((2 * M_HEADS, t), F32)],
        compiler_params=_params("parallel"),
        name="mlstm_proj",
    )(x, g, w_voq, w_kt, w_gt, b_g)


def _mlstm_kernel(q_ref, kt_ref, v_ref, o_ref, g_ref, gh_ref, y_ref, c_ref, m_ref):
    L = q_ref.shape[0]

    @pl.when(pl.program_id(1) == 0)
    def _():
        c_ref[...] = jnp.zeros_like(c_ref)
        m_ref[...] = jnp.zeros_like(m_ref)

    jj = lax.broadcasted_iota(jnp.int32, (L, L), 0)
    ss = lax.broadcasted_iota(jnp.int32, (L, L), 1)
    causal = ss <= jj
    upper = (jj <= ss).astype(F32)

    gates = g_ref[...]
    cums = jnp.dot(gates, upper, preferred_element_type=F32, precision=lax.Precision.HIGHEST)
    b8 = pltpu.roll(cums, M_HEADS, axis=0)
    r8 = gates - b8
    lane = lax.broadcasted_iota(jnp.int32, r8.shape, 1)
    cm8 = r8
    shift = 1
    while shift < L:
        cm8 = jnp.maximum(cm8, jnp.where(lane >= shift, pltpu.roll(cm8, shift, axis=1), -jnp.inf))
        shift *= 2
    stack = jnp.concatenate([cm8, b8, jnp.zeros((LANES - 16, L), F32)], axis=0)
    cols = stack.T
    lane1 = lax.broadcasted_iota(jnp.int32, (1, LANES), 1)
    m_prev = [m_ref[h] for h in range(M_HEADS)]
    m_lanes = sum(jnp.where(lane1 == h, m_prev[h], 0.0) for h in range(M_HEADS))
    m_cols = jnp.maximum(cols, m_lanes)
    b_cols = pltpu.roll(cols, LANES - 8, axis=1)
    is_head = lane1 < M_HEADS
    s_inter = jnp.where(is_head, jnp.exp(m_lanes - m_cols), 0.0).astype(BF16)
    floor = jnp.where(is_head, jnp.exp(-(b_cols + m_cols)), 0.0).astype(BF16)
    spread = (lax.broadcasted_iota(jnp.int32, (LANES, M_HEADS * LANES), 0)
              == lax.broadcasted_iota(jnp.int32, (LANES, M_HEADS * LANES), 1) // LANES
              ).astype(BF16)
    s_inter = jnp.dot(s_inter, spread, preferred_element_type=F32)
    floor = jnp.dot(floor, spread, preferred_element_type=F32)
    ones_blk = jnp.ones((L, LANES), BF16)
    mean_blk = jnp.full((M_V_DIM, LANES), 1.0 / M_V_DIM, BF16)

    for h in range(M_HEADS):
        lanes_h = slice(h * LANES, (h + 1) * LANES)
        q = q_ref[:, lanes_h]
        kt = kt_ref[lanes_h, :]
        v_ext = jnp.concatenate([v_ref[:, h * M_V_DIM:(h + 1) * M_V_DIM], ones_blk], axis=1)
        c_prev = c_ref[h]
        r_row = r8[h:h + 1]

        w_intra = jnp.exp(jnp.where(causal, r_row, -jnp.inf) - m_cols[:, h:h + 1])
        s_qk = jnp.dot(q, kt, preferred_element_type=F32) * w_intra
        s_in = s_inter[:, lanes_h]
        ext = (jnp.concatenate([s_in, s_in, s_in], axis=1)
               * jnp.dot(q, c_prev.astype(BF16), preferred_element_type=F32)
               + jnp.dot(s_qk.astype(BF16), v_ext, preferred_element_type=F32))
        den = ext[:, M_V_DIM:]
        inv = 1.0 / jnp.maximum(jnp.abs(den), floor[:, lanes_h])
        h_out = ext[:, :M_V_DIM] * jnp.concatenate([inv, inv], axis=1)

        m_last = jnp.maximum(cm8[h:h + 1, L - 1:L], m_prev[h])
        decay = jnp.exp(m_prev[h] - m_last)
        kwt = (kt.astype(F32) * jnp.exp(r_row - m_last)).astype(BF16)
        c_ref[h] = decay * c_prev + jnp.dot(kwt, v_ext, preferred_element_type=F32)
        m_ref[h] = b8[h:h + 1, L - 1:L] + m_last

        sl = slice(h * M_V_DIM, (h + 1) * M_V_DIM)
        ms = jnp.dot((h_out * h_out).astype(BF16), mean_blk, preferred_element_type=F32)
        rs = lax.rsqrt(ms + NORM_EPS)
        og = jax.nn.sigmoid(o_ref[:, sl].astype(F32))
        y_ref[:, sl] = (og * h_out * jnp.concatenate([rs, rs], axis=1)
                        * gh_ref[:, sl]).astype(BF16)


def _mlstm(voq, kt, gates, g_head, batch, seq):
    L = M_CHUNK
    nc = seq // L
    t = batch * seq
    row = lambda b, c: b * nc + c
    return pl.pallas_call(
        _mlstm_kernel,
        grid=(batch, nc),
        in_specs=[pl.BlockSpec((L, M_QK_COLS), lambda b, c: (row(b, c), 2 * D_MODEL // M_QK_COLS)),
                  pl.BlockSpec((M_QK_COLS, L), lambda b, c: (0, row(b, c))),
                  pl.BlockSpec((L, D_MODEL), lambda b, c: (row(b, c), 0)),
                  pl.BlockSpec((L, D_MODEL), lambda b, c: (row(b, c), 1)),
                  pl.BlockSpec((2 * M_HEADS, L), lambda b, c: (0, row(b, c))),
                  pl.BlockSpec((1, D_MODEL), lambda b, c: (0, 0))],
        out_specs=pl.BlockSpec((L, D_MODEL), lambda b, c: (row(b, c), 0)),
        out_shape=jax.ShapeDtypeStruct((t, D_MODEL), BF16),
        scratch_shapes=[pltpu.VMEM((M_HEADS, M_QK_DIM, M_V_DIM + LANES), F32),
                        pltpu.VMEM((M_HEADS, 1, 1), F32)],
        compiler_params=_params("parallel", "arbitrary"),
        name="mlstm_mixer",
    )(voq, kt, voq, voq, gates, g_head)


def _post_kernel(x_ref, y_ref, wo_ref, g1_ref, g2_ref, g3_ref, wu_ref, wd_ref, out_ref, *,
                 y_feature_major):
    z = lax.dot_general(y_ref[...], wo_ref[...], _TN if y_feature_major else _NN,
                        preferred_element_type=F32)
    x1 = x_ref[...] + _rms(z, g1_ref[...])
    hb = _rms(x1, g2_ref[...]).astype(BF16)
    acc = jnp.zeros(x1.shape, F32)
    for f in range(0, D_FF, 1024):
        u = jnp.maximum(jnp.dot(hb, wu_ref[:, f:f + 1024], preferred_element_type=F32), 0.0)
        acc = acc + jnp.dot((u * u).astype(BF16), wd_ref[f:f + 1024, :],
                            preferred_element_type=F32)
    out_ref[...] = x1 + _rms(acc, g3_ref[...])


def _post(x, y, w_out, g1, g2, g3, w_up, w_down, y_feature_major):
    t = x.shape[0]
    tm = ROW_TILE
    y_spec = (pl.BlockSpec((D_MODEL, tm), lambda i: (0, i)) if y_feature_major
              else pl.BlockSpec((tm, D_MODEL), lambda i: (i, 0)))
    return pl.pallas_call(
        functools.partial(_post_kernel, y_feature_major=y_feature_major),
        grid=(t // tm,),
        in_specs=[pl.BlockSpec((tm, D_MODEL), lambda i: (i, 0)),
                  y_spec,
                  _const_spec((D_MODEL, D_MODEL)),
                  _const_spec((1, D_MODEL)), _const_spec((1, D_MODEL)), _const_spec((1, D_MODEL)),
                  _const_spec((D_MODEL, D_FF)),
                  _const_spec((D_FF, D_MODEL))],
        out_specs=pl.BlockSpec((tm, D_MODEL), lambda i: (i, 0)),
        out_shape=jax.ShapeDtypeStruct((t, D_MODEL), F32),
        compiler_params=_params("parallel"),
        name="outproj_mlp",
    )(x, y, w_out, g1, g2, g3, w_up, w_down)


def _kv_proj_kernel(x_ref, g_ref, wk_ref, wvt_ref, cos_ref, sin_ref, k_ref, vt_ref, kn_ref):
    hb = _rms(x_ref[...], g_ref[...]).astype(BF16)
    cos = cos_ref[...]
    sin = sin_ref[...]
    kk = jnp.dot(hb, wk_ref[...], preferred_element_type=F32)
    col = lax.broadcasted_iota(jnp.int32, (D_MODEL, LANES), 0)
    out = lax.broadcasted_iota(jnp.int32, (D_MODEL, LANES), 1)
    pick = (out == 2 * (col >> 7) + ((col >> 5) & 1)).astype(BF16)
    k_sq = jnp.dot((kk * kk).astype(BF16), pick, preferred_element_type=F32)
    kn_ref[0] = jnp.max(k_sq, axis=0, keepdims=True)
    for h in range(A_HEADS):
        sl = slice(h * LANES, (h + 1) * LANES)
        k_ref[:, sl] = _rope(kk[:, sl], cos, sin).astype(BF16)
    vt = lax.dot_general(wvt_ref[...], hb, _NT, preferred_element_type=F32)
    ones = jnp.ones((V_ROWS - A_V_DIM, vt.shape[1]), BF16)
    for h in range(A_HEADS):
        vt_ref[0, 0, h, :A_V_DIM, :] = vt[h * A_V_DIM:(h + 1) * A_V_DIM].astype(BF16)
        vt_ref[0, 0, h, A_V_DIM:, :] = ones


def _kv_proj(x, g, w_k, w_vt, cos, sin, batch, seq):
    t = x.shape[0]
    tm = ATT_TK
    nk = seq // tm
    return pl.pallas_call(
        _kv_proj_kernel,
        grid=(t // tm,),
        in_specs=[pl.BlockSpec((tm, D_MODEL), lambda i: (i, 0)),
                  _const_spec((1, D_MODEL)),
                  _const_spec((D_MODEL, D_MODEL)),
                  _const_spec((D_MODEL, D_MODEL)),
                  pl.BlockSpec((tm, LANES), lambda i: (i, 0)),
                  pl.BlockSpec((tm, LANES), lambda i: (i, 0))],
        out_specs=[pl.BlockSpec((tm, D_MODEL), lambda i: (i, 0)),
                   pl.BlockSpec((1, 1, A_HEADS, V_ROWS, tm),
                                lambda i: (i // nk, i % nk, 0, 0, 0)),
                   pl.BlockSpec((1, 1, LANES), lambda i: (i, 0, 0))],
        out_shape=[jax.ShapeDtypeStruct((t, D_MODEL), BF16),
                   jax.ShapeDtypeStruct((batch, nk, A_HEADS, V_ROWS, tm), BF16),
                   jax.ShapeDtypeStruct((t // tm, 1, LANES), F32)],
        compiler_params=_params("parallel"),
        name="kv_proj",
    )(x, g, w_k, w_vt, cos, sin)


def _q_proj_kernel(x_ref, g_ref, wqt_ref, cos_ref, sin_ref, qt_ref):
    hb = _rms(x_ref[...], g_ref[...]).astype(BF16)
    cos = cos_ref[...]
    sin = sin_ref[...]
    qq = lax.dot_general(wqt_ref[...], hb, _NT, preferred_element_type=F32)
    half = LANES // 2
    for h in range(A_HEADS):
        lo = qq[h * LANES:h * LANES + half]
        hi = qq[h * LANES + half:(h + 1) * LANES]
        qt_ref[h * LANES:h * LANES + half, :] = (lo * cos - hi * sin).astype(BF16)
        qt_ref[h * LANES + half:(h + 1) * LANES, :] = (hi * cos + lo * sin).astype(BF16)


def _q_proj(x, g, w_qt, cos_t, sin_t):
    t = x.shape[0]
    tm = ROW_TILE
    return pl.pallas_call(
        _q_proj_kernel,
        grid=(t // tm,),
        in_specs=[pl.BlockSpec((tm, D_MODEL), lambda i: (i, 0)),
                  _const_spec((1, D_MODEL)),
                  _const_spec((D_MODEL, D_MODEL)),
                  pl.BlockSpec((A_HEAD_DIM, tm), lambda i: (0, i)),
                  pl.BlockSpec((A_HEAD_DIM, tm), lambda i: (0, i))],
        out_specs=pl.BlockSpec((D_MODEL, tm), lambda i: (0, i)),
        out_shape=jax.ShapeDtypeStruct((D_MODEL, t), BF16),
        compiler_params=_params("parallel"),
        name="q_proj",
    )(x, g, w_qt, cos_t, sin_t)


def _attn_kernel(lam_ref, kn_ref, q_ref, k_ref, vt_ref, g_ref, o_ref, sa_ref, sb_ref, acc_ref, *,
                 lam_init):
    tq = q_ref.shape[1]
    tk = vt_ref.shape[4]
    head = pl.program_id(1)
    i = pl.program_id(2)

    lam = lam_ref[...]
    lam_full = (jnp.exp(jnp.sum(lam[0:1] * lam[1:2], keepdims=True))
                - jnp.exp(jnp.sum(lam[2:3] * lam[3:4], keepdims=True)) + lam_init)

    q_t = q_ref[...].astype(F32)
    row_map = (lax.broadcasted_iota(jnp.int32, q_t.shape, 0) >> 5) & 1
    q_cat = jnp.concatenate([jnp.where(row_map == c, q_t, 0.0) for c in range(2)],
                            axis=1).astype(BF16)

    k_sq = jnp.max(kn_ref[...], axis=0)
    lane = lax.broadcasted_iota(jnp.int32, k_sq.shape, 1)
    q_sq = q_t * q_t
    bound = jnp.concatenate(
        [jnp.sqrt(jnp.sum(jnp.where(row_map == c, q_sq, 0.0), axis=0, keepdims=True)
                  * jnp.max(jnp.where(lane == 2 * head + c, k_sq, 0.0), axis=1, keepdims=True))
         for c in range(2)], axis=1) * BOUND_SLACK

    def causal(j, s):
        key_i = j * tk + lax.broadcasted_iota(jnp.int32, s.shape, 0)
        qry_i = i * tq + (lax.broadcasted_iota(jnp.int32, s.shape, 1) & (tq - 1))
        return jnp.where(key_i <= qry_i, s, -jnp.inf)

    n_full = (i * tq) // tk

    def bounded():
        def tile(j, diagonal):
            kb = k_ref[pl.ds(pl.multiple_of(j * tk, tk), tk), :]
            s = jnp.dot(kb, q_cat, preferred_element_type=F32) - bound
            if diagonal:
                s = causal(j, s)
            return jnp.dot(vt_ref[0, j, 0], jnp.exp2(s).astype(BF16),
                           preferred_element_type=F32)

        def trip(n_tiles, first):
            def body(jj, carry):
                j = first + n_tiles * jj
                total = tile(j, False)
                for t in range(1, n_tiles):
                    total = total + tile(j + t, False)
                acc_ref[...] += total
                return carry
            return body

        n_even = n_full - n_full % 2

        def odd_start():
            acc_ref[...] = tile(n_even, False) + tile(n_full, True)

        def even_start():
            acc_ref[...] = tile(n_full, True)

        lax.cond(n_even < n_full, odd_start, even_start)
        done = 0
        for n_tiles in ATT_TRIP_TILES:
            n_trips = (n_even - done) // n_tiles
            lax.fori_loop(0, n_trips, trip(n_tiles, done), 0)
            done = done + n_trips * n_tiles

    def online():
        def scores(j, s_ref):
            kb = k_ref[pl.ds(pl.multiple_of(j * tk, tk), tk), :]
            s_ref[...] = jnp.dot(kb, q_cat, preferred_element_type=F32)

        def softmax_pv(j, s_ref, m_old, diagonal):
            s = s_ref[...]
            if diagonal:
                s = causal(j, s)
            m_new = jnp.maximum(m_old, jnp.max(s, axis=0, keepdims=True))
            alpha = jnp.exp2(m_old - m_new)
            p = jnp.exp2(s - m_new).astype(BF16)
            acc_ref[...] = alpha * acc_ref[...] + jnp.dot(vt_ref[0, j, 0], p,
                                                          preferred_element_type=F32)
            return m_new

        acc_ref[...] = jnp.zeros_like(acc_ref)
        scores(0, sa_ref)

        def pair(jj, m_run):
            j = 2 * jj
            scores(j + 1, sb_ref)
            m_run = softmax_pv(j, sa_ref, m_run, False)
            scores(j + 2, sa_ref)
            return softmax_pv(j + 1, sb_ref, m_run, False)

        m_run = lax.fori_loop(0, n_full // 2, pair, jnp.full((1, 2 * tq), -jnp.inf, F32))

        def tail_odd(m_old):
            scores(n_full, sb_ref)
            m_mid = softmax_pv(n_full - 1, sa_ref, m_old, False)
            return softmax_pv(n_full, sb_ref, m_mid, True)

        def tail_even(m_old):
            return softmax_pv(n_full, sa_ref, m_old, True)

        lax.cond(n_full % 2 == 1, tail_odd, tail_even, m_run)

    lax.cond(jnp.max(bound) <= BOUND_MAX, bounded, online)

    acc = acc_ref[...]
    inv_l = 1.0 / acc[A_V_DIM:A_V_DIM + 1]
    o = acc[:A_V_DIM, :tq] * inv_l[:, :tq] - lam_full * (acc[:A_V_DIM, tq:] * inv_l[:, tq:])
    ms = jnp.mean(o * o, axis=0, keepdims=True)
    o = o * lax.rsqrt(ms + NORM_EPS) * g_ref[...] * (1.0 - lam_init)
    o_ref[...] = o.astype(BF16)


def _attention(q, k, vt, k_norms, lam, g_col, batch, seq, lam_init):
    tq, tk = ATT_TQ, ATT_TK
    assert tk % tq == 0 and tq & (tq - 1) == 0
    nq = seq // tq
    nk = seq // tk
    t = batch * seq
    return pl.pallas_call(
        functools.partial(_attn_kernel, lam_init=lam_init),
        grid=(batch, A_HEADS, nq),
        in_specs=[pl.BlockSpec((4, A_HEAD_DIM), lambda b, h, i: (0, 0)),
                  pl.BlockSpec((nk, 1, LANES), lambda b, h, i: (b, 0, 0)),
                  pl.BlockSpec((LANES, tq), lambda b, h, i: (h, b * nq + i)),
                  pl.BlockSpec((seq, LANES), lambda b, h, i: (b, h)),
                  pl.BlockSpec((1, nk, 1, V_ROWS, tk), lambda b, h, i: (b, 0, h, 0, 0)),
                  pl.BlockSpec((A_V_DIM, 1), lambda b, h, i: (h, 0))],
        out_specs=pl.BlockSpec((A_V_DIM, tq), lambda b, h, i: (h, b * nq + i)),
        out_shape=jax.ShapeDtypeStruct((D_MODEL, t), BF16),
        scratch_shapes=[pltpu.VMEM((tk, 2 * tq), F32),
                        pltpu.VMEM((tk, 2 * tq), F32),
                        pltpu.VMEM((V_ROWS, 2 * tq), F32)],
        compiler_params=_params("parallel", "parallel", "arbitrary"),
        name="diff_attention",
    )(lam, k_norms, q, k, vt, g_col)


def _head_perm(w):
    half = A_HEAD_DIM // 2
    w = w.reshape(w.shape[0], A_HEADS, 2, 2, half)
    return w.transpose(0, 1, 3, 2, 4).reshape(w.shape[0], D_MODEL)


def kernel(x, positions, norm_g, a_w_in, a_b_gates, a_g_head, a_w_out, kv_norm_g, w_kv,
           b_w_q, b_lam, b_g_head, b_w_out, mlp_w_up, mlp_w_down):
    batch, seq, _ = x.shape
    t = batch * seq
    depth = norm_g.shape[0]
    n_a = a_w_in.shape[0]
    xf = x.reshape(t, D_MODEL)
    row = lambda v: v.reshape(1, -1).astype(F32)

    cos, sin, cos_t, sin_t = _rope_tables(positions.reshape(t, 1), positions.reshape(1, t))
    k_sh = vt_sh = k_norms = None
    for layer in range(depth):
        g = norm_g[layer]
        if layer < n_a:
            w_in = a_w_in[layer]
            vo_end = 2 * M_QK_COLS + 2 * D_MODEL
            w_voq = jnp.concatenate([w_in[:, 2 * M_QK_COLS:vo_end], w_in[:, :M_QK_COLS]],
                                    axis=1).astype(BF16)
            w_kt = (w_in[:, M_QK_COLS:2 * M_QK_COLS] * M_QK_DIM ** -0.5).T.astype(BF16)
            w_gt = w_in[:, vo_end:].T.astype(BF16)
            voq, kt, gates = _a_proj(xf, row(g[0]), w_voq, w_kt, w_gt,
                                     a_b_gates[layer].reshape(-1, 1).astype(F32))
            y = _mlstm(voq, kt, gates, row(a_g_head[layer]), batch, seq)
            w_out = a_w_out[layer]
        else:
            j = layer - n_a
            if j == 0:
                w_k = _head_perm(w_kv[:, :D_MODEL]).astype(BF16)
                w_vt = w_kv[:, D_MODEL:].T.astype(BF16)
                k_sh, vt_sh, k_norms = _kv_proj(xf, row(kv_norm_g), w_k, w_vt, cos, sin,
                                                batch, seq)
            lam_init = 0.8 - 0.6 * math.exp(-0.3 * layer)
            w_qt = (_head_perm(b_w_q[j]) * (A_HEAD_DIM ** -0.5 * math.log2(math.e))
                    ).T.astype(BF16)
            q_t = _q_proj(xf, row(g[0]), w_qt, cos_t, sin_t)
            y = _attention(q_t, k_sh, vt_sh, k_norms, b_lam[j].astype(F32),
                           b_g_head[j].reshape(-1, 1).astype(F32), batch, seq, lam_init)
            w_out = b_w_out[j]
        xf = _post(xf, y, w_out.astype(BF16), row(g[1]), row(g[2]), row(g[3]),
                   mlp_w_up[layer].astype(BF16), mlp_w_down[layer].astype(BF16),
                   y_feature_major=layer >= n_a)
    return xf.reshape(batch, seq, D_MODEL)
```

```python
import functools
import math

import jax
import jax.numpy as jnp
from jax import lax
from jax.experimental import pallas as pl
from jax.experimental.pallas import tpu as pltpu

F32 = jnp.float32
BF16 = jnp.bfloat16

D_MODEL = 1024
D_FF = 4 * D_MODEL
NORM_EPS = 1e-6

M_HEADS = 4
M_V_DIM = D_MODEL // M_HEADS
M_QK_DIM = M_V_DIM // 2
M_QK_COLS = M_HEADS * M_QK_DIM
GATE_SOFTCAP = 15.0
M_CHUNK = 512

A_HEADS = 8
A_HEAD_DIM = D_MODEL // (2 * A_HEADS)
A_V_DIM = 2 * A_HEAD_DIM
ROPE_THETA = 10000.0

LANES = 128
VMEM_LIMIT_BYTES = 56 * 1024 * 1024

ROW_TILE = 512
ATT_TQ = 512
ATT_TK = 512
ATT_TRIP_TILES = (8, 4, 2)
BOUND_MAX = 40.0
BOUND_SLACK = 1.03
V_ROWS = A_V_DIM + 16

_NN = (((1,), (0,)), ((), ()))
_NT = (((1,), (1,)), ((), ()))
_TN = (((0,), (0,)), ((), ()))


def _params(*sem):
    return pltpu.CompilerParams(dimension_semantics=sem, vmem_limit_bytes=VMEM_LIMIT_BYTES)


def _rms(x, g):
    ms = jnp.mean(x * x, axis=-1, keepdims=True)
    return x * lax.rsqrt(ms + NORM_EPS) * g


def _const_spec(shape):
    nd = len(shape)
    return pl.BlockSpec(shape, lambda *_: (0,) * nd, pipeline_mode=pl.Buffered(1))


def _rope_table_kernel(pos_ref, inv_ref, sign_ref, cos_ref, sin_ref, cost_ref, sint_ref):
    ang = inv_ref[...] * pos_ref[...].astype(F32)
    cos32 = jnp.cos(ang)
    sin32 = jnp.sin(ang)
    cost_ref[...] = jnp.concatenate([cos32, cos32], axis=0)
    sint_ref[...] = jnp.concatenate([sin32, sin32], axis=0)
    cos_ref[...] = jnp.concatenate([cos32] * 4, axis=0).T
    sin_ref[...] = jnp.concatenate([sin32] * 4, axis=0).T * sign_ref[...]


def _rope_tables(pos_row):
    t = pos_row.shape[1]
    half = A_HEAD_DIM // 2
    inv = 1.0 / (ROPE_THETA ** (jnp.arange(0, A_HEAD_DIM, 2, dtype=F32) / A_HEAD_DIM))
    sign = jnp.where(jnp.arange(LANES) < LANES // 2, -1.0, 1.0).astype(F32).reshape(1, LANES)
    tm = ROW_TILE
    return pl.pallas_call(
        _rope_table_kernel,
        grid=(t // tm,),
        in_specs=[pl.BlockSpec((1, tm), lambda i: (0, i)),
                  _const_spec((half, 1)), _const_spec((1, LANES))],
        out_specs=[pl.BlockSpec((tm, LANES), lambda i: (i, 0))] * 2
        + [pl.BlockSpec((A_HEAD_DIM, tm), lambda i: (0, i))] * 2,
        out_shape=[jax.ShapeDtypeStruct((t, LANES), F32)] * 2
        + [jax.ShapeDtypeStruct((A_HEAD_DIM, t), F32)] * 2,
        compiler_params=_params("parallel"),
        name="rope_tables",
    )(pos_row, inv.reshape(half, 1), sign)


def _rope(t, cos, sin):
    return t * cos + pltpu.roll(t, LANES // 2, axis=1) * sin


def _a_proj_kernel(x_ref, g_ref, w_ref, wkt_ref, wgt_ref, bg_ref, voq_ref, kt_ref, gates_ref):
    hb = _rms(x_ref[...], g_ref[...]).astype(BF16)
    n_out = voq_ref.shape[1]
    for n in range(0, n_out, 512):
        voq_ref[:, n:n + 512] = jnp.dot(
            hb, w_ref[:, n:n + 512], preferred_element_type=F32).astype(BF16)
    kt_ref[...] = lax.dot_general(wkt_ref[...], hb, _NT,
                                  preferred_element_type=F32).astype(BF16)
    gr = lax.dot_general(wgt_ref[...], hb, _NT, preferred_element_type=F32) + bg_ref[...]
    gr = GATE_SOFTCAP * jnp.tanh(gr / GATE_SOFTCAP)
    log_sig = jnp.minimum(gr, 0.0) - jnp.log1p(jnp.exp(-jnp.abs(gr)))
    row = lax.broadcasted_iota(jnp.int32, gr.shape, 0)
    gates_ref[...] = jnp.where(row < M_HEADS, gr, log_sig)


def _a_proj(x, g, w_voq, w_kt, w_gt, b_g):
    t = x.shape[0]
    tm = ROW_TILE
    n_out = w_voq.shape[1]
    return pl.pallas_call(
        _a_proj_kernel,
        grid=(t // tm,),
        in_specs=[pl.BlockSpec((tm, D_MODEL), lambda i: (i, 0)),
                  _const_spec((1, D_MODEL)),
                  _const_spec((D_MODEL, n_out)),
                  _const_spec((M_QK_COLS, D_MODEL)),
                  _const_spec((2 * M_HEADS, D_MODEL)),
                  _const_spec((2 * M_HEADS, 1))],
        out_specs=[pl.BlockSpec((tm, n_out), lambda i: (i, 0)),
                   pl.BlockSpec((M_QK_COLS, tm), lambda i: (0, i)),
                   pl.BlockSpec((2 * M_HEADS, tm), lambda i: (0, i))],
        out_shape=[jax.ShapeDtypeStruct((t, n_out), BF16),
                   jax.ShapeDtypeStruct((M_QK_COLS, t), BF16),
                   jax.ShapeDtypeStruct((2 * M_HEADS, t), F32)],
        compiler_params=_params("parallel"),
        name="mlstm_proj",
    )(x, g, w_voq, w_kt, w_gt, b_g)


def _mlstm_kernel(q_ref, kt_ref, v_ref, o_ref, g_ref, gh_ref, y_ref, c_ref, m_ref):
    L = q_ref.shape[0]

    @pl.when(pl.program_id(1) == 0)
    def _():
        c_ref[...] = jnp.zeros_like(c_ref)
        m_ref[...] = jnp.zeros_like(m_ref)

    jj = lax.broadcasted_iota(jnp.int32, (L, L), 0)
    ss = lax.broadcasted_iota(jnp.int32, (L, L), 1)
    causal = ss <= jj
    upper = (jj <= ss).astype(F32)

    gates = g_ref[...]
    cums = jnp.dot(gates, upper, preferred_element_type=F32, precision=lax.Precision.HIGHEST)
    b8 = pltpu.roll(cums, M_HEADS, axis=0)
    r8 = gates - b8
    lane = lax.broadcasted_iota(jnp.int32, r8.shape, 1)
    cm8 = r8
    shift = 1
    while shift < L:
        cm8 = jnp.maximum(cm8, jnp.where(lane >= shift, pltpu.roll(cm8, shift, axis=1), -jnp.inf))
        shift *= 2
    stack = jnp.concatenate([cm8, b8, jnp.zeros((LANES - 16, L), F32)], axis=0)
    cols = stack.T
    lane1 = lax.broadcasted_iota(jnp.int32, (1, LANES), 1)
    m_prev = [m_ref[h] for h in range(M_HEADS)]
    m_lanes = sum(jnp.where(lane1 == h, m_prev[h], 0.0) for h in range(M_HEADS))
    m_cols = jnp.maximum(cols, m_lanes)
    b_cols = pltpu.roll(cols, LANES - 8, axis=1)
    is_head = lane1 < M_HEADS
    s_inter = jnp.where(is_head, jnp.exp(m_lanes - m_cols), 0.0).astype(BF16)
    floor = jnp.where(is_head, jnp.exp(-(b_cols + m_cols)), 0.0).astype(BF16)
    spread = (lax.broadcasted_iota(jnp.int32, (LANES, M_HEADS * LANES), 0)
              == lax.broadcasted_iota(jnp.int32, (LANES, M_HEADS * LANES), 1) // LANES
              ).astype(BF16)
    s_inter = jnp.dot(s_inter, spread, preferred_element_type=F32)
    floor = jnp.dot(floor, spread, preferred_element_type=F32)
    ones_blk = jnp.ones((L, LANES), BF16)
    mean_blk = jnp.full((M_V_DIM, LANES), 1.0 / M_V_DIM, BF16)

    for h in range(M_HEADS):
        lanes_h = slice(h * LANES, (h + 1) * LANES)
        q = q_ref[:, lanes_h]
        kt = kt_ref[lanes_h, :]
        v_ext = jnp.concatenate([v_ref[:, h * M_V_DIM:(h + 1) * M_V_DIM], ones_blk], axis=1)
        c_prev = c_ref[h]
        r_row = r8[h:h + 1]

        w_intra = jnp.exp(jnp.where(causal, r_row, -jnp.inf) - m_cols[:, h:h + 1])
        s_qk = jnp.dot(q, kt, preferred_element_type=F32) * w_intra
        s_in = s_inter[:, lanes_h]
        ext = (jnp.concatenate([s_in, s_in, s_in], axis=1)
               * jnp.dot(q, c_prev.astype(BF16), preferred_element_type=F32)
               + jnp.dot(s_qk.astype(BF16), v_ext, preferred_element_type=F32))
        den = ext[:, M_V_DIM:]
        inv = 1.0 / jnp.maximum(jnp.abs(den), floor[:, lanes_h])
        h_out = ext[:, :M_V_DIM] * jnp.concatenate([inv, inv], axis=1)

        m_last = jnp.maximum(cm8[h:h + 1, L - 1:L], m_prev[h])
        decay = jnp.exp(m_prev[h] - m_last)
        kwt = (kt.astype(F32) * jnp.exp(r_row - m_last)).astype(BF16)
        c_ref[h] = decay * c_prev + jnp.dot(kwt, v_ext, preferred_element_type=F32)
        m_ref[h] = b8[h:h + 1, L - 1:L] + m_last

        sl = slice(h * M_V_DIM, (h + 1) * M_V_DIM)
        ms = jnp.dot((h_out * h_out).astype(BF16), mean_blk, preferred_element_type=F32)
        rs = lax.rsqrt(ms + NORM_EPS)
        og = jax.nn.sigmoid(o_ref[:, sl].astype(F32))
        y_ref[:, sl] = (og * h_out * jnp.concatenate([rs, rs], axis=1)
                        * gh_ref[:, sl]).astype(BF16)


def _mlstm(voq, kt, gates, g_head, batch, seq):
    L = M_CHUNK
    nc = seq // L
    t = batch * seq
    row = lambda b, c: b * nc + c
    return pl.pallas_call(
        _mlstm_kernel,
        grid=(batch, nc),
        in_specs=[pl.BlockSpec((L, M_QK_COLS), lambda b, c: (row(b, c), 2 * D_MODEL // M_QK_COLS)),
                  pl.BlockSpec((M_QK_COLS, L), lambda b, c: (0, row(b, c))),
                  pl.BlockSpec((L, D_MODEL), lambda b, c: (row(b, c), 0)),
                  pl.BlockSpec((L, D_MODEL), lambda b, c: (row(b, c), 1)),
                  pl.BlockSpec((2 * M_HEADS, L), lambda b, c: (0, row(b, c))),
                  pl.BlockSpec((1, D_MODEL), lambda b, c: (0, 0))],
        out_specs=pl.BlockSpec((L, D_MODEL), lambda b, c: (row(b, c), 0)),
        out_shape=jax.ShapeDtypeStruct((t, D_MODEL), BF16),
        scratch_shapes=[pltpu.VMEM((M_HEADS, M_QK_DIM, M_V_DIM + LANES), F32),
                        pltpu.VMEM((M_HEADS, 1, 1), F32)],
        compiler_params=_params("parallel", "arbitrary"),
        name="mlstm_mixer",
    )(voq, kt, voq, voq, gates, g_head)


def _post_kernel(x_ref, y_ref, wo_ref, g1_ref, g2_ref, g3_ref, wu_ref, wd_ref, out_ref, *,
                 y_feature_major):
    z = lax.dot_general(y_ref[...], wo_ref[...], _TN if y_feature_major else _NN,
                        preferred_element_type=F32)
    x1 = x_ref[...] + _rms(z, g1_ref[...])
    hb = _rms(x1, g2_ref[...]).astype(BF16)
    acc = jnp.zeros(x1.shape, F32)
    for f in range(0, D_FF, 1024):
        u = jnp.maximum(jnp.dot(hb, wu_ref[:, f:f + 1024], preferred_element_type=F32), 0.0)
        acc = acc + jnp.dot((u * u).astype(BF16), wd_ref[f:f + 1024, :],
                            preferred_element_type=F32)
    out_ref[...] = x1 + _rms(acc, g3_ref[...])


def _post(x, y, w_out, g1, g2, g3, w_up, w_down, y_feature_major):
    t = x.shape[0]
    tm = ROW_TILE
    y_spec = (pl.BlockSpec((D_MODEL, tm), lambda i: (0, i)) if y_feature_major
              else pl.BlockSpec((tm, D_MODEL), lambda i: (i, 0)))
    return pl.pallas_call(
        functools.partial(_post_kernel, y_feature_major=y_feature_major),
        grid=(t // tm,),
        in_specs=[pl.BlockSpec((tm, D_MODEL), lambda i: (i, 0)),
                  y_spec,
                  _const_spec((D_MODEL, D_MODEL)),
                  _const_spec((1, D_MODEL)), _const_spec((1, D_MODEL)), _const_spec((1, D_MODEL)),
                  _const_spec((D_MODEL, D_FF)),
                  _const_spec((D_FF, D_MODEL))],
        out_specs=pl.BlockSpec((tm, D_MODEL), lambda i: (i, 0)),
        out_shape=jax.ShapeDtypeStruct((t, D_MODEL), F32),
        compiler_params=_params("parallel"),
        name="outproj_mlp",
    )(x, y, w_out, g1, g2, g3, w_up, w_down)


def _kv_proj_kernel(x_ref, g_ref, wk_ref, wvt_ref, cos_ref, sin_ref, k_ref, vt_ref, kn_ref):
    hb = _rms(x_ref[...], g_ref[...]).astype(BF16)
    cos = cos_ref[...]
    sin = sin_ref[...]
    kk = jnp.dot(hb, wk_ref[...], preferred_element_type=F32)
    col = lax.broadcasted_iota(jnp.int32, (D_MODEL, LANES), 0)
    out = lax.broadcasted_iota(jnp.int32, (D_MODEL, LANES), 1)
    pick = (out == 2 * (col >> 7) + ((col >> 5) & 1)).astype(BF16)
    k_sq = jnp.dot((kk * kk).astype(BF16), pick, preferred_element_type=F32)
    kn_ref[0] = jnp.max(k_sq, axis=0, keepdims=True)
    for h in range(A_HEADS):
        sl = slice(h * LANES, (h + 1) * LANES)
        k_ref[:, sl] = _rope(kk[:, sl], cos, sin).astype(BF16)
    vt = lax.dot_general(wvt_ref[...], hb, _NT, preferred_element_type=F32)
    ones = jnp.ones((V_ROWS - A_V_DIM, vt.shape[1]), BF16)
    for h in range(A_HEADS):
        vt_ref[0, 0, h, :A_V_DIM, :] = vt[h * A_V_DIM:(h + 1) * A_V_DIM].astype(BF16)
        vt_ref[0, 0, h, A_V_DIM:, :] = ones


def _kv_proj(x, g, w_k, w_vt, cos, sin, batch, seq):
    t = x.shape[0]
    tm = ATT_TK
    nk = seq // tm
    return pl.pallas_call(
        _kv_proj_kernel,
        grid=(t // tm,),
        in_specs=[pl.BlockSpec((tm, D_MODEL), lambda i: (i, 0)),
                  _const_spec((1, D_MODEL)),
                  _const_spec((D_MODEL, D_MODEL)),
                  _const_spec((D_MODEL, D_MODEL)),
                  pl.BlockSpec((tm, LANES), lambda i: (i, 0)),
                  pl.BlockSpec((tm, LANES), lambda i: (i, 0))],
        out_specs=[pl.BlockSpec((tm, D_MODEL), lambda i: (i, 0)),
                   pl.BlockSpec((1, 1, A_HEADS, V_ROWS, tm),
                                lambda i: (i // nk, i % nk, 0, 0, 0)),
                   pl.BlockSpec((1, 1, LANES), lambda i: (i, 0, 0))],
        out_shape=[jax.ShapeDtypeStruct((t, D_MODEL), BF16),
                   jax.ShapeDtypeStruct((batch, nk, A_HEADS, V_ROWS, tm), BF16),
                   jax.ShapeDtypeStruct((t // tm, 1, LANES), F32)],
        compiler_params=_params("parallel"),
        name="kv_proj",
    )(x, g, w_k, w_vt, cos, sin)


def _q_proj_kernel(x_ref, g_ref, wqt_ref, cos_ref, sin_ref, qt_ref):
    hb = _rms(x_ref[...], g_ref[...]).astype(BF16)
    cos = cos_ref[...]
    sin = sin_ref[...]
    qq = lax.dot_general(wqt_ref[...], hb, _NT, preferred_element_type=F32)
    half = LANES // 2
    for h in range(A_HEADS):
        lo = qq[h * LANES:h * LANES + half]
        hi = qq[h * LANES + half:(h + 1) * LANES]
        qt_ref[h * LANES:h * LANES + half, :] = (lo * cos - hi * sin).astype(BF16)
        qt_ref[h * LANES + half:(h + 1) * LANES, :] = (hi * cos + lo * sin).astype(BF16)


def _q_proj(x, g, w_qt, cos_t, sin_t):
    t = x.shape[0]
    tm = ROW_TILE
    return pl.pallas_call(
        _q_proj_kernel,
        grid=(t // tm,),
        in_specs=[pl.BlockSpec((tm, D_MODEL), lambda i: (i, 0)),
                  _const_spec((1, D_MODEL)),
                  _const_spec((D_MODEL, D_MODEL)),
                  pl.BlockSpec((A_HEAD_DIM, tm), lambda i: (0, i)),
                  pl.BlockSpec((A_HEAD_DIM, tm), lambda i: (0, i))],
        out_specs=pl.BlockSpec((D_MODEL, tm), lambda i: (0, i)),
        out_shape=jax.ShapeDtypeStruct((D_MODEL, t), BF16),
        compiler_params=_params("parallel"),
        name="q_proj",
    )(x, g, w_qt, cos_t, sin_t)


def _attn_kernel(lam_ref, kn_ref, q_ref, k_ref, vt_ref, g_ref, o_ref, sa_ref, sb_ref, acc_ref, *,
                 lam_init):
    tq = q_ref.shape[1]
    tk = vt_ref.shape[4]
    head = pl.program_id(1)
    i = pl.program_id(2)

    lam = lam_ref[...]
    lam_full = (jnp.exp(jnp.sum(lam[0:1] * lam[1:2], keepdims=True))
                - jnp.exp(jnp.sum(lam[2:3] * lam[3:4], keepdims=True)) + lam_init)

    q_t = q_ref[...].astype(F32)
    row_map = (lax.broadcasted_iota(jnp.int32, q_t.shape, 0) >> 5) & 1
    q_cat = jnp.concatenate([jnp.where(row_map == c, q_t, 0.0) for c in range(2)],
                            axis=1).astype(BF16)

    k_sq = jnp.max(kn_ref[...], axis=0)
    lane = lax.broadcasted_iota(jnp.int32, k_sq.shape, 1)
    q_sq = q_t * q_t
    bound = jnp.concatenate(
        [jnp.sqrt(jnp.sum(jnp.where(row_map == c, q_sq, 0.0), axis=0, keepdims=True)
                  * jnp.max(jnp.where(lane == 2 * head + c, k_sq, 0.0), axis=1, keepdims=True))
         for c in range(2)], axis=1) * BOUND_SLACK

    def causal(j, s):
        key_i = j * tk + lax.broadcasted_iota(jnp.int32, s.shape, 0)
        qry_i = i * tq + (lax.broadcasted_iota(jnp.int32, s.shape, 1) & (tq - 1))
        return jnp.where(key_i <= qry_i, s, -jnp.inf)

    n_full = (i * tq) // tk

    def bounded():
        def tile(j, diagonal):
            kb = k_ref[pl.ds(pl.multiple_of(j * tk, tk), tk), :]
            s = jnp.dot(kb, q_cat, preferred_element_type=F32) - bound
            if diagonal:
                s = causal(j, s)
            return jnp.dot(vt_ref[0, j, 0], jnp.exp2(s).astype(BF16),
                           preferred_element_type=F32)

        def trip(n_tiles, first):
            def body(jj, carry):
                j = first + n_tiles * jj
                total = tile(j, False)
                for t in range(1, n_tiles):
                    total = total + tile(j + t, False)
                acc_ref[...] += total
                return carry
            return body

        n_even = n_full - n_full % 2

        def odd_start():
            acc_ref[...] = tile(n_even, False) + tile(n_full, True)

        def even_start():
            acc_ref[...] = tile(n_full, True)

        lax.cond(n_even < n_full, odd_start, even_start)
        done = 0
        for n_tiles in ATT_TRIP_TILES:
            n_trips = (n_even - done) // n_tiles
            lax.fori_loop(0, n_trips, trip(n_tiles, done), 0)
            done = done + n_trips * n_tiles

    def online():
        def scores(j, s_ref):
            kb = k_ref[pl.ds(pl.multiple_of(j * tk, tk), tk), :]
            s_ref[...] = jnp.dot(kb, q_cat, preferred_element_type=F32)

        def softmax_pv(j, s_ref, m_old, diagonal):
            s = s_ref[...]
            if diagonal:
                s = causal(j, s)
            m_new = jnp.maximum(m_old, jnp.max(s, axis=0, keepdims=True))
            alpha = jnp.exp2(m_old - m_new)
            p = jnp.exp2(s - m_new).astype(BF16)
            acc_ref[...] = alpha * acc_ref[...] + jnp.dot(vt_ref[0, j, 0], p,
                                                          preferred_element_type=F32)
            return m_new

        acc_ref[...] = jnp.zeros_like(acc_ref)
        scores(0, sa_ref)

        def pair(jj, m_run):
            j = 2 * jj
            scores(j + 1, sb_ref)
            m_run = softmax_pv(j, sa_ref, m_run, False)
            scores(j + 2, sa_ref)
            return softmax_pv(j + 1, sb_ref, m_run, False)

        m_run = lax.fori_loop(0, n_full // 2, pair, jnp.full((1, 2 * tq), -jnp.inf, F32))

        def tail_odd(m_old):
            scores(n_full, sb_ref)
            m_mid = softmax_pv(n_full - 1, sa_ref, m_old, False)
            return softmax_pv(n_full, sb_ref, m_mid, True)

        def tail_even(m_old):
            return softmax_pv(n_full, sa_ref, m_old, True)

        lax.cond(n_full % 2 == 1, tail_odd, tail_even, m_run)

    lax.cond(jnp.max(bound) <= BOUND_MAX, bounded, online)

    acc = acc_ref[...]
    inv_l = 1.0 / acc[A_V_DIM:A_V_DIM + 1]
    o = acc[:A_V_DIM, :tq] * inv_l[:, :tq] - lam_full * (acc[:A_V_DIM, tq:] * inv_l[:, tq:])
    ms = jnp.mean(o * o, axis=0, keepdims=True)
    o = o * lax.rsqrt(ms + NORM_EPS) * g_ref[...] * (1.0 - lam_init)
    o_ref[...] = o.astype(BF16)


def _attention(q, k, vt, k_norms, lam, g_col, batch, seq, lam_init):
    tq, tk = ATT_TQ, ATT_TK
    assert tk % tq == 0 and tq & (tq - 1) == 0
    nq = seq // tq
    nk = seq // tk
    t = batch * seq
    return pl.pallas_call(
        functools.partial(_attn_kernel, lam_init=lam_init),
        grid=(batch, A_HEADS, nq),
        in_specs=[pl.BlockSpec((4, A_HEAD_DIM), lambda b, h, i: (0, 0)),
                  pl.BlockSpec((nk, 1, LANES), lambda b, h, i: (b, 0, 0)),
                  pl.BlockSpec((LANES, tq), lambda b, h, i: (h, b * nq + i)),
                  pl.BlockSpec((seq, LANES), lambda b, h, i: (b, h)),
                  pl.BlockSpec((1, nk, 1, V_ROWS, tk), lambda b, h, i: (b, 0, h, 0, 0)),
                  pl.BlockSpec((A_V_DIM, 1), lambda b, h, i: (h, 0))],
        out_specs=pl.BlockSpec((A_V_DIM, tq), lambda b, h, i: (h, b * nq + i)),
        out_shape=jax.ShapeDtypeStruct((D_MODEL, t), BF16),
        scratch_shapes=[pltpu.VMEM((tk, 2 * tq), F32),
                        pltpu.VMEM((tk, 2 * tq), F32),
                        pltpu.VMEM((V_ROWS, 2 * tq), F32)],
        compiler_params=_params("parallel", "parallel", "arbitrary"),
        name="diff_attention",
    )(lam, k_norms, q, k, vt, g_col)


def _head_perm(w):
    half = A_HEAD_DIM // 2
    w = w.reshape(w.shape[0], A_HEADS, 2, 2, half)
    return w.transpose(0, 1, 3, 2, 4).reshape(w.shape[0], D_MODEL)


def kernel(x, positions, norm_g, a_w_in, a_b_gates, a_g_head, a_w_out, kv_norm_g, w_kv,
           b_w_q, b_lam, b_g_head, b_w_out, mlp_w_up, mlp_w_down):
    batch, seq, _ = x.shape
    t = batch * seq
    depth = norm_g.shape[0]
    n_a = a_w_in.shape[0]
    xf = x.reshape(t, D_MODEL)
    row = lambda v: v.reshape(1, -1).astype(F32)

    cos, sin, cos_t, sin_t = _rope_tables(positions.reshape(1, t))
    k_sh = vt_sh = k_norms = None
    for layer in range(depth):
        g = norm_g[layer]
        if layer < n_a:
            w_in = a_w_in[layer]
            vo_end = 2 * M_QK_COLS + 2 * D_MODEL
            w_voq = jnp.concatenate([w_in[:, 2 * M_QK_COLS:vo_end], w_in[:, :M_QK_COLS]],
                                    axis=1).astype(BF16)
            w_kt = (w_in[:, M_QK_COLS:2 * M_QK_COLS] * M_QK_DIM ** -0.5).T.astype(BF16)
            w_gt = w_in[:, vo_end:].T.astype(BF16)
            voq, kt, gates = _a_proj(xf, row(g[0]), w_voq, w_kt, w_gt,
                                     a_b_gates[layer].reshape(-1, 1).astype(F32))
            y = _mlstm(voq, kt, gates, row(a_g_head[layer]), batch, seq)
            w_out = a_w_out[layer]
        else:
            j = layer - n_a
            if j == 0:
                w_k = _head_perm(w_kv[:, :D_MODEL]).astype(BF16)
                w_vt = w_kv[:, D_MODEL:].T.astype(BF16)
                k_sh, vt_sh, k_norms = _kv_proj(xf, row(kv_norm_g), w_k, w_vt, cos, sin,
                                                batch, seq)
            lam_init = 0.8 - 0.6 * math.exp(-0.3 * layer)
            w_qt = (_head_perm(b_w_q[j]) * (A_HEAD_DIM ** -0.5 * math.log2(math.e))
                    ).T.astype(BF16)
            q_t = _q_proj(xf, row(g[0]), w_qt, cos_t, sin_t)
            y = _attention(q_t, k_sh, vt_sh, k_norms, b_lam[j].astype(F32),
                           b_g_head[j].reshape(-1, 1).astype(F32), batch, seq, lam_init)
            w_out = b_w_out[j]
        xf = _post(xf, y, w_out.astype(BF16), row(g[1]), row(g[2]), row(g[3]),
                   mlp_w_up[layer].astype(BF16), mlp_w_down[layer].astype(BF16),
                   y_feature_major=layer >= n_a)
    return xf.reshape(batch, seq, D_MODEL)
```

```python
import functools
import math

import jax
import jax.numpy as jnp
from jax import lax
from jax.experimental import pallas as pl
from jax.experimental.pallas import tpu as pltpu

F32 = jnp.float32
BF16 = jnp.bfloat16

D_MODEL = 1024
D_FF = 4 * D_MODEL
NORM_EPS = 1e-6

M_HEADS = 4
M_V_DIM = D_MODEL // M_HEADS
M_QK_DIM = M_V_DIM // 2
M_QK_COLS = M_HEADS * M_QK_DIM
GATE_SOFTCAP = 15.0
M_CHUNK = 512

A_HEADS = 8
A_HEAD_DIM = D_MODEL // (2 * A_HEADS)
A_V_DIM = 2 * A_HEAD_DIM
ROPE_THETA = 10000.0

LANES = 128
VMEM_LIMIT_BYTES = 56 * 1024 * 1024

ROW_TILE = 512
ATT_TQ = 512
ATT_TK = 512
ATT_TRIP_TILES = (8, 4, 2)
ATT_Q_TILES_PER_STEP = 8
BOUND_MAX = 40.0
BOUND_SLACK = 1.03
V_ROWS = A_V_DIM + 16

_NT = (((1,), (1,)), ((), ()))
_TN = (((0,), (0,)), ((), ()))


def _params(*sem):
    return pltpu.CompilerParams(dimension_semantics=sem, vmem_limit_bytes=VMEM_LIMIT_BYTES)


def _rms(x, g):
    ms = jnp.mean(x * x, axis=-1, keepdims=True)
    return x * lax.rsqrt(ms + NORM_EPS) * g


def _const_spec(shape):
    nd = len(shape)
    return pl.BlockSpec(shape, lambda *_: (0,) * nd, pipeline_mode=pl.Buffered(1))


def _rope_table_kernel(pos_ref, inv_ref, sign_ref, cos_ref, sin_ref, cost_ref, sint_ref):
    ang = inv_ref[...] * pos_ref[...].astype(F32)
    cos32 = jnp.cos(ang)
    sin32 = jnp.sin(ang)
    cost_ref[...] = jnp.concatenate([cos32, cos32], axis=0)
    sint_ref[...] = jnp.concatenate([sin32, sin32], axis=0)
    cos_ref[...] = jnp.concatenate([cos32] * 4, axis=0).T
    sin_ref[...] = jnp.concatenate([sin32] * 4, axis=0).T * sign_ref[...]


def _rope_tables(pos_row):
    t = pos_row.shape[1]
    half = A_HEAD_DIM // 2
    inv = 1.0 / (ROPE_THETA ** (jnp.arange(0, A_HEAD_DIM, 2, dtype=F32) / A_HEAD_DIM))
    sign = jnp.where(jnp.arange(LANES) < LANES // 2, -1.0, 1.0).astype(F32).reshape(1, LANES)
    tm = ROW_TILE
    return pl.pallas_call(
        _rope_table_kernel,
        grid=(t // tm,),
        in_specs=[pl.BlockSpec((1, tm), lambda i: (0, i)),
                  _const_spec((half, 1)), _const_spec((1, LANES))],
        out_specs=[pl.BlockSpec((tm, LANES), lambda i: (i, 0))] * 2
        + [pl.BlockSpec((A_HEAD_DIM, tm), lambda i: (0, i))] * 2,
        out_shape=[jax.ShapeDtypeStruct((t, LANES), F32)] * 2
        + [jax.ShapeDtypeStruct((A_HEAD_DIM, t), F32)] * 2,
        compiler_params=_params("parallel"),
        name="rope_tables",
    )(pos_row, inv.reshape(half, 1), sign)


def _rope(t, cos, sin):
    return t * cos + pltpu.roll(t, LANES // 2, axis=1) * sin


def _a_proj_kernel(x_ref, g_ref, w_ref, wkt_ref, wgt_ref, bg_ref, voq_ref, kt_ref, gates_ref):
    hb = _rms(x_ref[...], g_ref[...]).astype(BF16)
    n_out = voq_ref.shape[1]
    for n in range(0, n_out, 512):
        voq_ref[:, n:n + 512] = jnp.dot(
            hb, w_ref[:, n:n + 512], preferred_element_type=F32).astype(BF16)
    kt_ref[...] = lax.dot_general(wkt_ref[...], hb, _NT,
                                  preferred_element_type=F32).astype(BF16)
    gr = lax.dot_general(wgt_ref[...], hb, _NT, preferred_element_type=F32) + bg_ref[...]
    gr = GATE_SOFTCAP * jnp.tanh(gr / GATE_SOFTCAP)
    log_sig = jnp.minimum(gr, 0.0) - jnp.log1p(jnp.exp(-jnp.abs(gr)))
    row = lax.broadcasted_iota(jnp.int32, gr.shape, 0)
    gates_ref[...] = jnp.where(row < M_HEADS, gr, log_sig)


def _a_proj(x, g, w_voq, w_kt, w_gt, b_g):
    t = x.shape[0]
    tm = ROW_TILE
    n_out = w_voq.shape[1]
    return pl.pallas_call(
        _a_proj_kernel,
        grid=(t // tm,),
        in_specs=[pl.BlockSpec((tm, D_MODEL), lambda i: (i, 0)),
                  _const_spec((1, D_MODEL)),
                  _const_spec((D_MODEL, n_out)),
                  _const_spec((M_QK_COLS, D_MODEL)),
                  _const_spec((2 * M_HEADS, D_MODEL)),
                  _const_spec((2 * M_HEADS, 1))],
        out_specs=[pl.BlockSpec((tm, n_out), lambda i: (i, 0)),
                   pl.BlockSpec((M_QK_COLS, tm), lambda i: (0, i)),
                   pl.BlockSpec((2 * M_HEADS, tm), lambda i: (0, i))],
        out_shape=[jax.ShapeDtypeStruct((t, n_out), BF16),
                   jax.ShapeDtypeStruct((M_QK_COLS, t), BF16),
                   jax.ShapeDtypeStruct((2 * M_HEADS, t), F32)],
        compiler_params=_params("parallel"),
        name="mlstm_proj",
    )(x, g, w_voq, w_kt, w_gt, b_g)


def _mlstm_kernel(q_ref, kt_ref, v_ref, o_ref, g_ref, gh_ref, y_ref, c_ref, m_ref):
    L = q_ref.shape[0]

    @pl.when(pl.program_id(1) == 0)
    def _():
        c_ref[...] = jnp.zeros_like(c_ref)
        m_ref[...] = jnp.zeros_like(m_ref)

    jj = lax.broadcasted_iota(jnp.int32, (L, L), 0)
    ss = lax.broadcasted_iota(jnp.int32, (L, L), 1)
    causal = ss <= jj
    upper = (jj <= ss).astype(F32)

    gates = g_ref[...]
    cums = jnp.dot(gates, upper, preferred_element_type=F32, precision=lax.Precision.HIGHEST)
    b8 = pltpu.roll(cums, M_HEADS, axis=0)
    r8 = gates - b8
    lane = lax.broadcasted_iota(jnp.int32, r8.shape, 1)
    cm8 = r8
    shift = 1
    while shift < L:
        cm8 = jnp.maximum(cm8, jnp.where(lane >= shift, pltpu.roll(cm8, shift, axis=1), -jnp.inf))
        shift *= 2
    stack = jnp.concatenate([cm8, b8, jnp.zeros((LANES - 16, L), F32)], axis=0)
    cols = stack.T
    lane1 = lax.broadcasted_iota(jnp.int32, (1, LANES), 1)
    m_prev = [m_ref[h] for h in range(M_HEADS)]
    m_lanes = sum(jnp.where(lane1 == h, m_prev[h], 0.0) for h in range(M_HEADS))
    m_cols = jnp.maximum(cols, m_lanes)
    b_cols = pltpu.roll(cols, LANES - 8, axis=1)
    is_head = lane1 < M_HEADS
    s_inter = jnp.where(is_head, jnp.exp(m_lanes - m_cols), 0.0).astype(BF16)
    floor = jnp.where(is_head, jnp.exp(-(b_cols + m_cols)), 0.0).astype(BF16)
    spread = (lax.broadcasted_iota(jnp.int32, (LANES, M_HEADS * LANES), 0)
              == lax.broadcasted_iota(jnp.int32, (LANES, M_HEADS * LANES), 1) // LANES
              ).astype(BF16)
    s_inter = jnp.dot(s_inter, spread, preferred_element_type=F32)
    floor = jnp.dot(floor, spread, preferred_element_type=F32)
    ones_blk = jnp.ones((L, LANES), BF16)
    mean_blk = jnp.full((M_V_DIM, LANES), 1.0 / M_V_DIM, BF16)

    for h in range(M_HEADS):
        lanes_h = slice(h * LANES, (h + 1) * LANES)
        q = q_ref[:, lanes_h]
        kt = kt_ref[lanes_h, :]
        v_ext = jnp.concatenate([v_ref[:, h * M_V_DIM:(h + 1) * M_V_DIM], ones_blk], axis=1)
        c_prev = c_ref[h]
        r_row = r8[h:h + 1]

        w_intra = jnp.exp(jnp.where(causal, r_row, -jnp.inf) - m_cols[:, h:h + 1])
        s_qk = jnp.dot(q, kt, preferred_element_type=F32) * w_intra
        s_in = s_inter[:, lanes_h]
        ext = (jnp.concatenate([s_in, s_in, s_in], axis=1)
               * jnp.dot(q, c_prev.astype(BF16), preferred_element_type=F32)
               + jnp.dot(s_qk.astype(BF16), v_ext, preferred_element_type=F32))
        den = ext[:, M_V_DIM:]
        inv = 1.0 / jnp.maximum(jnp.abs(den), floor[:, lanes_h])
        h_out = ext[:, :M_V_DIM] * jnp.concatenate([inv, inv], axis=1)

        m_last = jnp.maximum(cm8[h:h + 1, L - 1:L], m_prev[h])
        decay = jnp.exp(m_prev[h] - m_last)
        kwt = (kt.astype(F32) * jnp.exp(r_row - m_last)).astype(BF16)
        c_ref[h] = decay * c_prev + jnp.dot(kwt, v_ext, preferred_element_type=F32)
        m_ref[h] = b8[h:h + 1, L - 1:L] + m_last

        sl = slice(h * M_V_DIM, (h + 1) * M_V_DIM)
        ms = jnp.dot((h_out * h_out).astype(BF16), mean_blk, preferred_element_type=F32)
        rs = lax.rsqrt(ms + NORM_EPS)
        og = jax.nn.sigmoid(o_ref[:, sl].astype(F32))
        y_ref[:, sl] = (og * h_out * jnp.concatenate([rs, rs], axis=1)
                        * gh_ref[:, sl]).astype(BF16)


def _mlstm(voq, kt, gates, g_head, batch, seq):
    L = M_CHUNK
    nc = seq // L
    t = batch * seq
    row = lambda b, c: b * nc + c
    return pl.pallas_call(
        _mlstm_kernel,
        grid=(batch, nc),
        in_specs=[pl.BlockSpec((L, M_QK_COLS), lambda b, c: (row(b, c), 2 * D_MODEL // M_QK_COLS)),
                  pl.BlockSpec((M_QK_COLS, L), lambda b, c: (0, row(b, c))),
                  pl.BlockSpec((L, D_MODEL), lambda b, c: (row(b, c), 0)),
                  pl.BlockSpec((L, D_MODEL), lambda b, c: (row(b, c), 1)),
                  pl.BlockSpec((2 * M_HEADS, L), lambda b, c: (0, row(b, c))),
                  pl.BlockSpec((1, D_MODEL), lambda b, c: (0, 0))],
        out_specs=pl.BlockSpec((L, D_MODEL), lambda b, c: (row(b, c), 0)),
        out_shape=jax.ShapeDtypeStruct((t, D_MODEL), BF16),
        scratch_shapes=[pltpu.VMEM((M_HEADS, M_QK_DIM, M_V_DIM + LANES), F32),
                        pltpu.VMEM((M_HEADS, 1, 1), F32)],
        compiler_params=_params("parallel", "arbitrary"),
        name="mlstm_mixer",
    )(voq, kt, voq, voq, gates, g_head)


def _post_kernel(x_ref, y_ref, wo_ref, g1_ref, g2_ref, g3_ref, wu_ref, wd_ref, out_ref, *,
                 y_feature_major):
    if y_feature_major:
        z = lax.dot_general(y_ref[0], wo_ref[...], _TN, preferred_element_type=F32)
    else:
        z = jnp.dot(y_ref[...], wo_ref[...], preferred_element_type=F32)
    x1 = x_ref[...] + _rms(z, g1_ref[...])
    hb = _rms(x1, g2_ref[...]).astype(BF16)
    acc = jnp.zeros(x1.shape, F32)
    for f in range(0, D_FF, 1024):
        u = jnp.maximum(jnp.dot(hb, wu_ref[:, f:f + 1024], preferred_element_type=F32), 0.0)
        acc = acc + jnp.dot((u * u).astype(BF16), wd_ref[f:f + 1024, :],
                            preferred_element_type=F32)
    out_ref[...] = x1 + _rms(acc, g3_ref[...])


def _post(x, y, w_out, g1, g2, g3, w_up, w_down, y_feature_major):
    t = x.shape[0]
    tm = ROW_TILE
    assert tm == ATT_TQ
    y_spec = (pl.BlockSpec((1, D_MODEL, tm), lambda i: (i, 0, 0)) if y_feature_major
              else pl.BlockSpec((tm, D_MODEL), lambda i: (i, 0)))
    return pl.pallas_call(
        functools.partial(_post_kernel, y_feature_major=y_feature_major),
        grid=(t // tm,),
        in_specs=[pl.BlockSpec((tm, D_MODEL), lambda i: (i, 0)),
                  y_spec,
                  _const_spec((D_MODEL, D_MODEL)),
                  _const_spec((1, D_MODEL)), _const_spec((1, D_MODEL)), _const_spec((1, D_MODEL)),
                  _const_spec((D_MODEL, D_FF)),
                  _const_spec((D_FF, D_MODEL))],
        out_specs=pl.BlockSpec((tm, D_MODEL), lambda i: (i, 0)),
        out_shape=jax.ShapeDtypeStruct((t, D_MODEL), F32),
        compiler_params=_params("parallel"),
        name="outproj_mlp",
    )(x, y, w_out, g1, g2, g3, w_up, w_down)


def _kv_proj_kernel(x_ref, g_ref, wk_ref, wvt_ref, cos_ref, sin_ref, k_ref, vt_ref, kn_ref):
    hb = _rms(x_ref[...], g_ref[...]).astype(BF16)
    cos = cos_ref[...]
    sin = sin_ref[...]
    kk = jnp.dot(hb, wk_ref[...], preferred_element_type=F32)
    col = lax.broadcasted_iota(jnp.int32, (D_MODEL, LANES), 0)
    out = lax.broadcasted_iota(jnp.int32, (D_MODEL, LANES), 1)
    pick = (out == 2 * (col >> 7) + ((col >> 5) & 1)).astype(BF16)
    k_sq = jnp.dot((kk * kk).astype(BF16), pick, preferred_element_type=F32)
    kn_ref[0] = jnp.max(k_sq, axis=0, keepdims=True)
    for h in range(A_HEADS):
        sl = slice(h * LANES, (h + 1) * LANES)
        k_ref[:, sl] = _rope(kk[:, sl], cos, sin).astype(BF16)
    vt = lax.dot_general(wvt_ref[...], hb, _NT, preferred_element_type=F32)
    ones = jnp.ones((V_ROWS - A_V_DIM, vt.shape[1]), BF16)
    for h in range(A_HEADS):
        vt_ref[0, 0, h, :A_V_DIM, :] = vt[h * A_V_DIM:(h + 1) * A_V_DIM].astype(BF16)
        vt_ref[0, 0, h, A_V_DIM:, :] = ones


def _kv_proj(x, g, w_k, w_vt, cos, sin, batch, seq):
    t = x.shape[0]
    tm = ATT_TK
    nk = seq // tm
    return pl.pallas_call(
        _kv_proj_kernel,
        grid=(t // tm,),
        in_specs=[pl.BlockSpec((tm, D_MODEL), lambda i: (i, 0)),
                  _const_spec((1, D_MODEL)),
                  _const_spec((D_MODEL, D_MODEL)),
                  _const_spec((D_MODEL, D_MODEL)),
                  pl.BlockSpec((tm, LANES), lambda i: (i, 0)),
                  pl.BlockSpec((tm, LANES), lambda i: (i, 0))],
        out_specs=[pl.BlockSpec((tm, D_MODEL), lambda i: (i, 0)),
                   pl.BlockSpec((1, 1, A_HEADS, V_ROWS, tm),
                                lambda i: (i // nk, i % nk, 0, 0, 0)),
                   pl.BlockSpec((1, 1, LANES), lambda i: (i, 0, 0))],
        out_shape=[jax.ShapeDtypeStruct((t, D_MODEL), BF16),
                   jax.ShapeDtypeStruct((batch, nk, A_HEADS, V_ROWS, tm), BF16),
                   jax.ShapeDtypeStruct((t // tm, 1, LANES), F32)],
        compiler_params=_params("parallel"),
        name="kv_proj",
    )(x, g, w_k, w_vt, cos, sin)


def _q_proj_kernel(x_ref, g_ref, wqt_ref, cos_ref, sin_ref, qt_ref):
    hb = _rms(x_ref[...], g_ref[...]).astype(BF16)
    cos = cos_ref[...]
    sin = sin_ref[...]
    qq = lax.dot_general(wqt_ref[...], hb, _NT, preferred_element_type=F32)
    half = LANES // 2
    for h in range(A_HEADS):
        lo = qq[h * LANES:h * LANES + half]
        hi = qq[h * LANES + half:(h + 1) * LANES]
        qt_ref[0, h * LANES:h * LANES + half, :] = (lo * cos - hi * sin).astype(BF16)
        qt_ref[0, h * LANES + half:(h + 1) * LANES, :] = (hi * cos + lo * sin).astype(BF16)


def _q_proj(x, g, w_qt, cos_t, sin_t):
    t = x.shape[0]
    tm = ATT_TQ
    return pl.pallas_call(
        _q_proj_kernel,
        grid=(t // tm,),
        in_specs=[pl.BlockSpec((tm, D_MODEL), lambda i: (i, 0)),
                  _const_spec((1, D_MODEL)),
                  _const_spec((D_MODEL, D_MODEL)),
                  pl.BlockSpec((A_HEAD_DIM, tm), lambda i: (0, i)),
                  pl.BlockSpec((A_HEAD_DIM, tm), lambda i: (0, i))],
        out_specs=pl.BlockSpec((1, D_MODEL, tm), lambda i: (i, 0, 0)),
        out_shape=jax.ShapeDtypeStruct((t // tm, D_MODEL, tm), BF16),
        compiler_params=_params("parallel"),
        name="q_proj",
    )(x, g, w_qt, cos_t, sin_t)


def _attn_kernel(lam_ref, kn_ref, q_ref, k_ref, vt_ref, g_ref, o_ref, sa_ref, sb_ref, acc_ref, *,
                 lam_init):
    n_sub = q_ref.shape[0]

    def q_tile(t, carry):
        o_ref[t] = _attn_tile(pl.program_id(2) * n_sub + t, q_ref[t], lam_ref, kn_ref, k_ref,
                              vt_ref, g_ref, sa_ref, sb_ref, acc_ref, lam_init)
        return carry

    lax.fori_loop(0, n_sub, q_tile, 0)


def _attn_tile(i, q_bf, lam_ref, kn_ref, k_ref, vt_ref, g_ref, sa_ref, sb_ref, acc_ref, lam_init):
    tq = q_bf.shape[1]
    tk = vt_ref.shape[4]
    head = pl.program_id(1)

    lam = lam_ref[...]
    lam_full = (jnp.exp(jnp.sum(lam[0:1] * lam[1:2], keepdims=True))
                - jnp.exp(jnp.sum(lam[2:3] * lam[3:4], keepdims=True)) + lam_init)

    q_t = q_bf.astype(F32)
    row_map = (lax.broadcasted_iota(jnp.int32, q_t.shape, 0) >> 5) & 1
    q_cat = jnp.concatenate([jnp.where(row_map == c, q_t, 0.0) for c in range(2)],
                            axis=1).astype(BF16)

    k_sq = jnp.max(kn_ref[...], axis=0)
    lane = lax.broadcasted_iota(jnp.int32, k_sq.shape, 1)
    q_sq = q_t * q_t
    bound = jnp.concatenate(
        [jnp.sqrt(jnp.sum(jnp.where(row_map == c, q_sq, 0.0), axis=0, keepdims=True)
                  * jnp.max(jnp.where(lane == 2 * head + c, k_sq, 0.0), axis=1, keepdims=True))
         for c in range(2)], axis=1) * BOUND_SLACK

    def causal(j, s):
        key_i = j * tk + lax.broadcasted_iota(jnp.int32, s.shape, 0)
        qry_i = i * tq + (lax.broadcasted_iota(jnp.int32, s.shape, 1) & (tq - 1))
        return jnp.where(key_i <= qry_i, s, -jnp.inf)

    n_full = (i * tq) // tk

    def bounded():
        def tile(j, diagonal):
            kb = k_ref[pl.ds(pl.multiple_of(j * tk, tk), tk), :]
            s = jnp.dot(kb, q_cat, preferred_element_type=F32) - bound
            if diagonal:
                s = causal(j, s)
            return jnp.dot(vt_ref[0, j, 0], jnp.exp2(s).astype(BF16),
                           preferred_element_type=F32)

        def trip(n_tiles, first):
            def body(jj, carry):
                j = first + n_tiles * jj
                total = tile(j, False)
                for t in range(1, n_tiles):
                    total = total + tile(j + t, False)
                acc_ref[...] += total
                return carry
            return body

        n_even = n_full - n_full % 2

        def odd_start():
            acc_ref[...] = tile(n_even, False) + tile(n_full, True)

        def even_start():
            acc_ref[...] = tile(n_full, True)

        lax.cond(n_even < n_full, odd_start, even_start)
        done = 0
        for n_tiles in ATT_TRIP_TILES:
            n_trips = (n_even - done) // n_tiles
            lax.fori_loop(0, n_trips, trip(n_tiles, done), 0)
            done = done + n_trips * n_tiles

    def online():
        def scores(j, s_ref):
            kb = k_ref[pl.ds(pl.multiple_of(j * tk, tk), tk), :]
            s_ref[...] = jnp.dot(kb, q_cat, preferred_element_type=F32)

        def softmax_pv(j, s_ref, m_old, diagonal):
            s = s_ref[...]
            if diagonal:
                s = causal(j, s)
            m_new = jnp.maximum(m_old, jnp.max(s, axis=0, keepdims=True))
            alpha = jnp.exp2(m_old - m_new)
            p = jnp.exp2(s - m_new).astype(BF16)
            acc_ref[...] = alpha * acc_ref[...] + jnp.dot(vt_ref[0, j, 0], p,
                                                          preferred_element_type=F32)
            return m_new

        acc_ref[...] = jnp.zeros_like(acc_ref)
        scores(0, sa_ref)

        def pair(jj, m_run):
            j = 2 * jj
            scores(j + 1, sb_ref)
            m_run = softmax_pv(j, sa_ref, m_run, False)
            scores(j + 2, sa_ref)
            return softmax_pv(j + 1, sb_ref, m_run, False)

        m_run = lax.fori_loop(0, n_full // 2, pair, jnp.full((1, 2 * tq), -jnp.inf, F32))

        def tail_odd(m_old):
            scores(n_full, sb_ref)
            m_mid = softmax_pv(n_full - 1, sa_ref, m_old, False)
            return softmax_pv(n_full, sb_ref, m_mid, True)

        def tail_even(m_old):
            return softmax_pv(n_full, sa_ref, m_old, True)

        lax.cond(n_full % 2 == 1, tail_odd, tail_even, m_run)

    lax.cond(jnp.max(bound) <= BOUND_MAX, bounded, online)

    acc = acc_ref[...]
    inv_l = 1.0 / acc[A_V_DIM:A_V_DIM + 1]
    o = acc[:A_V_DIM, :tq] * inv_l[:, :tq] - lam_full * (acc[:A_V_DIM, tq:] * inv_l[:, tq:])
    ms = jnp.mean(o * o, axis=0, keepdims=True)
    o = o * lax.rsqrt(ms + NORM_EPS) * g_ref[...] * (1.0 - lam_init)
    return o.astype(BF16)


def _attention(q, k, vt, k_norms, lam, g_col, batch, seq, lam_init):
    tq, tk = ATT_TQ, ATT_TK
    assert tk % tq == 0 and tq & (tq - 1) == 0
    nq = seq // tq
    n_sub = math.gcd(ATT_Q_TILES_PER_STEP, nq)
    nk = seq // tk
    t = batch * seq
    steps = nq // n_sub
    q_spec = pl.BlockSpec((n_sub, LANES, tq), lambda b, h, i: (b * steps + i, h, 0))
    return pl.pallas_call(
        functools.partial(_attn_kernel, lam_init=lam_init),
        grid=(batch, A_HEADS, steps),
        in_specs=[pl.BlockSpec((4, A_HEAD_DIM), lambda b, h, i: (0, 0)),
                  pl.BlockSpec((nk, 1, LANES), lambda b, h, i: (b, 0, 0)),
                  q_spec,
                  pl.BlockSpec((seq, LANES), lambda b, h, i: (b, h)),
                  pl.BlockSpec((1, nk, 1, V_ROWS, tk), lambda b, h, i: (b, 0, h, 0, 0)),
                  pl.BlockSpec((A_V_DIM, 1), lambda b, h, i: (h, 0))],
        out_specs=q_spec,
        out_shape=jax.ShapeDtypeStruct((t // tq, D_MODEL, tq), BF16),
        scratch_shapes=[pltpu.VMEM((tk, 2 * tq), F32),
                        pltpu.VMEM((tk, 2 * tq), F32),
                        pltpu.VMEM((V_ROWS, 2 * tq), F32)],
        compiler_params=_params("parallel", "parallel", "arbitrary"),
        name="diff_attention",
    )(lam, k_norms, q, k, vt, g_col)


def _head_perm(w):
    half = A_HEAD_DIM // 2
    w = w.reshape(w.shape[0], A_HEADS, 2, 2, half)
    return w.transpose(0, 1, 3, 2, 4).reshape(w.shape[0], D_MODEL)


def kernel(x, positions, norm_g, a_w_in, a_b_gates, a_g_head, a_w_out, kv_norm_g, w_kv,
           b_w_q, b_lam, b_g_head, b_w_out, mlp_w_up, mlp_w_down):
    batch, seq, _ = x.shape
    t = batch * seq
    depth = norm_g.shape[0]
    n_a = a_w_in.shape[0]
    xf = x.reshape(t, D_MODEL)
    row = lambda v: v.reshape(1, -1).astype(F32)

    cos, sin, cos_t, sin_t = _rope_tables(positions.reshape(1, t))
    k_sh = vt_sh = k_norms = None
    for layer in range(depth):
        g = norm_g[layer]
        if layer < n_a:
            w_in = a_w_in[layer]
            vo_end = 2 * M_QK_COLS + 2 * D_MODEL
            w_voq = jnp.concatenate([w_in[:, 2 * M_QK_COLS:vo_end], w_in[:, :M_QK_COLS]],
                                    axis=1).astype(BF16)
            w_kt = (w_in[:, M_QK_COLS:2 * M_QK_COLS] * M_QK_DIM ** -0.5).T.astype(BF16)
            w_gt = w_in[:, vo_end:].T.astype(BF16)
            voq, kt, gates = _a_proj(xf, row(g[0]), w_voq, w_kt, w_gt,
                                     a_b_gates[layer].reshape(-1, 1).astype(F32))
            y = _mlstm(voq, kt, gates, row(a_g_head[layer]), batch, seq)
            w_out = a_w_out[layer]
        else:
            j = layer - n_a
            if j == 0:
                w_k = _head_perm(w_kv[:, :D_MODEL]).astype(BF16)
                w_vt = w_kv[:, D_MODEL:].T.astype(BF16)
                k_sh, vt_sh, k_norms = _kv_proj(xf, row(kv_norm_g), w_k, w_vt, cos, sin,
                                                batch, seq)
            lam_init = 0.8 - 0.6 * math.exp(-0.3 * layer)
            w_qt = (_head_perm(b_w_q[j]) * (A_HEAD_DIM ** -0.5 * math.log2(math.e))
                    ).T.astype(BF16)
            q_t = _q_proj(xf, row(g[0]), w_qt, cos_t, sin_t)
            y = _attention(q_t, k_sh, vt_sh, k_norms, b_lam[j].astype(F32),
                           b_g_head[j].reshape(-1, 1).astype(F32), batch, seq, lam_init)
            w_out = b_w_out[j]
        xf = _post(xf, y, w_out.astype(BF16), row(g[1]), row(g[2]), row(g[3]),
                   mlp_w_up[layer].astype(BF16), mlp_w_down[layer].astype(BF16),
                   y_feature_major=layer >= n_a)
    return xf.reshape(batch, seq, D_MODEL)
```

```python
import functools
import math

import jax
import jax.numpy as jnp
from jax import lax
from jax.experimental import pallas as pl
from jax.experimental.pallas import tpu as pltpu

F32 = jnp.float32
BF16 = jnp.bfloat16

D_MODEL = 1024
D_FF = 4 * D_MODEL
NORM_EPS = 1e-6

M_HEADS = 4
M_V_DIM = D_MODEL // M_HEADS
M_QK_DIM = M_V_DIM // 2
M_QK_COLS = M_HEADS * M_QK_DIM
GATE_SOFTCAP = 15.0
M_CHUNK = 512

A_HEADS = 8
A_HEAD_DIM = D_MODEL // (2 * A_HEADS)
A_V_DIM = 2 * A_HEAD_DIM
ROPE_THETA = 10000.0

LANES = 128
VMEM_LIMIT_BYTES = 56 * 1024 * 1024

ROW_TILE = 512
ATT_TQ = 512
ATT_TK = 512
ATT_TRIP_TILES = (4, 2)
BOUND_MAX = 40.0
BOUND_SLACK = 1.03
V_ROWS = A_V_DIM + 16

_NN = (((1,), (0,)), ((), ()))
_NT = (((1,), (1,)), ((), ()))
_TN = (((0,), (0,)), ((), ()))


def _params(*sem):
    return pltpu.CompilerParams(dimension_semantics=sem, vmem_limit_bytes=VMEM_LIMIT_BYTES)


def _rms(x, g):
    ms = jnp.mean(x * x, axis=-1, keepdims=True)
    return x * lax.rsqrt(ms + NORM_EPS) * g


def _const_spec(shape):
    nd = len(shape)
    return pl.BlockSpec(shape, lambda *_: (0,) * nd, pipeline_mode=pl.Buffered(1))


def _rope_table_kernel(pos_ref, inv_ref, sign_ref, cos_ref, sin_ref, cost_ref, sint_ref):
    ang = inv_ref[...] * pos_ref[...].astype(F32)
    cos32 = jnp.cos(ang)
    sin32 = jnp.sin(ang)
    cost_ref[...] = jnp.concatenate([cos32, cos32], axis=0)
    sint_ref[...] = jnp.concatenate([sin32, sin32], axis=0)
    cos_ref[...] = jnp.concatenate([cos32] * 4, axis=0).T
    sin_ref[...] = jnp.concatenate([sin32] * 4, axis=0).T * sign_ref[...]


def _rope_tables(pos_row):
    t = pos_row.shape[1]
    half = A_HEAD_DIM // 2
    inv = 1.0 / (ROPE_THETA ** (jnp.arange(0, A_HEAD_DIM, 2, dtype=F32) / A_HEAD_DIM))
    sign = jnp.where(jnp.arange(LANES) < LANES // 2, -1.0, 1.0).astype(F32).reshape(1, LANES)
    tm = ROW_TILE
    return pl.pallas_call(
        _rope_table_kernel,
        grid=(t // tm,),
        in_specs=[pl.BlockSpec((1, tm), lambda i: (0, i)),
                  _const_spec((half, 1)), _const_spec((1, LANES))],
        out_specs=[pl.BlockSpec((tm, LANES), lambda i: (i, 0))] * 2
        + [pl.BlockSpec((A_HEAD_DIM, tm), lambda i: (0, i))] * 2,
        out_shape=[jax.ShapeDtypeStruct((t, LANES), F32)] * 2
        + [jax.ShapeDtypeStruct((A_HEAD_DIM, t), F32)] * 2,
        compiler_params=_params("parallel"),
        name="rope_tables",
    )(pos_row, inv.reshape(half, 1), sign)


def _rope(t, cos, sin):
    return t * cos + pltpu.roll(t, LANES // 2, axis=1) * sin


def _a_proj_kernel(x_ref, g_ref, w_ref, wkt_ref, wgt_ref, bg_ref, voq_ref, kt_ref, gates_ref):
    hb = _rms(x_ref[...], g_ref[...]).astype(BF16)
    n_out = voq_ref.shape[1]
    for n in range(0, n_out, 512):
        voq_ref[:, n:n + 512] = jnp.dot(
            hb, w_ref[:, n:n + 512], preferred_element_type=F32).astype(BF16)
    kt_ref[...] = lax.dot_general(wkt_ref[...], hb, _NT,
                                  preferred_element_type=F32).astype(BF16)
    gr = lax.dot_general(wgt_ref[...], hb, _NT, preferred_element_type=F32) + bg_ref[...]
    gr = GATE_SOFTCAP * jnp.tanh(gr / GATE_SOFTCAP)
    log_sig = jnp.minimum(gr, 0.0) - jnp.log1p(jnp.exp(-jnp.abs(gr)))
    row = lax.broadcasted_iota(jnp.int32, gr.shape, 0)
    gates_ref[...] = jnp.where(row < M_HEADS, gr, log_sig)


def _a_proj(x, g, w_voq, w_kt, w_gt, b_g):
    t = x.shape[0]
    tm = ROW_TILE
    n_out = w_voq.shape[1]
    return pl.pallas_call(
        _a_proj_kernel,
        grid=(t // tm,),
        in_specs=[pl.BlockSpec((tm, D_MODEL), lambda i: (i, 0)),
                  _const_spec((1, D_MODEL)),
                  _const_spec((D_MODEL, n_out)),
                  _const_spec((M_QK_COLS, D_MODEL)),
                  _const_spec((2 * M_HEADS, D_MODEL)),
                  _const_spec((2 * M_HEADS, 1))],
        out_specs=[pl.BlockSpec((tm, n_out), lambda i: (i, 0)),
                   pl.BlockSpec((M_QK_COLS, tm), lambda i: (0, i)),
                   pl.BlockSpec((2 * M_HEADS, tm), lambda i: (0, i))],
        out_shape=[jax.ShapeDtypeStruct((t, n_out), BF16),
                   jax.ShapeDtypeStruct((M_QK_COLS, t), BF16),
                   jax.ShapeDtypeStruct((2 * M_HEADS, t), F32)],
        compiler_params=_params("parallel"),
        name="mlstm_proj",
    )(x, g, w_voq, w_kt, w_gt, b_g)


def _mlstm_kernel(q_ref, kt_ref, v_ref, o_ref, g_ref, gh_ref, y_ref, c_ref, m_ref):
    L = q_ref.shape[0]

    @pl.when(pl.program_id(1) == 0)
    def _():
        c_ref[...] = jnp.zeros_like(c_ref)
        m_ref[...] = jnp.zeros_like(m_ref)

    jj = lax.broadcasted_iota(jnp.int32, (L, L), 0)
    ss = lax.broadcasted_iota(jnp.int32, (L, L), 1)
    causal = ss <= jj
    upper = (jj <= ss).astype(F32)

    gates = g_ref[...]
    cums = jnp.dot(gates, upper, preferred_element_type=F32, precision=lax.Precision.HIGHEST)
    b8 = pltpu.roll(cums, M_HEADS, axis=0)
    r8 = gates - b8
    lane = lax.broadcasted_iota(jnp.int32, r8.shape, 1)
    cm8 = r8
    shift = 1
    while shift < L:
        cm8 = jnp.maximum(cm8, jnp.where(lane >= shift, pltpu.roll(cm8, shift, axis=1), -jnp.inf))
        shift *= 2
    stack = jnp.concatenate([cm8, b8, jnp.zeros((LANES - 16, L), F32)], axis=0)
    cols = stack.T
    lane1 = lax.broadcasted_iota(jnp.int32, (1, LANES), 1)
    m_prev = [m_ref[h] for h in range(M_HEADS)]
    m_lanes = sum(jnp.where(lane1 == h, m_prev[h], 0.0) for h in range(M_HEADS))
    m_cols = jnp.maximum(cols, m_lanes)
    b_cols = pltpu.roll(cols, LANES - 8, axis=1)
    is_head = lane1 < M_HEADS
    s_inter = jnp.where(is_head, jnp.exp(m_lanes - m_cols), 0.0).astype(BF16)
    floor = jnp.where(is_head, jnp.exp(-(b_cols + m_cols)), 0.0).astype(BF16)
    spread = (lax.broadcasted_iota(jnp.int32, (LANES, M_HEADS * LANES), 0)
              == lax.broadcasted_iota(jnp.int32, (LANES, M_HEADS * LANES), 1) // LANES
              ).astype(BF16)
    s_inter = jnp.dot(s_inter, spread, preferred_element_type=F32)
    floor = jnp.dot(floor, spread, preferred_element_type=F32)
    ones_blk = jnp.ones((L, LANES), BF16)
    mean_blk = jnp.full((M_V_DIM, LANES), 1.0 / M_V_DIM, BF16)

    for h in range(M_HEADS):
        lanes_h = slice(h * LANES, (h + 1) * LANES)
        q = q_ref[:, lanes_h]
        kt = kt_ref[lanes_h, :]
        v_ext = jnp.concatenate([v_ref[:, h * M_V_DIM:(h + 1) * M_V_DIM], ones_blk], axis=1)
        c_prev = c_ref[h]
        r_row = r8[h:h + 1]

        w_intra = jnp.exp(jnp.where(causal, r_row, -jnp.inf) - m_cols[:, h:h + 1])
        s_qk = jnp.dot(q, kt, preferred_element_type=F32) * w_intra
        s_in = s_inter[:, lanes_h]
        ext = (jnp.concatenate([s_in, s_in, s_in], axis=1)
               * jnp.dot(q, c_prev.astype(BF16), preferred_element_type=F32)
               + jnp.dot(s_qk.astype(BF16), v_ext, preferred_element_type=F32))
        den = ext[:, M_V_DIM:]
        inv = 1.0 / jnp.maximum(jnp.abs(den), floor[:, lanes_h])
        h_out = ext[:, :M_V_DIM] * jnp.concatenate([inv, inv], axis=1)

        m_last = jnp.maximum(cm8[h:h + 1, L - 1:L], m_prev[h])
        decay = jnp.exp(m_prev[h] - m_last)
        kwt = (kt.astype(F32) * jnp.exp(r_row - m_last)).astype(BF16)
        c_ref[h] = decay * c_prev + jnp.dot(kwt, v_ext, preferred_element_type=F32)
        m_ref[h] = b8[h:h + 1, L - 1:L] + m_last

        sl = slice(h * M_V_DIM, (h + 1) * M_V_DIM)
        ms = jnp.dot((h_out * h_out).astype(BF16), mean_blk, preferred_element_type=F32)
        rs = lax.rsqrt(ms + NORM_EPS)
        og = jax.nn.sigmoid(o_ref[:, sl].astype(F32))
        y_ref[:, sl] = (og * h_out * jnp.concatenate([rs, rs], axis=1)
                        * gh_ref[:, sl]).astype(BF16)


def _mlstm(voq, kt, gates, g_head, batch, seq):
    L = M_CHUNK
    nc = seq // L
    t = batch * seq
    row = lambda b, c: b * nc + c
    return pl.pallas_call(
        _mlstm_kernel,
        grid=(batch, nc),
        in_specs=[pl.BlockSpec((L, M_QK_COLS), lambda b, c: (row(b, c), 2 * D_MODEL // M_QK_COLS)),
                  pl.BlockSpec((M_QK_COLS, L), lambda b, c: (0, row(b, c))),
                  pl.BlockSpec((L, D_MODEL), lambda b, c: (row(b, c), 0)),
                  pl.BlockSpec((L, D_MODEL), lambda b, c: (row(b, c), 1)),
                  pl.BlockSpec((2 * M_HEADS, L), lambda b, c: (0, row(b, c))),
                  pl.BlockSpec((1, D_MODEL), lambda b, c: (0, 0))],
        out_specs=pl.BlockSpec((L, D_MODEL), lambda b, c: (row(b, c), 0)),
        out_shape=jax.ShapeDtypeStruct((t, D_MODEL), BF16),
        scratch_shapes=[pltpu.VMEM((M_HEADS, M_QK_DIM, M_V_DIM + LANES), F32),
                        pltpu.VMEM((M_HEADS, 1, 1), F32)],
        compiler_params=_params("parallel", "arbitrary"),
        name="mlstm_mixer",
    )(voq, kt, voq, voq, gates, g_head)


def _post_kernel(x_ref, y_ref, wo_ref, g1_ref, g2_ref, g3_ref, wu_ref, wd_ref, out_ref, *,
                 y_feature_major):
    z = lax.dot_general(y_ref[...], wo_ref[...], _TN if y_feature_major else _NN,
                        preferred_element_type=F32)
    x1 = x_ref[...] + _rms(z, g1_ref[...])
    hb = _rms(x1, g2_ref[...]).astype(BF16)
    acc = jnp.zeros(x1.shape, F32)
    for f in range(0, D_FF, 1024):
        u = jnp.maximum(jnp.dot(hb, wu_ref[:, f:f + 1024], preferred_element_type=F32), 0.0)
        acc = acc + jnp.dot((u * u).astype(BF16), wd_ref[f:f + 1024, :],
                            preferred_element_type=F32)
    out_ref[...] = x1 + _rms(acc, g3_ref[...])


def _post(x, y, w_out, g1, g2, g3, w_up, w_down, y_feature_major):
    t = x.shape[0]
    tm = ROW_TILE
    y_spec = (pl.BlockSpec((D_MODEL, tm), lambda i: (0, i)) if y_feature_major
              else pl.BlockSpec((tm, D_MODEL), lambda i: (i, 0)))
    return pl.pallas_call(
        functools.partial(_post_kernel, y_feature_major=y_feature_major),
        grid=(t // tm,),
        in_specs=[pl.BlockSpec((tm, D_MODEL), lambda i: (i, 0)),
                  y_spec,
                  _const_spec((D_MODEL, D_MODEL)),
                  _const_spec((1, D_MODEL)), _const_spec((1, D_MODEL)), _const_spec((1, D_MODEL)),
                  _const_spec((D_MODEL, D_FF)),
                  _const_spec((D_FF, D_MODEL))],
        out_specs=pl.BlockSpec((tm, D_MODEL), lambda i: (i, 0)),
        out_shape=jax.ShapeDtypeStruct((t, D_MODEL), F32),
        compiler_params=_params("parallel"),
        name="outproj_mlp",
    )(x, y, w_out, g1, g2, g3, w_up, w_down)


def _kv_proj_kernel(x_ref, g_ref, wk_ref, wvt_ref, cos_ref, sin_ref, k_ref, vt_ref, kn_ref):
    hb = _rms(x_ref[...], g_ref[...]).astype(BF16)
    cos = cos_ref[...]
    sin = sin_ref[...]
    kk = jnp.dot(hb, wk_ref[...], preferred_element_type=F32)
    col = lax.broadcasted_iota(jnp.int32, (D_MODEL, LANES), 0)
    out = lax.broadcasted_iota(jnp.int32, (D_MODEL, LANES), 1)
    pick = (out == 2 * (col >> 7) + ((col >> 5) & 1)).astype(BF16)
    k_sq = jnp.dot((kk * kk).astype(BF16), pick, preferred_element_type=F32)
    kn_ref[0] = jnp.max(k_sq, axis=0, keepdims=True)
    for h in range(A_HEADS):
        sl = slice(h * LANES, (h + 1) * LANES)
        k_ref[:, sl] = _rope(kk[:, sl], cos, sin).astype(BF16)
    vt = lax.dot_general(wvt_ref[...], hb, _NT, preferred_element_type=F32)
    ones = jnp.ones((V_ROWS - A_V_DIM, vt.shape[1]), BF16)
    for h in range(A_HEADS):
        vt_ref[0, 0, h, :A_V_DIM, :] = vt[h * A_V_DIM:(h + 1) * A_V_DIM].astype(BF16)
        vt_ref[0, 0, h, A_V_DIM:, :] = ones


def _kv_proj(x, g, w_k, w_vt, cos, sin, batch, seq):
    t = x.shape[0]
    tm = ATT_TK
    nk = seq // tm
    return pl.pallas_call(
        _kv_proj_kernel,
        grid=(t // tm,),
        in_specs=[pl.BlockSpec((tm, D_MODEL), lambda i: (i, 0)),
                  _const_spec((1, D_MODEL)),
                  _const_spec((D_MODEL, D_MODEL)),
                  _const_spec((D_MODEL, D_MODEL)),
                  pl.BlockSpec((tm, LANES), lambda i: (i, 0)),
                  pl.BlockSpec((tm, LANES), lambda i: (i, 0))],
        out_specs=[pl.BlockSpec((tm, D_MODEL), lambda i: (i, 0)),
                   pl.BlockSpec((1, 1, A_HEADS, V_ROWS, tm),
                                lambda i: (i // nk, i % nk, 0, 0, 0)),
                   pl.BlockSpec((1, 1, LANES), lambda i: (i, 0, 0))],
        out_shape=[jax.ShapeDtypeStruct((t, D_MODEL), BF16),
                   jax.ShapeDtypeStruct((batch, nk, A_HEADS, V_ROWS, tm), BF16),
                   jax.ShapeDtypeStruct((t // tm, 1, LANES), F32)],
        compiler_params=_params("parallel"),
        name="kv_proj",
    )(x, g, w_k, w_vt, cos, sin)


def _q_proj_kernel(x_ref, g_ref, wqt_ref, cos_ref, sin_ref, qt_ref):
    hb = _rms(x_ref[...], g_ref[...]).astype(BF16)
    cos = cos_ref[...]
    sin = sin_ref[...]
    qq = lax.dot_general(wqt_ref[...], hb, _NT, preferred_element_type=F32)
    half = LANES // 2
    for h in range(A_HEADS):
        lo = qq[h * LANES:h * LANES + half]
        hi = qq[h * LANES + half:(h + 1) * LANES]
        qt_ref[h * LANES:h * LANES + half, :] = (lo * cos - hi * sin).astype(BF16)
        qt_ref[h * LANES + half:(h + 1) * LANES, :] = (hi * cos + lo * sin).astype(BF16)


def _q_proj(x, g, w_qt, cos_t, sin_t):
    t = x.shape[0]
    tm = ROW_TILE
    return pl.pallas_call(
        _q_proj_kernel,
        grid=(t // tm,),
        in_specs=[pl.BlockSpec((tm, D_MODEL), lambda i: (i, 0)),
                  _const_spec((1, D_MODEL)),
                  _const_spec((D_MODEL, D_MODEL)),
                  pl.BlockSpec((A_HEAD_DIM, tm), lambda i: (0, i)),
                  pl.BlockSpec((A_HEAD_DIM, tm), lambda i: (0, i))],
        out_specs=pl.BlockSpec((D_MODEL, tm), lambda i: (0, i)),
        out_shape=jax.ShapeDtypeStruct((D_MODEL, t), BF16),
        compiler_params=_params("parallel"),
        name="q_proj",
    )(x, g, w_qt, cos_t, sin_t)


def _attn_kernel(lam_ref, kn_ref, q_ref, k_ref, vt_ref, g_ref, o_ref, sa_ref, sb_ref, acc_ref, *,
                 lam_init):
    tq = q_ref.shape[1]
    tk = vt_ref.shape[4]
    head = pl.program_id(1)
    i = pl.program_id(2)

    lam = lam_ref[...]
    lam_full = (jnp.exp(jnp.sum(lam[0:1] * lam[1:2], keepdims=True))
                - jnp.exp(jnp.sum(lam[2:3] * lam[3:4], keepdims=True)) + lam_init)

    q_t = q_ref[...].astype(F32)
    row_map = (lax.broadcasted_iota(jnp.int32, q_t.shape, 0) >> 5) & 1
    q_cat = jnp.concatenate([jnp.where(row_map == c, q_t, 0.0) for c in range(2)],
                            axis=1).astype(BF16)

    k_sq = jnp.max(kn_ref[...], axis=0)
    lane = lax.broadcasted_iota(jnp.int32, k_sq.shape, 1)
    q_sq = q_t * q_t
    bound = jnp.concatenate(
        [jnp.sqrt(jnp.sum(jnp.where(row_map == c, q_sq, 0.0), axis=0, keepdims=True)
                  * jnp.max(jnp.where(lane == 2 * head + c, k_sq, 0.0), axis=1, keepdims=True))
         for c in range(2)], axis=1) * BOUND_SLACK

    def causal(j, s):
        key_i = j * tk + lax.broadcasted_iota(jnp.int32, s.shape, 0)
        qry_i = i * tq + (lax.broadcasted_iota(jnp.int32, s.shape, 1) & (tq - 1))
        return jnp.where(key_i <= qry_i, s, -jnp.inf)

    n_full = (i * tq) // tk

    def bounded():
        def tile(j, diagonal):
            kb = k_ref[pl.ds(pl.multiple_of(j * tk, tk), tk), :]
            s = jnp.dot(kb, q_cat, preferred_element_type=F32) - bound
            if diagonal:
                s = causal(j, s)
            return jnp.dot(vt_ref[0, j, 0], jnp.exp2(s).astype(BF16),
                           preferred_element_type=F32)

        def trip(n_tiles, first):
            def body(jj, carry):
                j = first + n_tiles * jj
                total = tile(j, False)
                for t in range(1, n_tiles):
                    total = total + tile(j + t, False)
                acc_ref[...] += total
                return carry
            return body

        n_even = n_full - n_full % 2

        def odd_start():
            acc_ref[...] = tile(n_even, False) + tile(n_full, True)

        def even_start():
            acc_ref[...] = tile(n_full, True)

        lax.cond(n_even < n_full, odd_start, even_start)
        done = 0
        for n_tiles in ATT_TRIP_TILES:
            n_trips = (n_even - done) // n_tiles
            lax.fori_loop(0, n_trips, trip(n_tiles, done), 0)
            done = done + n_trips * n_tiles

    def online():
        def scores(j, s_ref):
            kb = k_ref[pl.ds(pl.multiple_of(j * tk, tk), tk), :]
            s_ref[...] = jnp.dot(kb, q_cat, preferred_element_type=F32)

        def softmax_pv(j, s_ref, m_old, diagonal):
            s = s_ref[...]
            if diagonal:
                s = causal(j, s)
            m_new = jnp.maximum(m_old, jnp.max(s, axis=0, keepdims=True))
            alpha = jnp.exp2(m_old - m_new)
            p = jnp.exp2(s - m_new).astype(BF16)
            acc_ref[...] = alpha * acc_ref[...] + jnp.dot(vt_ref[0, j, 0], p,
                                                          preferred_element_type=F32)
            return m_new

        acc_ref[...] = jnp.zeros_like(acc_ref)
        scores(0, sa_ref)

        def pair(jj, m_run):
            j = 2 * jj
            scores(j + 1, sb_ref)
            m_run = softmax_pv(j, sa_ref, m_run, False)
            scores(j + 2, sa_ref)
            return softmax_pv(j + 1, sb_ref, m_run, False)

        m_run = lax.fori_loop(0, n_full // 2, pair, jnp.full((1, 2 * tq), -jnp.inf, F32))

        def tail_odd(m_old):
            scores(n_full, sb_ref)
            m_mid = softmax_pv(n_full - 1, sa_ref, m_old, False)
            return softmax_pv(n_full, sb_ref, m_mid, True)

        def tail_even(m_old):
            return softmax_pv(n_full, sa_ref, m_old, True)

        lax.cond(n_full % 2 == 1, tail_odd, tail_even, m_run)

    lax.cond(jnp.max(bound) <= BOUND_MAX, bounded, online)

    acc = acc_ref[...]
    inv_l = 1.0 / acc[A_V_DIM:A_V_DIM + 1]
    o = acc[:A_V_DIM, :tq] * inv_l[:, :tq] - lam_full * (acc[:A_V_DIM, tq:] * inv_l[:, tq:])
    ms = jnp.mean(o * o, axis=0, keepdims=True)
    o = o * lax.rsqrt(ms + NORM_EPS) * g_ref[...] * (1.0 - lam_init)
    o_ref[...] = o.astype(BF16)


def _attention(q, k, vt, k_norms, lam, g_col, batch, seq, lam_init):
    tq, tk = ATT_TQ, ATT_TK
    assert tk % tq == 0 and tq & (tq - 1) == 0
    nq = seq // tq
    nk = seq // tk
    t = batch * seq
    return pl.pallas_call(
        functools.partial(_attn_kernel, lam_init=lam_init),
        grid=(batch, A_HEADS, nq),
        in_specs=[pl.BlockSpec((4, A_HEAD_DIM), lambda b, h, i: (0, 0)),
                  pl.BlockSpec((nk, 1, LANES), lambda b, h, i: (b, 0, 0)),
                  pl.BlockSpec((LANES, tq), lambda b, h, i: (h, b * nq + i)),
                  pl.BlockSpec((seq, LANES), lambda b, h, i: (b, h)),
                  pl.BlockSpec((1, nk, 1, V_ROWS, tk), lambda b, h, i: (b, 0, h, 0, 0)),
                  pl.BlockSpec((A_V_DIM, 1), lambda b, h, i: (h, 0))],
        out_specs=pl.BlockSpec((A_V_DIM, tq), lambda b, h, i: (h, b * nq + i)),
        out_shape=jax.ShapeDtypeStruct((D_MODEL, t), BF16),
        scratch_shapes=[pltpu.VMEM((tk, 2 * tq), F32),
                        pltpu.VMEM((tk, 2 * tq), F32),
                        pltpu.VMEM((V_ROWS, 2 * tq), F32)],
        compiler_params=_params("parallel", "parallel", "arbitrary"),
        name="diff_attention",
    )(lam, k_norms, q, k, vt, g_col)


def _head_perm(w):
    half = A_HEAD_DIM // 2
    w = w.reshape(w.shape[0], A_HEADS, 2, 2, half)
    return w.transpose(0, 1, 3, 2, 4).reshape(w.shape[0], D_MODEL)


def kernel(x, positions, norm_g, a_w_in, a_b_gates, a_g_head, a_w_out, kv_norm_g, w_kv,
           b_w_q, b_lam, b_g_head, b_w_out, mlp_w_up, mlp_w_down):
    batch, seq, _ = x.shape
    t = batch * seq
    depth = norm_g.shape[0]
    n_a = a_w_in.shape[0]
    xf = x.reshape(t, D_MODEL)
    row = lambda v: v.reshape(1, -1).astype(F32)

    cos, sin, cos_t, sin_t = _rope_tables(positions.reshape(1, t))
    k_sh = vt_sh = k_norms = None
    for layer in range(depth):
        g = norm_g[layer]
        if layer < n_a:
            w_in = a_w_in[layer]
            vo_end = 2 * M_QK_COLS + 2 * D_MODEL
            w_voq = jnp.concatenate([w_in[:, 2 * M_QK_COLS:vo_end], w_in[:, :M_QK_COLS]],
                                    axis=1).astype(BF16)
            w_kt = (w_in[:, M_QK_COLS:2 * M_QK_COLS] * M_QK_DIM ** -0.5).T.astype(BF16)
            w_gt = w_in[:, vo_end:].T.astype(BF16)
            voq, kt, gates = _a_proj(xf, row(g[0]), w_voq, w_kt, w_gt,
                                     a_b_gates[layer].reshape(-1, 1).astype(F32))
            y = _mlstm(voq, kt, gates, row(a_g_head[layer]), batch, seq)
            w_out = a_w_out[layer]
        else:
            j = layer - n_a
            if j == 0:
                w_k = _head_perm(w_kv[:, :D_MODEL]).astype(BF16)
                w_vt = w_kv[:, D_MODEL:].T.astype(BF16)
                k_sh, vt_sh, k_norms = _kv_proj(xf, row(kv_norm_g), w_k, w_vt, cos, sin,
                                                batch, seq)
            lam_init = 0.8 - 0.6 * math.exp(-0.3 * layer)
            w_qt = (_head_perm(b_w_q[j]) * (A_HEAD_DIM ** -0.5 * math.log2(math.e))
                    ).T.astype(BF16)
            q_t = _q_proj(xf, row(g[0]), w_qt, cos_t, sin_t)
            y = _attention(q_t, k_sh, vt_sh, k_norms, b_lam[j].astype(F32),
                           b_g_head[j].reshape(-1, 1).astype(F32), batch, seq, lam_init)
            w_out = b_w_out[j]
        xf = _post(xf, y, w_out.astype(BF16), row(g[1]), row(g[2]), row(g[3]),
                   mlp_w_up[layer].astype(BF16), mlp_w_down[layer].astype(BF16),
                   y_feature_major=layer >= n_a)
    return xf.reshape(batch, seq, D_MODEL)
```

```python
import functools
import math

import jax
import jax.numpy as jnp
from jax import lax
from jax.experimental import pallas as pl
from jax.experimental.pallas import tpu as pltpu

F32 = jnp.float32
BF16 = jnp.bfloat16

D_MODEL = 1024
D_FF = 4 * D_MODEL
NORM_EPS = 1e-6

M_HEADS = 4
M_V_DIM = D_MODEL // M_HEADS
M_QK_DIM = M_V_DIM // 2
M_QK_COLS = M_HEADS * M_QK_DIM
GATE_SOFTCAP = 15.0
M_CHUNK = 512

A_HEADS = 8
A_HEAD_DIM = D_MODEL // (2 * A_HEADS)
A_V_DIM = 2 * A_HEAD_DIM
ROPE_THETA = 10000.0

LANES = 128
VMEM_LIMIT_BYTES = 56 * 1024 * 1024

ROW_TILE = 512
ATT_TQ = 512
ATT_KV_TILE = 2 * ATT_TQ
ATT_TRIP_TILES = (8, 4, 2, 1)
BOUND_MAX = 40.0
BOUND_SLACK = 1.03
V_ROWS = A_V_DIM + 16

_NN = (((1,), (0,)), ((), ()))
_NT = (((1,), (1,)), ((), ()))
_TN = (((0,), (0,)), ((), ()))


def _params(*sem):
    return pltpu.CompilerParams(dimension_semantics=sem, vmem_limit_bytes=VMEM_LIMIT_BYTES)


def _rms(x, g):
    ms = jnp.mean(x * x, axis=-1, keepdims=True)
    return x * lax.rsqrt(ms + NORM_EPS) * g


def _const_spec(shape):
    nd = len(shape)
    return pl.BlockSpec(shape, lambda *_: (0,) * nd, pipeline_mode=pl.Buffered(1))


def _rope_table_kernel(pos_ref, inv_ref, sign_ref, cos_ref, sin_ref, cost_ref, sint_ref):
    ang = inv_ref[...] * pos_ref[...].astype(F32)
    cos32 = jnp.cos(ang)
    sin32 = jnp.sin(ang)
    cost_ref[...] = jnp.concatenate([cos32, cos32], axis=0)
    sint_ref[...] = jnp.concatenate([sin32, sin32], axis=0)
    cos_ref[...] = jnp.concatenate([cos32] * 4, axis=0).T
    sin_ref[...] = jnp.concatenate([sin32] * 4, axis=0).T * sign_ref[...]


def _rope_tables(pos_row):
    t = pos_row.shape[1]
    half = A_HEAD_DIM // 2
    inv = 1.0 / (ROPE_THETA ** (jnp.arange(0, A_HEAD_DIM, 2, dtype=F32) / A_HEAD_DIM))
    sign = jnp.where(jnp.arange(LANES) < LANES // 2, -1.0, 1.0).astype(F32).reshape(1, LANES)
    tm = ROW_TILE
    return pl.pallas_call(
        _rope_table_kernel,
        grid=(t // tm,),
        in_specs=[pl.BlockSpec((1, tm), lambda i: (0, i)),
                  _const_spec((half, 1)), _const_spec((1, LANES))],
        out_specs=[pl.BlockSpec((tm, LANES), lambda i: (i, 0))] * 2
        + [pl.BlockSpec((A_HEAD_DIM, tm), lambda i: (0, i))] * 2,
        out_shape=[jax.ShapeDtypeStruct((t, LANES), F32)] * 2
        + [jax.ShapeDtypeStruct((A_HEAD_DIM, t), F32)] * 2,
        compiler_params=_params("parallel"),
        name="rope_tables",
    )(pos_row, inv.reshape(half, 1), sign)


def _rope(t, cos, sin):
    return t * cos + pltpu.roll(t, LANES // 2, axis=1) * sin


def _a_proj_kernel(x_ref, g_ref, w_ref, wkt_ref, wgt_ref, bg_ref, voq_ref, kt_ref, gates_ref):
    hb = _rms(x_ref[...], g_ref[...]).astype(BF16)
    n_out = voq_ref.shape[1]
    for n in range(0, n_out, 512):
        voq_ref[:, n:n + 512] = jnp.dot(
            hb, w_ref[:, n:n + 512], preferred_element_type=F32).astype(BF16)
    kt_ref[...] = lax.dot_general(wkt_ref[...], hb, _NT,
                                  preferred_element_type=F32).astype(BF16)
    gr = lax.dot_general(wgt_ref[...], hb, _NT, preferred_element_type=F32) + bg_ref[...]
    gr = GATE_SOFTCAP * jnp.tanh(gr / GATE_SOFTCAP)
    log_sig = jnp.minimum(gr, 0.0) - jnp.log1p(jnp.exp(-jnp.abs(gr)))
    row = lax.broadcasted_iota(jnp.int32, gr.shape, 0)
    gates_ref[...] = jnp.where(row < M_HEADS, gr, log_sig)


def _a_proj(x, g, w_voq, w_kt, w_gt, b_g):
    t = x.shape[0]
    tm = ROW_TILE
    n_out = w_voq.shape[1]
    return pl.pallas_call(
        _a_proj_kernel,
        grid=(t // tm,),
        in_specs=[pl.BlockSpec((tm, D_MODEL), lambda i: (i, 0)),
                  _const_spec((1, D_MODEL)),
                  _const_spec((D_MODEL, n_out)),
                  _const_spec((M_QK_COLS, D_MODEL)),
                  _const_spec((2 * M_HEADS, D_MODEL)),
                  _const_spec((2 * M_HEADS, 1))],
        out_specs=[pl.BlockSpec((tm, n_out), lambda i: (i, 0)),
                   pl.BlockSpec((M_QK_COLS, tm), lambda i: (0, i)),
                   pl.BlockSpec((2 * M_HEADS, tm), lambda i: (0, i))],
        out_shape=[jax.ShapeDtypeStruct((t, n_out), BF16),
                   jax.ShapeDtypeStruct((M_QK_COLS, t), BF16),
                   jax.ShapeDtypeStruct((2 * M_HEADS, t), F32)],
        compiler_params=_params("parallel"),
        name="mlstm_proj",
    )(x, g, w_voq, w_kt, w_gt, b_g)


def _mlstm_kernel(q_ref, kt_ref, v_ref, o_ref, g_ref, gh_ref, y_ref, c_ref, m_ref):
    L = q_ref.shape[0]

    @pl.when(pl.program_id(1) == 0)
    def _():
        c_ref[...] = jnp.zeros_like(c_ref)
        m_ref[...] = jnp.zeros_like(m_ref)

    jj = lax.broadcasted_iota(jnp.int32, (L, L), 0)
    ss = lax.broadcasted_iota(jnp.int32, (L, L), 1)
    causal = ss <= jj
    upper = (jj <= ss).astype(F32)

    gates = g_ref[...]
    cums = jnp.dot(gates, upper, preferred_element_type=F32, precision=lax.Precision.HIGHEST)
    b8 = pltpu.roll(cums, M_HEADS, axis=0)
    r8 = gates - b8
    lane = lax.broadcasted_iota(jnp.int32, r8.shape, 1)
    cm8 = r8
    shift = 1
    while shift < L:
        cm8 = jnp.maximum(cm8, jnp.where(lane >= shift, pltpu.roll(cm8, shift, axis=1), -jnp.inf))
        shift *= 2
    stack = jnp.concatenate([cm8, b8, jnp.zeros((LANES - 16, L), F32)], axis=0)
    cols = stack.T
    lane1 = lax.broadcasted_iota(jnp.int32, (1, LANES), 1)
    m_prev = [m_ref[h] for h in range(M_HEADS)]
    m_lanes = sum(jnp.where(lane1 == h, m_prev[h], 0.0) for h in range(M_HEADS))
    m_cols = jnp.maximum(cols, m_lanes)
    b_cols = pltpu.roll(cols, LANES - 8, axis=1)
    is_head = lane1 < M_HEADS
    s_inter = jnp.where(is_head, jnp.exp(m_lanes - m_cols), 0.0).astype(BF16)
    floor = jnp.where(is_head, jnp.exp(-(b_cols + m_cols)), 0.0).astype(BF16)
    spread = (lax.broadcasted_iota(jnp.int32, (LANES, M_HEADS * LANES), 0)
              == lax.broadcasted_iota(jnp.int32, (LANES, M_HEADS * LANES), 1) // LANES
              ).astype(BF16)
    s_inter = jnp.dot(s_inter, spread, preferred_element_type=F32)
    floor = jnp.dot(floor, spread, preferred_element_type=F32)
    ones_blk = jnp.ones((L, LANES), BF16)
    mean_blk = jnp.full((M_V_DIM, LANES), 1.0 / M_V_DIM, BF16)

    for h in range(M_HEADS):
        lanes_h = slice(h * LANES, (h + 1) * LANES)
        q = q_ref[:, lanes_h]
        kt = kt_ref[lanes_h, :]
        v_ext = jnp.concatenate([v_ref[:, h * M_V_DIM:(h + 1) * M_V_DIM], ones_blk], axis=1)
        c_prev = c_ref[h]
        r_row = r8[h:h + 1]

        w_intra = jnp.exp(jnp.where(causal, r_row, -jnp.inf) - m_cols[:, h:h + 1])
        s_qk = jnp.dot(q, kt, preferred_element_type=F32) * w_intra
        s_in = s_inter[:, lanes_h]
        ext = (jnp.concatenate([s_in, s_in, s_in], axis=1)
               * jnp.dot(q, c_prev.astype(BF16), preferred_element_type=F32)
               + jnp.dot(s_qk.astype(BF16), v_ext, preferred_element_type=F32))
        den = ext[:, M_V_DIM:]
        inv = 1.0 / jnp.maximum(jnp.abs(den), floor[:, lanes_h])
        h_out = ext[:, :M_V_DIM] * jnp.concatenate([inv, inv], axis=1)

        m_last = jnp.maximum(cm8[h:h + 1, L - 1:L], m_prev[h])
        decay = jnp.exp(m_prev[h] - m_last)
        kwt = (kt.astype(F32) * jnp.exp(r_row - m_last)).astype(BF16)
        c_ref[h] = decay * c_prev + jnp.dot(kwt, v_ext, preferred_element_type=F32)
        m_ref[h] = b8[h:h + 1, L - 1:L] + m_last

        sl = slice(h * M_V_DIM, (h + 1) * M_V_DIM)
        ms = jnp.dot((h_out * h_out).astype(BF16), mean_blk, preferred_element_type=F32)
        rs = lax.rsqrt(ms + NORM_EPS)
        og = jax.nn.sigmoid(o_ref[:, sl].astype(F32))
        y_ref[:, sl] = (og * h_out * jnp.concatenate([rs, rs], axis=1)
                        * gh_ref[:, sl]).astype(BF16)


def _mlstm(voq, kt, gates, g_head, batch, seq):
    L = M_CHUNK
    nc = seq // L
    t = batch * seq
    row = lambda b, c: b * nc + c
    return pl.pallas_call(
        _mlstm_kernel,
        grid=(batch, nc),
        in_specs=[pl.BlockSpec((L, M_QK_COLS), lambda b, c: (row(b, c), 2 * D_MODEL // M_QK_COLS)),
                  pl.BlockSpec((M_QK_COLS, L), lambda b, c: (0, row(b, c))),
                  pl.BlockSpec((L, D_MODEL), lambda b, c: (row(b, c), 0)),
                  pl.BlockSpec((L, D_MODEL), lambda b, c: (row(b, c), 1)),
                  pl.BlockSpec((2 * M_HEADS, L), lambda b, c: (0, row(b, c))),
                  pl.BlockSpec((1, D_MODEL), lambda b, c: (0, 0))],
        out_specs=pl.BlockSpec((L, D_MODEL), lambda b, c: (row(b, c), 0)),
        out_shape=jax.ShapeDtypeStruct((t, D_MODEL), BF16),
        scratch_shapes=[pltpu.VMEM((M_HEADS, M_QK_DIM, M_V_DIM + LANES), F32),
                        pltpu.VMEM((M_HEADS, 1, 1), F32)],
        compiler_params=_params("parallel", "arbitrary"),
        name="mlstm_mixer",
    )(voq, kt, voq, voq, gates, g_head)


def _post_kernel(x_ref, y_ref, wo_ref, g1_ref, g2_ref, g3_ref, wu_ref, wd_ref, out_ref, *,
                 y_feature_major):
    z = lax.dot_general(y_ref[...], wo_ref[...], _TN if y_feature_major else _NN,
                        preferred_element_type=F32)
    x1 = x_ref[...] + _rms(z, g1_ref[...])
    hb = _rms(x1, g2_ref[...]).astype(BF16)
    acc = jnp.zeros(x1.shape, F32)
    for f in range(0, D_FF, 1024):
        u = jnp.maximum(jnp.dot(hb, wu_ref[:, f:f + 1024], preferred_element_type=F32), 0.0)
        acc = acc + jnp.dot((u * u).astype(BF16), wd_ref[f:f + 1024, :],
                            preferred_element_type=F32)
    out_ref[...] = x1 + _rms(acc, g3_ref[...])


def _post(x, y, w_out, g1, g2, g3, w_up, w_down, y_feature_major):
    t = x.shape[0]
    tm = ROW_TILE
    y_spec = (pl.BlockSpec((D_MODEL, tm), lambda i: (0, i)) if y_feature_major
              else pl.BlockSpec((tm, D_MODEL), lambda i: (i, 0)))
    return pl.pallas_call(
        functools.partial(_post_kernel, y_feature_major=y_feature_major),
        grid=(t // tm,),
        in_specs=[pl.BlockSpec((tm, D_MODEL), lambda i: (i, 0)),
                  y_spec,
                  _const_spec((D_MODEL, D_MODEL)),
                  _const_spec((1, D_MODEL)), _const_spec((1, D_MODEL)), _const_spec((1, D_MODEL)),
                  _const_spec((D_MODEL, D_FF)),
                  _const_spec((D_FF, D_MODEL))],
        out_specs=pl.BlockSpec((tm, D_MODEL), lambda i: (i, 0)),
        out_shape=jax.ShapeDtypeStruct((t, D_MODEL), F32),
        compiler_params=_params("parallel"),
        name="outproj_mlp",
    )(x, y, w_out, g1, g2, g3, w_up, w_down)


def _kv_proj_kernel(x_ref, g_ref, wk_ref, wvt_ref, cos_ref, sin_ref, k_ref, vt_ref, kn_ref):
    hb = _rms(x_ref[...], g_ref[...]).astype(BF16)
    cos = cos_ref[...]
    sin = sin_ref[...]
    kk = jnp.dot(hb, wk_ref[...], preferred_element_type=F32)
    col = lax.broadcasted_iota(jnp.int32, (D_MODEL, LANES), 0)
    out = lax.broadcasted_iota(jnp.int32, (D_MODEL, LANES), 1)
    pick = (out == 2 * (col >> 7) + ((col >> 5) & 1)).astype(BF16)
    k_sq = jnp.dot((kk * kk).astype(BF16), pick, preferred_element_type=F32)
    kn_ref[0] = jnp.max(k_sq, axis=0, keepdims=True)
    for h in range(A_HEADS):
        sl = slice(h * LANES, (h + 1) * LANES)
        k_ref[:, sl] = _rope(kk[:, sl], cos, sin).astype(BF16)
    vt = lax.dot_general(wvt_ref[...], hb, _NT, preferred_element_type=F32)
    ones = jnp.ones((V_ROWS - A_V_DIM, vt.shape[1]), BF16)
    for h in range(A_HEADS):
        vt_ref[0, 0, h, :A_V_DIM, :] = vt[h * A_V_DIM:(h + 1) * A_V_DIM].astype(BF16)
        vt_ref[0, 0, h, A_V_DIM:, :] = ones


def _kv_proj(x, g, w_k, w_vt, cos, sin, batch, seq):
    t = x.shape[0]
    tm = ATT_KV_TILE
    nk = seq // tm
    return pl.pallas_call(
        _kv_proj_kernel,
        grid=(t // tm,),
        in_specs=[pl.BlockSpec((tm, D_MODEL), lambda i: (i, 0)),
                  _const_spec((1, D_MODEL)),
                  _const_spec((D_MODEL, D_MODEL)),
                  _const_spec((D_MODEL, D_MODEL)),
                  pl.BlockSpec((tm, LANES), lambda i: (i, 0)),
                  pl.BlockSpec((tm, LANES), lambda i: (i, 0))],
        out_specs=[pl.BlockSpec((tm, D_MODEL), lambda i: (i, 0)),
                   pl.BlockSpec((1, 1, A_HEADS, V_ROWS, tm),
                                lambda i: (i // nk, i % nk, 0, 0, 0)),
                   pl.BlockSpec((1, 1, LANES), lambda i: (i, 0, 0))],
        out_shape=[jax.ShapeDtypeStruct((t, D_MODEL), BF16),
                   jax.ShapeDtypeStruct((batch, nk, A_HEADS, V_ROWS, tm), BF16),
                   jax.ShapeDtypeStruct((t // tm, 1, LANES), F32)],
        compiler_params=_params("parallel"),
        name="kv_proj",
    )(x, g, w_k, w_vt, cos, sin)


def _q_proj_kernel(x_ref, g_ref, wqt_ref, cos_ref, sin_ref, qt_ref):
    hb = _rms(x_ref[...], g_ref[...]).astype(BF16)
    cos = cos_ref[...]
    sin = sin_ref[...]
    qq = lax.dot_general(wqt_ref[...], hb, _NT, preferred_element_type=F32)
    half = LANES // 2
    for h in range(A_HEADS):
        lo = qq[h * LANES:h * LANES + half]
        hi = qq[h * LANES + half:(h + 1) * LANES]
        qt_ref[h * LANES:h * LANES + half, :] = (lo * cos - hi * sin).astype(BF16)
        qt_ref[h * LANES + half:(h + 1) * LANES, :] = (hi * cos + lo * sin).astype(BF16)


def _q_proj(x, g, w_qt, cos_t, sin_t):
    t = x.shape[0]
    tm = ROW_TILE
    return pl.pallas_call(
        _q_proj_kernel,
        grid=(t // tm,),
        in_specs=[pl.BlockSpec((tm, D_MODEL), lambda i: (i, 0)),
                  _const_spec((1, D_MODEL)),
                  _const_spec((D_MODEL, D_MODEL)),
                  pl.BlockSpec((A_HEAD_DIM, tm), lambda i: (0, i)),
                  pl.BlockSpec((A_HEAD_DIM, tm), lambda i: (0, i))],
        out_specs=pl.BlockSpec((D_MODEL, tm), lambda i: (0, i)),
        out_shape=jax.ShapeDtypeStruct((D_MODEL, t), BF16),
        compiler_params=_params("parallel"),
        name="q_proj",
    )(x, g, w_qt, cos_t, sin_t)


def _attn_kernel(lam_ref, kn_ref, q_ref, k_ref, vt_ref, g_ref, o_ref, sa_ref, sb_ref, acc_ref, *,
                 lam_init):
    tq = q_ref.shape[1]
    tk2 = vt_ref.shape[4]
    tk = tk2 // 2
    head = pl.program_id(1)
    i = pl.program_id(2)

    lam = lam_ref[...]
    lam_full = (jnp.exp(jnp.sum(lam[0:1] * lam[1:2], keepdims=True))
                - jnp.exp(jnp.sum(lam[2:3] * lam[3:4], keepdims=True)) + lam_init)

    q_t = q_ref[...].astype(F32)
    row_map = (lax.broadcasted_iota(jnp.int32, q_t.shape, 0) >> 5) & 1
    q_cat = jnp.concatenate([jnp.where(row_map == c, q_t, 0.0) for c in range(2)],
                            axis=1).astype(BF16)

    k_sq = jnp.max(kn_ref[...], axis=0)
    lane = lax.broadcasted_iota(jnp.int32, k_sq.shape, 1)
    q_sq = q_t * q_t
    bound = jnp.concatenate(
        [jnp.sqrt(jnp.sum(jnp.where(row_map == c, q_sq, 0.0), axis=0, keepdims=True)
                  * jnp.max(jnp.where(lane == 2 * head + c, k_sq, 0.0), axis=1, keepdims=True))
         for c in range(2)], axis=1) * BOUND_SLACK

    def causal(key0, s):
        key_i = key0 + lax.broadcasted_iota(jnp.int32, s.shape, 0)
        qry_i = i * tq + (lax.broadcasted_iota(jnp.int32, s.shape, 1) & (tq - 1))
        return jnp.where(key_i <= qry_i, s, -jnp.inf)

    def bounded():
        def tile(j, half, diagonal):
            kb = k_ref[pl.ds(pl.multiple_of(j * tk, tk), tk), :]
            s = jnp.dot(kb, q_cat, preferred_element_type=F32) - bound
            if diagonal:
                s = causal(j * tk, s)
            return jnp.dot(vt_ref[0, j // 2, 0, :, half * tk:(half + 1) * tk],
                           jnp.exp2(s).astype(BF16), preferred_element_type=F32)

        def stored_tile(jb):
            kb = k_ref[pl.ds(pl.multiple_of(jb * tk2, tk2), tk2), :]
            s = jnp.dot(kb, q_cat, preferred_element_type=F32) - bound
            return jnp.dot(vt_ref[0, jb, 0], jnp.exp2(s).astype(BF16),
                           preferred_element_type=F32)

        def trip(n_tiles, first):
            def body(jj, carry):
                jb = first + n_tiles * jj
                total = stored_tile(jb)
                for t in range(1, n_tiles):
                    total = total + stored_tile(jb + t)
                acc_ref[...] += total
                return carry
            return body

        n_pairs = i // 2

        def odd_start():
            acc_ref[...] = tile(i - 1, 0, False) + tile(i, 1, True)

        def even_start():
            acc_ref[...] = tile(i, 0, True)

        lax.cond(i % 2 == 1, odd_start, even_start)
        done = 0
        for n_tiles in ATT_TRIP_TILES:
            n_trips = (n_pairs - done) // n_tiles
            lax.fori_loop(0, n_trips, trip(n_tiles, done), 0)
            done = done + n_trips * n_tiles

    def online():
        tk = tk2
        n_full = (i * tq) // tk

        def scores(j, s_ref):
            kb = k_ref[pl.ds(pl.multiple_of(j * tk, tk), tk), :]
            s_ref[...] = jnp.dot(kb, q_cat, preferred_element_type=F32)

        def softmax_pv(j, s_ref, m_old, diagonal):
            s = s_ref[...]
            if diagonal:
                s = causal(j * tk, s)
            m_new = jnp.maximum(m_old, jnp.max(s, axis=0, keepdims=True))
            alpha = jnp.exp2(m_old - m_new)
            p = jnp.exp2(s - m_new).astype(BF16)
            acc_ref[...] = alpha * acc_ref[...] + jnp.dot(vt_ref[0, j, 0], p,
                                                          preferred_element_type=F32)
            return m_new

        acc_ref[...] = jnp.zeros_like(acc_ref)
        scores(0, sa_ref)

        def pair(jj, m_run):
            j = 2 * jj
            scores(j + 1, sb_ref)
            m_run = softmax_pv(j, sa_ref, m_run, False)
            scores(j + 2, sa_ref)
            return softmax_pv(j + 1, sb_ref, m_run, False)

        m_run = lax.fori_loop(0, n_full // 2, pair, jnp.full((1, 2 * tq), -jnp.inf, F32))

        def tail_odd(m_old):
            scores(n_full, sb_ref)
            m_mid = softmax_pv(n_full - 1, sa_ref, m_old, False)
            return softmax_pv(n_full, sb_ref, m_mid, True)

        def tail_even(m_old):
            return softmax_pv(n_full, sa_ref, m_old, True)

        lax.cond(n_full % 2 == 1, tail_odd, tail_even, m_run)

    lax.cond(jnp.max(bound) <= BOUND_MAX, bounded, online)

    acc = acc_ref[...]
    inv_l = 1.0 / acc[A_V_DIM:A_V_DIM + 1]
    o = acc[:A_V_DIM, :tq] * inv_l[:, :tq] - lam_full * (acc[:A_V_DIM, tq:] * inv_l[:, tq:])
    ms = jnp.mean(o * o, axis=0, keepdims=True)
    o = o * lax.rsqrt(ms + NORM_EPS) * g_ref[...] * (1.0 - lam_init)
    o_ref[...] = o.astype(BF16)


def _attention(q, k, vt, k_norms, lam, g_col, batch, seq, lam_init):
    tq, tk = ATT_TQ, ATT_KV_TILE
    assert tk == 2 * tq and tq & (tq - 1) == 0
    nq = seq // tq
    nk = seq // tk
    t = batch * seq
    return pl.pallas_call(
        functools.partial(_attn_kernel, lam_init=lam_init),
        grid=(batch, A_HEADS, nq),
        in_specs=[pl.BlockSpec((4, A_HEAD_DIM), lambda b, h, i: (0, 0)),
                  pl.BlockSpec((nk, 1, LANES), lambda b, h, i: (b, 0, 0)),
                  pl.BlockSpec((LANES, tq), lambda b, h, i: (h, b * nq + i)),
                  pl.BlockSpec((seq, LANES), lambda b, h, i: (b, h)),
                  pl.BlockSpec((1, nk, 1, V_ROWS, tk), lambda b, h, i: (b, 0, h, 0, 0)),
                  pl.BlockSpec((A_V_DIM, 1), lambda b, h, i: (h, 0))],
        out_specs=pl.BlockSpec((A_V_DIM, tq), lambda b, h, i: (h, b * nq + i)),
        out_shape=jax.ShapeDtypeStruct((D_MODEL, t), BF16),
        scratch_shapes=[pltpu.VMEM((tk, 2 * tq), F32),
                        pltpu.VMEM((tk, 2 * tq), F32),
                        pltpu.VMEM((V_ROWS, 2 * tq), F32)],
        compiler_params=_params("parallel", "parallel", "arbitrary"),
        name="diff_attention",
    )(lam, k_norms, q, k, vt, g_col)


def _head_perm(w):
    half = A_HEAD_DIM // 2
    w = w.reshape(w.shape[0], A_HEADS, 2, 2, half)
    return w.transpose(0, 1, 3, 2, 4).reshape(w.shape[0], D_MODEL)


def kernel(x, positions, norm_g, a_w_in, a_b_gates, a_g_head, a_w_out, kv_norm_g, w_kv,
           b_w_q, b_lam, b_g_head, b_w_out, mlp_w_up, mlp_w_down):
    batch, seq, _ = x.shape
    t = batch * seq
    depth = norm_g.shape[0]
    n_a = a_w_in.shape[0]
    xf = x.reshape(t, D_MODEL)
    row = lambda v: v.reshape(1, -1).astype(F32)

    cos, sin, cos_t, sin_t = _rope_tables(positions.reshape(1, t))
    k_sh = vt_sh = k_norms = None
    for layer in range(depth):
        g = norm_g[layer]
        if layer < n_a:
            w_in = a_w_in[layer]
            vo_end = 2 * M_QK_COLS + 2 * D_MODEL
            w_voq = jnp.concatenate([w_in[:, 2 * M_QK_COLS:vo_end], w_in[:, :M_QK_COLS]],
                                    axis=1).astype(BF16)
            w_kt = (w_in[:, M_QK_COLS:2 * M_QK_COLS] * M_QK_DIM ** -0.5).T.astype(BF16)
            w_gt = w_in[:, vo_end:].T.astype(BF16)
            voq, kt, gates = _a_proj(xf, row(g[0]), w_voq, w_kt, w_gt,
                                     a_b_gates[layer].reshape(-1, 1).astype(F32))
            y = _mlstm(voq, kt, gates, row(a_g_head[layer]), batch, seq)
            w_out = a_w_out[layer]
        else:
            j = layer - n_a
            if j == 0:
                w_k = _head_perm(w_kv[:, :D_MODEL]).astype(BF16)
                w_vt = w_kv[:, D_MODEL:].T.astype(BF16)
                k_sh, vt_sh, k_norms = _kv_proj(xf, row(kv_norm_g), w_k, w_vt, cos, sin,
                                                batch, seq)
            lam_init = 0.8 - 0.6 * math.exp(-0.3 * layer)
            w_qt = (_head_perm(b_w_q[j]) * (A_HEAD_DIM ** -0.5 * math.log2(math.e))
                    ).T.astype(BF16)
            q_t = _q_proj(xf, row(g[0]), w_qt, cos_t, sin_t)
            y = _attention(q_t, k_sh, vt_sh, k_norms, b_lam[j].astype(F32),
                           b_g_head[j].reshape(-1, 1).astype(F32), batch, seq, lam_init)
            w_out = b_w_out[j]
        xf = _post(xf, y, w_out.astype(BF16), row(g[1]), row(g[2]), row(g[3]),
                   mlp_w_up[layer].astype(BF16), mlp_w_down[layer].astype(BF16),
                   y_feature_major=layer >= n_a)
    return xf.reshape(batch, seq, D_MODEL)
```

```python
import functools
import math

import jax
import jax.numpy as jnp
from jax import lax
from jax.experimental import pallas as pl
from jax.experimental.pallas import tpu as pltpu

F32 = jnp.float32
BF16 = jnp.bfloat16

D_MODEL = 1024
D_FF = 4 * D_MODEL
NORM_EPS = 1e-6

M_HEADS = 4
M_V_DIM = D_MODEL // M_HEADS
M_QK_DIM = M_V_DIM // 2
M_QK_COLS = M_HEADS * M_QK_DIM
GATE_SOFTCAP = 15.0
M_CHUNK = 512

A_HEADS = 8
A_HEAD_DIM = D_MODEL // (2 * A_HEADS)
A_V_DIM = 2 * A_HEAD_DIM
ROPE_THETA = 10000.0

LANES = 128
VMEM_LIMIT_BYTES = 56 * 1024 * 1024

ROW_TILE = 512
ATT_TQ = 512
ATT_KV_TILE = 4 * ATT_TQ
KV_PROJ_TILE = 1024
ATT_TRIP_TILES = (4, 2, 1)
BOUND_MAX = 40.0
BOUND_SLACK = 1.03
V_ROWS = A_V_DIM + 16

_NN = (((1,), (0,)), ((), ()))
_NT = (((1,), (1,)), ((), ()))
_TN = (((0,), (0,)), ((), ()))


def _params(*sem):
    return pltpu.CompilerParams(dimension_semantics=sem, vmem_limit_bytes=VMEM_LIMIT_BYTES)


def _rms(x, g):
    ms = jnp.mean(x * x, axis=-1, keepdims=True)
    return x * lax.rsqrt(ms + NORM_EPS) * g


def _const_spec(shape):
    nd = len(shape)
    return pl.BlockSpec(shape, lambda *_: (0,) * nd, pipeline_mode=pl.Buffered(1))


def _rope_table_kernel(pos_ref, inv_ref, sign_ref, cos_ref, sin_ref, cost_ref, sint_ref):
    ang = inv_ref[...] * pos_ref[...].astype(F32)
    cos32 = jnp.cos(ang)
    sin32 = jnp.sin(ang)
    cost_ref[...] = jnp.concatenate([cos32, cos32], axis=0)
    sint_ref[...] = jnp.concatenate([sin32, sin32], axis=0)
    cos_ref[...] = jnp.concatenate([cos32] * 4, axis=0).T
    sin_ref[...] = jnp.concatenate([sin32] * 4, axis=0).T * sign_ref[...]


def _rope_tables(pos_row):
    t = pos_row.shape[1]
    half = A_HEAD_DIM // 2
    inv = 1.0 / (ROPE_THETA ** (jnp.arange(0, A_HEAD_DIM, 2, dtype=F32) / A_HEAD_DIM))
    sign = jnp.where(jnp.arange(LANES) < LANES // 2, -1.0, 1.0).astype(F32).reshape(1, LANES)
    tm = ROW_TILE
    return pl.pallas_call(
        _rope_table_kernel,
        grid=(t // tm,),
        in_specs=[pl.BlockSpec((1, tm), lambda i: (0, i)),
                  _const_spec((half, 1)), _const_spec((1, LANES))],
        out_specs=[pl.BlockSpec((tm, LANES), lambda i: (i, 0))] * 2
        + [pl.BlockSpec((A_HEAD_DIM, tm), lambda i: (0, i))] * 2,
        out_shape=[jax.ShapeDtypeStruct((t, LANES), F32)] * 2
        + [jax.ShapeDtypeStruct((A_HEAD_DIM, t), F32)] * 2,
        compiler_params=_params("parallel"),
        name="rope_tables",
    )(pos_row, inv.reshape(half, 1), sign)


def _rope(t, cos, sin):
    return t * cos + pltpu.roll(t, LANES // 2, axis=1) * sin


def _a_proj_kernel(x_ref, g_ref, w_ref, wkt_ref, wgt_ref, bg_ref, voq_ref, kt_ref, gates_ref):
    hb = _rms(x_ref[...], g_ref[...]).astype(BF16)
    n_out = voq_ref.shape[1]
    for n in range(0, n_out, 512):
        voq_ref[:, n:n + 512] = jnp.dot(
            hb, w_ref[:, n:n + 512], preferred_element_type=F32).astype(BF16)
    kt_ref[...] = lax.dot_general(wkt_ref[...], hb, _NT,
                                  preferred_element_type=F32).astype(BF16)
    gr = lax.dot_general(wgt_ref[...], hb, _NT, preferred_element_type=F32) + bg_ref[...]
    gr = GATE_SOFTCAP * jnp.tanh(gr / GATE_SOFTCAP)
    log_sig = jnp.minimum(gr, 0.0) - jnp.log1p(jnp.exp(-jnp.abs(gr)))
    row = lax.broadcasted_iota(jnp.int32, gr.shape, 0)
    gates_ref[...] = jnp.where(row < M_HEADS, gr, log_sig)


def _a_proj(x, g, w_voq, w_kt, w_gt, b_g):
    t = x.shape[0]
    tm = ROW_TILE
    n_out = w_voq.shape[1]
    return pl.pallas_call(
        _a_proj_kernel,
        grid=(t // tm,),
        in_specs=[pl.BlockSpec((tm, D_MODEL), lambda i: (i, 0)),
                  _const_spec((1, D_MODEL)),
                  _const_spec((D_MODEL, n_out)),
                  _const_spec((M_QK_COLS, D_MODEL)),
                  _const_spec((2 * M_HEADS, D_MODEL)),
                  _const_spec((2 * M_HEADS, 1))],
        out_specs=[pl.BlockSpec((tm, n_out), lambda i: (i, 0)),
                   pl.BlockSpec((M_QK_COLS, tm), lambda i: (0, i)),
                   pl.BlockSpec((2 * M_HEADS, tm), lambda i: (0, i))],
        out_shape=[jax.ShapeDtypeStruct((t, n_out), BF16),
                   jax.ShapeDtypeStruct((M_QK_COLS, t), BF16),
                   jax.ShapeDtypeStruct((2 * M_HEADS, t), F32)],
        compiler_params=_params("parallel"),
        name="mlstm_proj",
    )(x, g, w_voq, w_kt, w_gt, b_g)


def _mlstm_kernel(q_ref, kt_ref, v_ref, o_ref, g_ref, gh_ref, y_ref, c_ref, m_ref):
    L = q_ref.shape[0]

    @pl.when(pl.program_id(1) == 0)
    def _():
        c_ref[...] = jnp.zeros_like(c_ref)
        m_ref[...] = jnp.zeros_like(m_ref)

    jj = lax.broadcasted_iota(jnp.int32, (L, L), 0)
    ss = lax.broadcasted_iota(jnp.int32, (L, L), 1)
    causal = ss <= jj
    upper = (jj <= ss).astype(F32)

    gates = g_ref[...]
    cums = jnp.dot(gates, upper, preferred_element_type=F32, precision=lax.Precision.HIGHEST)
    b8 = pltpu.roll(cums, M_HEADS, axis=0)
    r8 = gates - b8
    lane = lax.broadcasted_iota(jnp.int32, r8.shape, 1)
    cm8 = r8
    shift = 1
    while shift < L:
        cm8 = jnp.maximum(cm8, jnp.where(lane >= shift, pltpu.roll(cm8, shift, axis=1), -jnp.inf))
        shift *= 2
    stack = jnp.concatenate([cm8, b8, jnp.zeros((LANES - 16, L), F32)], axis=0)
    cols = stack.T
    lane1 = lax.broadcasted_iota(jnp.int32, (1, LANES), 1)
    m_prev = [m_ref[h] for h in range(M_HEADS)]
    m_lanes = sum(jnp.where(lane1 == h, m_prev[h], 0.0) for h in range(M_HEADS))
    m_cols = jnp.maximum(cols, m_lanes)
    b_cols = pltpu.roll(cols, LANES - 8, axis=1)
    is_head = lane1 < M_HEADS
    s_inter = jnp.where(is_head, jnp.exp(m_lanes - m_cols), 0.0).astype(BF16)
    floor = jnp.where(is_head, jnp.exp(-(b_cols + m_cols)), 0.0).astype(BF16)
    spread = (lax.broadcasted_iota(jnp.int32, (LANES, M_HEADS * LANES), 0)
              == lax.broadcasted_iota(jnp.int32, (LANES, M_HEADS * LANES), 1) // LANES
              ).astype(BF16)
    s_inter = jnp.dot(s_inter, spread, preferred_element_type=F32)
    floor = jnp.dot(floor, spread, preferred_element_type=F32)
    ones_blk = jnp.ones((L, LANES), BF16)
    mean_blk = jnp.full((M_V_DIM, LANES), 1.0 / M_V_DIM, BF16)

    for h in range(M_HEADS):
        lanes_h = slice(h * LANES, (h + 1) * LANES)
        q = q_ref[:, lanes_h]
        kt = kt_ref[lanes_h, :]
        v_ext = jnp.concatenate([v_ref[:, h * M_V_DIM:(h + 1) * M_V_DIM], ones_blk], axis=1)
        c_prev = c_ref[h]
        r_row = r8[h:h + 1]

        w_intra = jnp.exp(jnp.where(causal, r_row, -jnp.inf) - m_cols[:, h:h + 1])
        s_qk = jnp.dot(q, kt, preferred_element_type=F32) * w_intra
        s_in = s_inter[:, lanes_h]
        ext = (jnp.concatenate([s_in, s_in, s_in], axis=1)
               * jnp.dot(q, c_prev.astype(BF16), preferred_element_type=F32)
               + jnp.dot(s_qk.astype(BF16), v_ext, preferred_element_type=F32))
        den = ext[:, M_V_DIM:]
        inv = 1.0 / jnp.maximum(jnp.abs(den), floor[:, lanes_h])
        h_out = ext[:, :M_V_DIM] * jnp.concatenate([inv, inv], axis=1)

        m_last = jnp.maximum(cm8[h:h + 1, L - 1:L], m_prev[h])
        decay = jnp.exp(m_prev[h] - m_last)
        kwt = (kt.astype(F32) * jnp.exp(r_row - m_last)).astype(BF16)
        c_ref[h] = decay * c_prev + jnp.dot(kwt, v_ext, preferred_element_type=F32)
        m_ref[h] = b8[h:h + 1, L - 1:L] + m_last

        sl = slice(h * M_V_DIM, (h + 1) * M_V_DIM)
        ms = jnp.dot((h_out * h_out).astype(BF16), mean_blk, preferred_element_type=F32)
        rs = lax.rsqrt(ms + NORM_EPS)
        og = jax.nn.sigmoid(o_ref[:, sl].astype(F32))
        y_ref[:, sl] = (og * h_out * jnp.concatenate([rs, rs], axis=1)
                        * gh_ref[:, sl]).astype(BF16)


def _mlstm(voq, kt, gates, g_head, batch, seq):
    L = M_CHUNK
    nc = seq // L
    t = batch * seq
    row = lambda b, c: b * nc + c
    return pl.pallas_call(
        _mlstm_kernel,
        grid=(batch, nc),
        in_specs=[pl.BlockSpec((L, M_QK_COLS), lambda b, c: (row(b, c), 2 * D_MODEL // M_QK_COLS)),
                  pl.BlockSpec((M_QK_COLS, L), lambda b, c: (0, row(b, c))),
                  pl.BlockSpec((L, D_MODEL), lambda b, c: (row(b, c), 0)),
                  pl.BlockSpec((L, D_MODEL), lambda b, c: (row(b, c), 1)),
                  pl.BlockSpec((2 * M_HEADS, L), lambda b, c: (0, row(b, c))),
                  pl.BlockSpec((1, D_MODEL), lambda b, c: (0, 0))],
        out_specs=pl.BlockSpec((L, D_MODEL), lambda b, c: (row(b, c), 0)),
        out_shape=jax.ShapeDtypeStruct((t, D_MODEL), BF16),
        scratch_shapes=[pltpu.VMEM((M_HEADS, M_QK_DIM, M_V_DIM + LANES), F32),
                        pltpu.VMEM((M_HEADS, 1, 1), F32)],
        compiler_params=_params("parallel", "arbitrary"),
        name="mlstm_mixer",
    )(voq, kt, voq, voq, gates, g_head)


def _post_kernel(x_ref, y_ref, wo_ref, g1_ref, g2_ref, g3_ref, wu_ref, wd_ref, out_ref, *,
                 y_feature_major):
    z = lax.dot_general(y_ref[...], wo_ref[...], _TN if y_feature_major else _NN,
                        preferred_element_type=F32)
    x1 = x_ref[...] + _rms(z, g1_ref[...])
    hb = _rms(x1, g2_ref[...]).astype(BF16)
    acc = jnp.zeros(x1.shape, F32)
    for f in range(0, D_FF, 1024):
        u = jnp.maximum(jnp.dot(hb, wu_ref[:, f:f + 1024], preferred_element_type=F32), 0.0)
        acc = acc + jnp.dot((u * u).astype(BF16), wd_ref[f:f + 1024, :],
                            preferred_element_type=F32)
    out_ref[...] = x1 + _rms(acc, g3_ref[...])


def _post(x, y, w_out, g1, g2, g3, w_up, w_down, y_feature_major):
    t = x.shape[0]
    tm = ROW_TILE
    y_spec = (pl.BlockSpec((D_MODEL, tm), lambda i: (0, i)) if y_feature_major
              else pl.BlockSpec((tm, D_MODEL), lambda i: (i, 0)))
    return pl.pallas_call(
        functools.partial(_post_kernel, y_feature_major=y_feature_major),
        grid=(t // tm,),
        in_specs=[pl.BlockSpec((tm, D_MODEL), lambda i: (i, 0)),
                  y_spec,
                  _const_spec((D_MODEL, D_MODEL)),
                  _const_spec((1, D_MODEL)), _const_spec((1, D_MODEL)), _const_spec((1, D_MODEL)),
                  _const_spec((D_MODEL, D_FF)),
                  _const_spec((D_FF, D_MODEL))],
        out_specs=pl.BlockSpec((tm, D_MODEL), lambda i: (i, 0)),
        out_shape=jax.ShapeDtypeStruct((t, D_MODEL), F32),
        compiler_params=_params("parallel"),
        name="outproj_mlp",
    )(x, y, w_out, g1, g2, g3, w_up, w_down)


def _kv_proj_kernel(x_ref, g_ref, wk_ref, wvt_ref, cos_ref, sin_ref, k_ref, vt_ref, kn_ref):
    hb = _rms(x_ref[...], g_ref[...]).astype(BF16)
    cos = cos_ref[...]
    sin = sin_ref[...]
    kk = jnp.dot(hb, wk_ref[...], preferred_element_type=F32)
    col = lax.broadcasted_iota(jnp.int32, (D_MODEL, LANES), 0)
    out = lax.broadcasted_iota(jnp.int32, (D_MODEL, LANES), 1)
    pick = (out == 2 * (col >> 7) + ((col >> 5) & 1)).astype(BF16)
    k_sq = jnp.dot((kk * kk).astype(BF16), pick, preferred_element_type=F32)
    kn_ref[0] = jnp.max(k_sq, axis=0, keepdims=True)
    for h in range(A_HEADS):
        sl = slice(h * LANES, (h + 1) * LANES)
        k_ref[:, sl] = _rope(kk[:, sl], cos, sin).astype(BF16)
    vt = lax.dot_general(wvt_ref[...], hb, _NT, preferred_element_type=F32)
    ones = jnp.ones((V_ROWS - A_V_DIM, vt.shape[1]), BF16)
    for h in range(A_HEADS):
        vt_ref[0, 0, h, :A_V_DIM, :] = vt[h * A_V_DIM:(h + 1) * A_V_DIM].astype(BF16)
        vt_ref[0, 0, h, A_V_DIM:, :] = ones


def _kv_proj(x, g, w_k, w_vt, cos, sin, batch, seq):
    t = x.shape[0]
    tm = KV_PROJ_TILE
    per_stored = ATT_KV_TILE // tm
    per_batch = seq // tm
    return pl.pallas_call(
        _kv_proj_kernel,
        grid=(t // tm,),
        in_specs=[pl.BlockSpec((tm, D_MODEL), lambda i: (i, 0)),
                  _const_spec((1, D_MODEL)),
                  _const_spec((D_MODEL, D_MODEL)),
                  _const_spec((D_MODEL, D_MODEL)),
                  pl.BlockSpec((tm, LANES), lambda i: (i, 0)),
                  pl.BlockSpec((tm, LANES), lambda i: (i, 0))],
        out_specs=[pl.BlockSpec((tm, D_MODEL), lambda i: (i, 0)),
                   pl.BlockSpec((1, 1, A_HEADS, V_ROWS, tm),
                                lambda i: (i // per_batch, (i % per_batch) // per_stored, 0, 0,
                                           i % per_stored)),
                   pl.BlockSpec((1, 1, LANES), lambda i: (i, 0, 0))],
        out_shape=[jax.ShapeDtypeStruct((t, D_MODEL), BF16),
                   jax.ShapeDtypeStruct((batch, seq // ATT_KV_TILE, A_HEADS, V_ROWS, ATT_KV_TILE),
                                        BF16),
                   jax.ShapeDtypeStruct((t // tm, 1, LANES), F32)],
        compiler_params=_params("parallel"),
        name="kv_proj",
    )(x, g, w_k, w_vt, cos, sin)


def _q_proj_kernel(x_ref, g_ref, wqt_ref, cos_ref, sin_ref, qt_ref):
    hb = _rms(x_ref[...], g_ref[...]).astype(BF16)
    cos = cos_ref[...]
    sin = sin_ref[...]
    qq = lax.dot_general(wqt_ref[...], hb, _NT, preferred_element_type=F32)
    half = LANES // 2
    for h in range(A_HEADS):
        lo = qq[h * LANES:h * LANES + half]
        hi = qq[h * LANES + half:(h + 1) * LANES]
        qt_ref[h * LANES:h * LANES + half, :] = (lo * cos - hi * sin).astype(BF16)
        qt_ref[h * LANES + half:(h + 1) * LANES, :] = (hi * cos + lo * sin).astype(BF16)


def _q_proj(x, g, w_qt, cos_t, sin_t):
    t = x.shape[0]
    tm = ROW_TILE
    return pl.pallas_call(
        _q_proj_kernel,
        grid=(t // tm,),
        in_specs=[pl.BlockSpec((tm, D_MODEL), lambda i: (i, 0)),
                  _const_spec((1, D_MODEL)),
                  _const_spec((D_MODEL, D_MODEL)),
                  pl.BlockSpec((A_HEAD_DIM, tm), lambda i: (0, i)),
                  pl.BlockSpec((A_HEAD_DIM, tm), lambda i: (0, i))],
        out_specs=pl.BlockSpec((D_MODEL, tm), lambda i: (0, i)),
        out_shape=jax.ShapeDtypeStruct((D_MODEL, t), BF16),
        compiler_params=_params("parallel"),
        name="q_proj",
    )(x, g, w_qt, cos_t, sin_t)


def _attn_kernel(lam_ref, kn_ref, q_ref, k_ref, vt_ref, g_ref, o_ref, sa_ref, sb_ref, acc_ref, *,
                 lam_init):
    tq = q_ref.shape[1]
    tks = vt_ref.shape[4]
    head = pl.program_id(1)
    i = pl.program_id(2)

    lam = lam_ref[...]
    lam_full = (jnp.exp(jnp.sum(lam[0:1] * lam[1:2], keepdims=True))
                - jnp.exp(jnp.sum(lam[2:3] * lam[3:4], keepdims=True)) + lam_init)

    q_t = q_ref[...].astype(F32)
    row_map = (lax.broadcasted_iota(jnp.int32, q_t.shape, 0) >> 5) & 1
    q_cat = jnp.concatenate([jnp.where(row_map == c, q_t, 0.0) for c in range(2)],
                            axis=1).astype(BF16)

    k_sq = jnp.max(kn_ref[...], axis=0)
    lane = lax.broadcasted_iota(jnp.int32, k_sq.shape, 1)
    q_sq = q_t * q_t
    bound = jnp.concatenate(
        [jnp.sqrt(jnp.sum(jnp.where(row_map == c, q_sq, 0.0), axis=0, keepdims=True)
                  * jnp.max(jnp.where(lane == 2 * head + c, k_sq, 0.0), axis=1, keepdims=True))
         for c in range(2)], axis=1) * BOUND_SLACK

    def causal(key0, s):
        key_i = key0 + lax.broadcasted_iota(jnp.int32, s.shape, 0)
        qry_i = i * tq + (lax.broadcasted_iota(jnp.int32, s.shape, 1) & (tq - 1))
        return jnp.where(key_i <= qry_i, s, -jnp.inf)

    def bounded():
        ratio = tks // tq
        diag = i // ratio

        def start(groups):
            def init():
                rows = groups * tq
                kb = k_ref[pl.ds(pl.multiple_of(diag * tks, tks), rows), :]
                s = jnp.dot(kb, q_cat, preferred_element_type=F32) - bound
                last = causal(diag * tks + rows - tq, s[rows - tq:])
                s = last if groups == 1 else jnp.concatenate([s[:rows - tq], last], axis=0)
                acc_ref[...] = jnp.dot(vt_ref[0, diag, 0, :, :rows], jnp.exp2(s).astype(BF16),
                                       preferred_element_type=F32)
            return init

        def stored_tile(jb):
            kb = k_ref[pl.ds(pl.multiple_of(jb * tks, tks), tks), :]
            s = jnp.dot(kb, q_cat, preferred_element_type=F32) - bound
            return jnp.dot(vt_ref[0, jb, 0], jnp.exp2(s).astype(BF16),
                           preferred_element_type=F32)

        def trip(n_tiles, first):
            def body(jj, carry):
                jb = first + n_tiles * jj
                total = stored_tile(jb)
                for t in range(1, n_tiles):
                    total = total + stored_tile(jb + t)
                acc_ref[...] += total
                return carry
            return body

        def choose(lo, hi):
            if hi - lo == 1:
                return start(lo + 1)
            mid = (lo + hi) // 2
            return lambda: lax.cond(i % ratio < mid, choose(lo, mid), choose(mid, hi))

        choose(0, ratio)()
        done = 0
        for n_tiles in ATT_TRIP_TILES:
            n_trips = (diag - done) // n_tiles
            lax.fori_loop(0, n_trips, trip(n_tiles, done), 0)
            done = done + n_trips * n_tiles

    def online():
        tk = tks // 2
        n_full = (i * tq) // tk

        def scores(j, s_ref):
            kb = k_ref[pl.ds(pl.multiple_of(j * tk, tk), tk), :]
            s_ref[...] = jnp.dot(kb, q_cat, preferred_element_type=F32)

        def softmax_pv(j, half, s_ref, m_old, diagonal):
            s = s_ref[...]
            if diagonal:
                s = causal(j * tk, s)
            m_new = jnp.maximum(m_old, jnp.max(s, axis=0, keepdims=True))
            alpha = jnp.exp2(m_old - m_new)
            p = jnp.exp2(s - m_new).astype(BF16)
            acc_ref[...] = alpha * acc_ref[...] + jnp.dot(
                vt_ref[0, j // 2, 0, :, half * tk:(half + 1) * tk], p, preferred_element_type=F32)
            return m_new

        acc_ref[...] = jnp.zeros_like(acc_ref)
        scores(0, sa_ref)

        def pair(jj, m_run):
            j = 2 * jj
            scores(j + 1, sb_ref)
            m_run = softmax_pv(j, 0, sa_ref, m_run, False)
            scores(j + 2, sa_ref)
            return softmax_pv(j + 1, 1, sb_ref, m_run, False)

        m_run = lax.fori_loop(0, n_full // 2, pair, jnp.full((1, 2 * tq), -jnp.inf, F32))

        def tail_odd(m_old):
            scores(n_full, sb_ref)
            m_mid = softmax_pv(n_full - 1, 0, sa_ref, m_old, False)
            return softmax_pv(n_full, 1, sb_ref, m_mid, True)

        def tail_even(m_old):
            return softmax_pv(n_full, 0, sa_ref, m_old, True)

        lax.cond(n_full % 2 == 1, tail_odd, tail_even, m_run)

    lax.cond(jnp.max(bound) <= BOUND_MAX, bounded, online)

    acc = acc_ref[...]
    inv_l = 1.0 / acc[A_V_DIM:A_V_DIM + 1]
    o = acc[:A_V_DIM, :tq] * inv_l[:, :tq] - lam_full * (acc[:A_V_DIM, tq:] * inv_l[:, tq:])
    ms = jnp.mean(o * o, axis=0, keepdims=True)
    o = o * lax.rsqrt(ms + NORM_EPS) * g_ref[...] * (1.0 - lam_init)
    o_ref[...] = o.astype(BF16)


def _attention(q, k, vt, k_norms, lam, g_col, batch, seq, lam_init):
    tq, tk = ATT_TQ, ATT_KV_TILE
    assert tk % (2 * tq) == 0 and tq & (tq - 1) == 0
    nq = seq // tq
    nk = seq // tk
    t = batch * seq
    return pl.pallas_call(
        functools.partial(_attn_kernel, lam_init=lam_init),
        grid=(batch, A_HEADS, nq),
        in_specs=[pl.BlockSpec((4, A_HEAD_DIM), lambda b, h, i: (0, 0)),
                  pl.BlockSpec((seq // KV_PROJ_TILE, 1, LANES), lambda b, h, i: (b, 0, 0)),
                  pl.BlockSpec((LANES, tq), lambda b, h, i: (h, b * nq + i)),
                  pl.BlockSpec((seq, LANES), lambda b, h, i: (b, h)),
                  pl.BlockSpec((1, nk, 1, V_ROWS, tk), lambda b, h, i: (b, 0, h, 0, 0)),
                  pl.BlockSpec((A_V_DIM, 1), lambda b, h, i: (h, 0))],
        out_specs=pl.BlockSpec((A_V_DIM, tq), lambda b, h, i: (h, b * nq + i)),
        out_shape=jax.ShapeDtypeStruct((D_MODEL, t), BF16),
        scratch_shapes=[pltpu.VMEM((tk // 2, 2 * tq), F32),
                        pltpu.VMEM((tk // 2, 2 * tq), F32),
                        pltpu.VMEM((V_ROWS, 2 * tq), F32)],
        compiler_params=_params("parallel", "parallel", "arbitrary"),
        name="diff_attention",
    )(lam, k_norms, q, k, vt, g_col)


def _head_perm(w):
    half = A_HEAD_DIM // 2
    w = w.reshape(w.shape[0], A_HEADS, 2, 2, half)
    return w.transpose(0, 1, 3, 2, 4).reshape(w.shape[0], D_MODEL)


def kernel(x, positions, norm_g, a_w_in, a_b_gates, a_g_head, a_w_out, kv_norm_g, w_kv,
           b_w_q, b_lam, b_g_head, b_w_out, mlp_w_up, mlp_w_down):
    batch, seq, _ = x.shape
    t = batch * seq
    depth = norm_g.shape[0]
    n_a = a_w_in.shape[0]
    xf = x.reshape(t, D_MODEL)
    row = lambda v: v.reshape(1, -1).astype(F32)

    cos, sin, cos_t, sin_t = _rope_tables(positions.reshape(1, t))
    k_sh = vt_sh = k_norms = None
    for layer in range(depth):
        g = norm_g[layer]
        if layer < n_a:
            w_in = a_w_in[layer]
            vo_end = 2 * M_QK_COLS + 2 * D_MODEL
            w_voq = jnp.concatenate([w_in[:, 2 * M_QK_COLS:vo_end], w_in[:, :M_QK_COLS]],
                                    axis=1).astype(BF16)
            w_kt = (w_in[:, M_QK_COLS:2 * M_QK_COLS] * M_QK_DIM ** -0.5).T.astype(BF16)
            w_gt = w_in[:, vo_end:].T.astype(BF16)
            voq, kt, gates = _a_proj(xf, row(g[0]), w_voq, w_kt, w_gt,
                                     a_b_gates[layer].reshape(-1, 1).astype(F32))
            y = _mlstm(voq, kt, gates, row(a_g_head[layer]), batch, seq)
            w_out = a_w_out[layer]
        else:
            j = layer - n_a
            if j == 0:
                w_k = _head_perm(w_kv[:, :D_MODEL]).astype(BF16)
                w_vt = w_kv[:, D_MODEL:].T.astype(BF16)
                k_sh, vt_sh, k_norms = _kv_proj(xf, row(kv_norm_g), w_k, w_vt, cos, sin,
                                                batch, seq)
            lam_init = 0.8 - 0.6 * math.exp(-0.3 * layer)
            w_qt = (_head_perm(b_w_q[j]) * (A_HEAD_DIM ** -0.5 * math.log2(math.e))
                    ).T.astype(BF16)
            q_t = _q_proj(xf, row(g[0]), w_qt, cos_t, sin_t)
            y = _attention(q_t, k_sh, vt_sh, k_norms, b_lam[j].astype(F32),
                           b_g_head[j].reshape(-1, 1).astype(F32), batch, seq, lam_init)
            w_out = b_w_out[j]
        xf = _post(xf, y, w_out.astype(BF16), row(g[1]), row(g[2]), row(g[3]),
                   mlp_w_up[layer].astype(BF16), mlp_w_down[layer].astype(BF16),
                   y_feature_major=layer >= n_a)
    return xf.reshape(batch, seq, D_MODEL)
```

```python
import functools
import math

import jax
import jax.numpy as jnp
from jax import lax
from jax.experimental import pallas as pl
from jax.experimental.pallas import tpu as pltpu

F32 = jnp.float32
BF16 = jnp.bfloat16

D_MODEL = 1024
D_FF = 4 * D_MODEL
NORM_EPS = 1e-6

M_HEADS = 4
M_V_DIM = D_MODEL // M_HEADS
M_QK_DIM = M_V_DIM // 2
M_QK_COLS = M_HEADS * M_QK_DIM
GATE_SOFTCAP = 15.0
M_CHUNK = 512

A_HEADS = 8
A_HEAD_DIM = D_MODEL // (2 * A_HEADS)
A_V_DIM = 2 * A_HEAD_DIM
ROPE_THETA = 10000.0

LANES = 128
VMEM_LIMIT_BYTES = 56 * 1024 * 1024

ROW_TILE = 512
ATT_TQ = 512
ATT_KV_TILE = 4 * ATT_TQ
PROJ_TILE = 1024
KV_PROJ_TILE = 1024
ATT_TRIP_TILES = (4, 2, 1)
BOUND_MAX = 40.0
BOUND_SLACK = 1.03
V_ROWS = A_V_DIM + 16

_NN = (((1,), (0,)), ((), ()))
_NT = (((1,), (1,)), ((), ()))
_TN = (((0,), (0,)), ((), ()))


def _params(*sem):
    return pltpu.CompilerParams(dimension_semantics=sem, vmem_limit_bytes=VMEM_LIMIT_BYTES)


def _rms(x, g):
    ms = jnp.mean(x * x, axis=-1, keepdims=True)
    return x * lax.rsqrt(ms + NORM_EPS) * g


def _const_spec(shape):
    nd = len(shape)
    return pl.BlockSpec(shape, lambda *_: (0,) * nd, pipeline_mode=pl.Buffered(1))


def _rope_table_kernel(pos_ref, inv_ref, sign_ref, cos_ref, sin_ref, cost_ref, sint_ref):
    ang = inv_ref[...] * pos_ref[...].astype(F32)
    cos32 = jnp.cos(ang)
    sin32 = jnp.sin(ang)
    cost_ref[...] = jnp.concatenate([cos32, cos32], axis=0)
    sint_ref[...] = jnp.concatenate([sin32, sin32], axis=0)
    cos_ref[...] = jnp.concatenate([cos32] * 4, axis=0).T
    sin_ref[...] = jnp.concatenate([sin32] * 4, axis=0).T * sign_ref[...]


def _rope_tables(pos_row):
    t = pos_row.shape[1]
    half = A_HEAD_DIM // 2
    inv = 1.0 / (ROPE_THETA ** (jnp.arange(0, A_HEAD_DIM, 2, dtype=F32) / A_HEAD_DIM))
    sign = jnp.where(jnp.arange(LANES) < LANES // 2, -1.0, 1.0).astype(F32).reshape(1, LANES)
    tm = ROW_TILE
    return pl.pallas_call(
        _rope_table_kernel,
        grid=(t // tm,),
        in_specs=[pl.BlockSpec((1, tm), lambda i: (0, i)),
                  _const_spec((half, 1)), _const_spec((1, LANES))],
        out_specs=[pl.BlockSpec((tm, LANES), lambda i: (i, 0))] * 2
        + [pl.BlockSpec((A_HEAD_DIM, tm), lambda i: (0, i))] * 2,
        out_shape=[jax.ShapeDtypeStruct((t, LANES), F32)] * 2
        + [jax.ShapeDtypeStruct((A_HEAD_DIM, t), F32)] * 2,
        compiler_params=_params("parallel"),
        name="rope_tables",
    )(pos_row, inv.reshape(half, 1), sign)


def _rope(t, cos, sin):
    return t * cos + pltpu.roll(t, LANES // 2, axis=1) * sin


def _a_proj_kernel(x_ref, g_ref, w_ref, wkt_ref, wgt_ref, bg_ref, voq_ref, kt_ref, gates_ref):
    hb = _rms(x_ref[...], g_ref[...]).astype(BF16)
    n_out = voq_ref.shape[1]
    for n in range(0, n_out, 512):
        voq_ref[:, n:n + 512] = jnp.dot(
            hb, w_ref[:, n:n + 512], preferred_element_type=F32).astype(BF16)
    kt_ref[...] = lax.dot_general(wkt_ref[...], hb, _NT,
                                  preferred_element_type=F32).astype(BF16)
    gr = lax.dot_general(wgt_ref[...], hb, _NT, preferred_element_type=F32) + bg_ref[...]
    gr = GATE_SOFTCAP * jnp.tanh(gr / GATE_SOFTCAP)
    log_sig = jnp.minimum(gr, 0.0) - jnp.log1p(jnp.exp(-jnp.abs(gr)))
    row = lax.broadcasted_iota(jnp.int32, gr.shape, 0)
    gates_ref[...] = jnp.where(row < M_HEADS, gr, log_sig)


def _a_proj(x, g, w_voq, w_kt, w_gt, b_g):
    t = x.shape[0]
    tm = PROJ_TILE
    n_out = w_voq.shape[1]
    return pl.pallas_call(
        _a_proj_kernel,
        grid=(t // tm,),
        in_specs=[pl.BlockSpec((tm, D_MODEL), lambda i: (i, 0)),
                  _const_spec((1, D_MODEL)),
                  _const_spec((D_MODEL, n_out)),
                  _const_spec((M_QK_COLS, D_MODEL)),
                  _const_spec((2 * M_HEADS, D_MODEL)),
                  _const_spec((2 * M_HEADS, 1))],
        out_specs=[pl.BlockSpec((tm, n_out), lambda i: (i, 0)),
                   pl.BlockSpec((M_QK_COLS, tm), lambda i: (0, i)),
                   pl.BlockSpec((2 * M_HEADS, tm), lambda i: (0, i))],
        out_shape=[jax.ShapeDtypeStruct((t, n_out), BF16),
                   jax.ShapeDtypeStruct((M_QK_COLS, t), BF16),
                   jax.ShapeDtypeStruct((2 * M_HEADS, t), F32)],
        compiler_params=_params("parallel"),
        name="mlstm_proj",
    )(x, g, w_voq, w_kt, w_gt, b_g)


def _mlstm_kernel(q_ref, kt_ref, v_ref, o_ref, g_ref, gh_ref, y_ref, c_ref, m_ref):
    L = q_ref.shape[0]

    @pl.when(pl.program_id(1) == 0)
    def _():
        c_ref[...] = jnp.zeros_like(c_ref)
        m_ref[...] = jnp.zeros_like(m_ref)

    jj = lax.broadcasted_iota(jnp.int32, (L, L), 0)
    ss = lax.broadcasted_iota(jnp.int32, (L, L), 1)
    causal = ss <= jj
    upper = (jj <= ss).astype(F32)

    gates = g_ref[...]
    cums = jnp.dot(gates, upper, preferred_element_type=F32, precision=lax.Precision.HIGHEST)
    b8 = pltpu.roll(cums, M_HEADS, axis=0)
    r8 = gates - b8
    lane = lax.broadcasted_iota(jnp.int32, r8.shape, 1)
    cm8 = r8
    shift = 1
    while shift < L:
        cm8 = jnp.maximum(cm8, jnp.where(lane >= shift, pltpu.roll(cm8, shift, axis=1), -jnp.inf))
        shift *= 2
    stack = jnp.concatenate([cm8, b8, jnp.zeros((LANES - 16, L), F32)], axis=0)
    cols = stack.T
    lane1 = lax.broadcasted_iota(jnp.int32, (1, LANES), 1)
    m_prev = [m_ref[h] for h in range(M_HEADS)]
    m_lanes = sum(jnp.where(lane1 == h, m_prev[h], 0.0) for h in range(M_HEADS))
    m_cols = jnp.maximum(cols, m_lanes)
    b_cols = pltpu.roll(cols, LANES - 8, axis=1)
    is_head = lane1 < M_HEADS
    s_inter = jnp.where(is_head, jnp.exp(m_lanes - m_cols), 0.0).astype(BF16)
    floor = jnp.where(is_head, jnp.exp(-(b_cols + m_cols)), 0.0).astype(BF16)
    spread = (lax.broadcasted_iota(jnp.int32, (LANES, M_HEADS * LANES), 0)
              == lax.broadcasted_iota(jnp.int32, (LANES, M_HEADS * LANES), 1) // LANES
              ).astype(BF16)
    s_inter = jnp.dot(s_inter, spread, preferred_element_type=F32)
    floor = jnp.dot(floor, spread, preferred_element_type=F32)
    ones_blk = jnp.ones((L, LANES), BF16)
    mean_blk = jnp.full((M_V_DIM, LANES), 1.0 / M_V_DIM, BF16)

    for h in range(M_HEADS):
        lanes_h = slice(h * LANES, (h + 1) * LANES)
        q = q_ref[:, lanes_h]
        kt = kt_ref[lanes_h, :]
        v_ext = jnp.concatenate([v_ref[:, h * M_V_DIM:(h + 1) * M_V_DIM], ones_blk], axis=1)
        c_prev = c_ref[h]
        r_row = r8[h:h + 1]

        w_intra = jnp.exp(jnp.where(causal, r_row, -jnp.inf) - m_cols[:, h:h + 1])
        s_qk = jnp.dot(q, kt, preferred_element_type=F32) * w_intra
        s_in = s_inter[:, lanes_h]
        ext = (jnp.concatenate([s_in, s_in, s_in], axis=1)
               * jnp.dot(q, c_prev.astype(BF16), preferred_element_type=F32)
               + jnp.dot(s_qk.astype(BF16), v_ext, preferred_element_type=F32))
        den = ext[:, M_V_DIM:]
        inv = 1.0 / jnp.maximum(jnp.abs(den), floor[:, lanes_h])
        h_out = ext[:, :M_V_DIM] * jnp.concatenate([inv, inv], axis=1)

        m_last = jnp.maximum(cm8[h:h + 1, L - 1:L], m_prev[h])
        decay = jnp.exp(m_prev[h] - m_last)
        kwt = (kt.astype(F32) * jnp.exp(r_row - m_last)).astype(BF16)
        c_ref[h] = decay * c_prev + jnp.dot(kwt, v_ext, preferred_element_type=F32)
        m_ref[h] = b8[h:h + 1, L - 1:L] + m_last

        sl = slice(h * M_V_DIM, (h + 1) * M_V_DIM)
        ms = jnp.dot((h_out * h_out).astype(BF16), mean_blk, preferred_element_type=F32)
        rs = lax.rsqrt(ms + NORM_EPS)
        og = jax.nn.sigmoid(o_ref[:, sl].astype(F32))
        y_ref[:, sl] = (og * h_out * jnp.concatenate([rs, rs], axis=1)
                        * gh_ref[:, sl]).astype(BF16)


def _mlstm(voq, kt, gates, g_head, batch, seq):
    L = M_CHUNK
    nc = seq // L
    t = batch * seq
    row = lambda b, c: b * nc + c
    return pl.pallas_call(
        _mlstm_kernel,
        grid=(batch, nc),
        in_specs=[pl.BlockSpec((L, M_QK_COLS), lambda b, c: (row(b, c), 2 * D_MODEL // M_QK_COLS)),
                  pl.BlockSpec((M_QK_COLS, L), lambda b, c: (0, row(b, c))),
                  pl.BlockSpec((L, D_MODEL), lambda b, c: (row(b, c), 0)),
                  pl.BlockSpec((L, D_MODEL), lambda b, c: (row(b, c), 1)),
                  pl.BlockSpec((2 * M_HEADS, L), lambda b, c: (0, row(b, c))),
                  pl.BlockSpec((1, D_MODEL), lambda b, c: (0, 0))],
        out_specs=pl.BlockSpec((L, D_MODEL), lambda b, c: (row(b, c), 0)),
        out_shape=jax.ShapeDtypeStruct((t, D_MODEL), BF16),
        scratch_shapes=[pltpu.VMEM((M_HEADS, M_QK_DIM, M_V_DIM + LANES), F32),
                        pltpu.VMEM((M_HEADS, 1, 1), F32)],
        compiler_params=_params("parallel", "arbitrary"),
        name="mlstm_mixer",
    )(voq, kt, voq, voq, gates, g_head)


def _post_kernel(x_ref, y_ref, wo_ref, g1_ref, g2_ref, g3_ref, wu_ref, wd_ref, out_ref, *,
                 y_feature_major):
    z = lax.dot_general(y_ref[...], wo_ref[...], _TN if y_feature_major else _NN,
                        preferred_element_type=F32)
    x1 = x_ref[...] + _rms(z, g1_ref[...])
    hb = _rms(x1, g2_ref[...]).astype(BF16)
    acc = jnp.zeros(x1.shape, F32)
    for f in range(0, D_FF, 1024):
        u = jnp.maximum(jnp.dot(hb, wu_ref[:, f:f + 1024], preferred_element_type=F32), 0.0)
        acc = acc + jnp.dot((u * u).astype(BF16), wd_ref[f:f + 1024, :],
                            preferred_element_type=F32)
    out_ref[...] = x1 + _rms(acc, g3_ref[...])


def _post(x, y, w_out, g1, g2, g3, w_up, w_down, y_feature_major):
    t = x.shape[0]
    tm = ROW_TILE
    y_spec = (pl.BlockSpec((D_MODEL, tm), lambda i: (0, i)) if y_feature_major
              else pl.BlockSpec((tm, D_MODEL), lambda i: (i, 0)))
    return pl.pallas_call(
        functools.partial(_post_kernel, y_feature_major=y_feature_major),
        grid=(t // tm,),
        in_specs=[pl.BlockSpec((tm, D_MODEL), lambda i: (i, 0)),
                  y_spec,
                  _const_spec((D_MODEL, D_MODEL)),
                  _const_spec((1, D_MODEL)), _const_spec((1, D_MODEL)), _const_spec((1, D_MODEL)),
                  _const_spec((D_MODEL, D_FF)),
                  _const_spec((D_FF, D_MODEL))],
        out_specs=pl.BlockSpec((tm, D_MODEL), lambda i: (i, 0)),
        out_shape=jax.ShapeDtypeStruct((t, D_MODEL), F32),
        compiler_params=_params("parallel"),
        name="outproj_mlp",
    )(x, y, w_out, g1, g2, g3, w_up, w_down)


def _kv_proj_kernel(x_ref, g_ref, wk_ref, wvt_ref, cos_ref, sin_ref, k_ref, vt_ref, kn_ref):
    hb = _rms(x_ref[...], g_ref[...]).astype(BF16)
    cos = cos_ref[...]
    sin = sin_ref[...]
    kk = jnp.dot(hb, wk_ref[...], preferred_element_type=F32)
    col = lax.broadcasted_iota(jnp.int32, (D_MODEL, LANES), 0)
    out = lax.broadcasted_iota(jnp.int32, (D_MODEL, LANES), 1)
    pick = (out == 2 * (col >> 7) + ((col >> 5) & 1)).astype(BF16)
    k_sq = jnp.dot((kk * kk).astype(BF16), pick, preferred_element_type=F32)
    kn_ref[0] = jnp.max(k_sq, axis=0, keepdims=True)
    for h in range(A_HEADS):
        sl = slice(h * LANES, (h + 1) * LANES)
        k_ref[:, sl] = _rope(kk[:, sl], cos, sin).astype(BF16)
    vt = lax.dot_general(wvt_ref[...], hb, _NT, preferred_element_type=F32)
    ones = jnp.ones((V_ROWS - A_V_DIM, vt.shape[1]), BF16)
    for h in range(A_HEADS):
        vt_ref[0, 0, h, :A_V_DIM, :] = vt[h * A_V_DIM:(h + 1) * A_V_DIM].astype(BF16)
        vt_ref[0, 0, h, A_V_DIM:, :] = ones


def _kv_proj(x, g, w_k, w_vt, cos, sin, batch, seq):
    t = x.shape[0]
    tm = KV_PROJ_TILE
    per_stored = ATT_KV_TILE // tm
    per_batch = seq // tm
    return pl.pallas_call(
        _kv_proj_kernel,
        grid=(t // tm,),
        in_specs=[pl.BlockSpec((tm, D_MODEL), lambda i: (i, 0)),
                  _const_spec((1, D_MODEL)),
                  _const_spec((D_MODEL, D_MODEL)),
                  _const_spec((D_MODEL, D_MODEL)),
                  pl.BlockSpec((tm, LANES), lambda i: (i, 0)),
                  pl.BlockSpec((tm, LANES), lambda i: (i, 0))],
        out_specs=[pl.BlockSpec((tm, D_MODEL), lambda i: (i, 0)),
                   pl.BlockSpec((1, 1, A_HEADS, V_ROWS, tm),
                                lambda i: (i // per_batch, (i % per_batch) // per_stored, 0, 0,
                                           i % per_stored)),
                   pl.BlockSpec((1, 1, LANES), lambda i: (i, 0, 0))],
        out_shape=[jax.ShapeDtypeStruct((t, D_MODEL), BF16),
                   jax.ShapeDtypeStruct((batch, seq // ATT_KV_TILE, A_HEADS, V_ROWS, ATT_KV_TILE),
                                        BF16),
                   jax.ShapeDtypeStruct((t // tm, 1, LANES), F32)],
        compiler_params=_params("parallel"),
        name="kv_proj",
    )(x, g, w_k, w_vt, cos, sin)


def _q_proj_kernel(x_ref, g_ref, wqt_ref, cos_ref, sin_ref, qt_ref):
    hb = _rms(x_ref[...], g_ref[...]).astype(BF16)
    cos = cos_ref[...]
    sin = sin_ref[...]
    qq = lax.dot_general(wqt_ref[...], hb, _NT, preferred_element_type=F32)
    half = LANES // 2
    for h in range(A_HEADS):
        lo = qq[h * LANES:h * LANES + half]
        hi = qq[h * LANES + half:(h + 1) * LANES]
        qt_ref[h * LANES:h * LANES + half, :] = (lo * cos - hi * sin).astype(BF16)
        qt_ref[h * LANES + half:(h + 1) * LANES, :] = (hi * cos + lo * sin).astype(BF16)


def _q_proj(x, g, w_qt, cos_t, sin_t):
    t = x.shape[0]
    tm = PROJ_TILE
    return pl.pallas_call(
        _q_proj_kernel,
        grid=(t // tm,),
        in_specs=[pl.BlockSpec((tm, D_MODEL), lambda i: (i, 0)),
                  _const_spec((1, D_MODEL)),
                  _const_spec((D_MODEL, D_MODEL)),
                  pl.BlockSpec((A_HEAD_DIM, tm), lambda i: (0, i)),
                  pl.BlockSpec((A_HEAD_DIM, tm), lambda i: (0, i))],
        out_specs=pl.BlockSpec((D_MODEL, tm), lambda i: (0, i)),
        out_shape=jax.ShapeDtypeStruct((D_MODEL, t), BF16),
        compiler_params=_params("parallel"),
        name="q_proj",
    )(x, g, w_qt, cos_t, sin_t)


def _attn_kernel(lam_ref, kn_ref, q_ref, k_ref, vt_ref, g_ref, o_ref, sa_ref, sb_ref, acc_ref, *,
                 lam_init):
    tq = q_ref.shape[1]
    tks = vt_ref.shape[4]
    head = pl.program_id(1)
    i = pl.program_id(2)

    lam = lam_ref[...]
    lam_full = (jnp.exp(jnp.sum(lam[0:1] * lam[1:2], keepdims=True))
                - jnp.exp(jnp.sum(lam[2:3] * lam[3:4], keepdims=True)) + lam_init)

    q_t = q_ref[...].astype(F32)
    row_map = (lax.broadcasted_iota(jnp.int32, q_t.shape, 0) >> 5) & 1
    q_cat = jnp.concatenate([jnp.where(row_map == c, q_t, 0.0) for c in range(2)],
                            axis=1).astype(BF16)

    k_sq = jnp.max(kn_ref[...], axis=0)
    lane = lax.broadcasted_iota(jnp.int32, k_sq.shape, 1)
    q_sq = q_t * q_t
    bound = jnp.concatenate(
        [jnp.sqrt(jnp.sum(jnp.where(row_map == c, q_sq, 0.0), axis=0, keepdims=True)
                  * jnp.max(jnp.where(lane == 2 * head + c, k_sq, 0.0), axis=1, keepdims=True))
         for c in range(2)], axis=1) * BOUND_SLACK

    def causal(key0, s):
        key_i = key0 + lax.broadcasted_iota(jnp.int32, s.shape, 0)
        qry_i = i * tq + (lax.broadcasted_iota(jnp.int32, s.shape, 1) & (tq - 1))
        return jnp.where(key_i <= qry_i, s, -jnp.inf)

    def bounded():
        ratio = tks // tq
        diag = i // ratio

        def start(groups):
            def init():
                rows = groups * tq
                kb = k_ref[pl.ds(pl.multiple_of(diag * tks, tks), rows), :]
                s = jnp.dot(kb, q_cat, preferred_element_type=F32) - bound
                last = causal(diag * tks + rows - tq, s[rows - tq:])
                s = last if groups == 1 else jnp.concatenate([s[:rows - tq], last], axis=0)
                acc_ref[...] = jnp.dot(vt_ref[0, diag, 0, :, :rows], jnp.exp2(s).astype(BF16),
                                       preferred_element_type=F32)
            return init

        def stored_tile(jb):
            kb = k_ref[pl.ds(pl.multiple_of(jb * tks, tks), tks), :]
            s = jnp.dot(kb, q_cat, preferred_element_type=F32) - bound
            return jnp.dot(vt_ref[0, jb, 0], jnp.exp2(s).astype(BF16),
                           preferred_element_type=F32)

        def trip(n_tiles, first):
            def body(jj, carry):
                jb = first + n_tiles * jj
                total = stored_tile(jb)
                for t in range(1, n_tiles):
                    total = total + stored_tile(jb + t)
                acc_ref[...] += total
                return carry
            return body

        def choose(lo, hi):
            if hi - lo == 1:
                return start(lo + 1)
            mid = (lo + hi) // 2
            return lambda: lax.cond(i % ratio < mid, choose(lo, mid), choose(mid, hi))

        choose(0, ratio)()
        done = 0
        for n_tiles in ATT_TRIP_TILES:
            n_trips = (diag - done) // n_tiles
            lax.fori_loop(0, n_trips, trip(n_tiles, done), 0)
            done = done + n_trips * n_tiles

    def online():
        tk = tks // 2
        n_full = (i * tq) // tk

        def scores(j, s_ref):
            kb = k_ref[pl.ds(pl.multiple_of(j * tk, tk), tk), :]
            s_ref[...] = jnp.dot(kb, q_cat, preferred_element_type=F32)

        def softmax_pv(j, half, s_ref, m_old, diagonal):
            s = s_ref[...]
            if diagonal:
                s = causal(j * tk, s)
            m_new = jnp.maximum(m_old, jnp.max(s, axis=0, keepdims=True))
            alpha = jnp.exp2(m_old - m_new)
            p = jnp.exp2(s - m_new).astype(BF16)
            acc_ref[...] = alpha * acc_ref[...] + jnp.dot(
                vt_ref[0, j // 2, 0, :, half * tk:(half + 1) * tk], p, preferred_element_type=F32)
            return m_new

        acc_ref[...] = jnp.zeros_like(acc_ref)
        scores(0, sa_ref)

        def pair(jj, m_run):
            j = 2 * jj
            scores(j + 1, sb_ref)
            m_run = softmax_pv(j, 0, sa_ref, m_run, False)
            scores(j + 2, sa_ref)
            return softmax_pv(j + 1, 1, sb_ref, m_run, False)

        m_run = lax.fori_loop(0, n_full // 2, pair, jnp.full((1, 2 * tq), -jnp.inf, F32))

        def tail_odd(m_old):
            scores(n_full, sb_ref)
            m_mid = softmax_pv(n_full - 1, 0, sa_ref, m_old, False)
            return softmax_pv(n_full, 1, sb_ref, m_mid, True)

        def tail_even(m_old):
            return softmax_pv(n_full, 0, sa_ref, m_old, True)

        lax.cond(n_full % 2 == 1, tail_odd, tail_even, m_run)

    lax.cond(jnp.max(bound) <= BOUND_MAX, bounded, online)

    acc = acc_ref[...]
    inv_l = 1.0 / acc[A_V_DIM:A_V_DIM + 1]
    o = acc[:A_V_DIM, :tq] * inv_l[:, :tq] - lam_full * (acc[:A_V_DIM, tq:] * inv_l[:, tq:])
    ms = jnp.mean(o * o, axis=0, keepdims=True)
    o = o * lax.rsqrt(ms + NORM_EPS) * g_ref[...] * (1.0 - lam_init)
    o_ref[...] = o.astype(BF16)


def _attention(q, k, vt, k_norms, lam, g_col, batch, seq, lam_init):
    tq, tk = ATT_TQ, ATT_KV_TILE
    assert tk % (2 * tq) == 0 and tq & (tq - 1) == 0
    nq = seq // tq
    nk = seq // tk
    t = batch * seq
    return pl.pallas_call(
        functools.partial(_attn_kernel, lam_init=lam_init),
        grid=(batch, A_HEADS, nq),
        in_specs=[pl.BlockSpec((4, A_HEAD_DIM), lambda b, h, i: (0, 0)),
                  pl.BlockSpec((seq // KV_PROJ_TILE, 1, LANES), lambda b, h, i: (b, 0, 0)),
                  pl.BlockSpec((LANES, tq), lambda b, h, i: (h, b * nq + i)),
                  pl.BlockSpec((seq, LANES), lambda b, h, i: (b, h)),
                  pl.BlockSpec((1, nk, 1, V_ROWS, tk), lambda b, h, i: (b, 0, h, 0, 0)),
                  pl.BlockSpec((A_V_DIM, 1), lambda b, h, i: (h, 0))],
        out_specs=pl.BlockSpec((A_V_DIM, tq), lambda b, h, i: (h, b * nq + i)),
        out_shape=jax.ShapeDtypeStruct((D_MODEL, t), BF16),
        scratch_shapes=[pltpu.VMEM((tk // 2, 2 * tq), F32),
                        pltpu.VMEM((tk // 2, 2 * tq), F32),
                        pltpu.VMEM((V_ROWS, 2 * tq), F32)],
        compiler_params=_params("parallel", "parallel", "arbitrary"),
        name="diff_attention",
    )(lam, k_norms, q, k, vt, g_col)


def _head_perm(w):
    half = A_HEAD_DIM // 2
    w = w.reshape(w.shape[0], A_HEADS, 2, 2, half)
    return w.transpose(0, 1, 3, 2, 4).reshape(w.shape[0], D_MODEL)


def kernel(x, positions, norm_g, a_w_in, a_b_gates, a_g_head, a_w_out, kv_norm_g, w_kv,
           b_w_q, b_lam, b_g_head, b_w_out, mlp_w_up, mlp_w_down):
    batch, seq, _ = x.shape
    t = batch * seq
    depth = norm_g.shape[0]
    n_a = a_w_in.shape[0]
    xf = x.reshape(t, D_MODEL)
    row = lambda v: v.reshape(1, -1).astype(F32)

    cos, sin, cos_t, sin_t = _rope_tables(positions.reshape(1, t))
    k_sh = vt_sh = k_norms = None
    for layer in range(depth):
        g = norm_g[layer]
        if layer < n_a:
            w_in = a_w_in[layer]
            vo_end = 2 * M_QK_COLS + 2 * D_MODEL
            w_voq = jnp.concatenate([w_in[:, 2 * M_QK_COLS:vo_end], w_in[:, :M_QK_COLS]],
                                    axis=1).astype(BF16)
            w_kt = (w_in[:, M_QK_COLS:2 * M_QK_COLS] * M_QK_DIM ** -0.5).T.astype(BF16)
            w_gt = w_in[:, vo_end:].T.astype(BF16)
            voq, kt, gates = _a_proj(xf, row(g[0]), w_voq, w_kt, w_gt,
                                     a_b_gates[layer].reshape(-1, 1).astype(F32))
            y = _mlstm(voq, kt, gates, row(a_g_head[layer]), batch, seq)
            w_out = a_w_out[layer]
        else:
            j = layer - n_a
            if j == 0:
                w_k = _head_perm(w_kv[:, :D_MODEL]).astype(BF16)
                w_vt = w_kv[:, D_MODEL:].T.astype(BF16)
                k_sh, vt_sh, k_norms = _kv_proj(xf, row(kv_norm_g), w_k, w_vt, cos, sin,
                                                batch, seq)
            lam_init = 0.8 - 0.6 * math.exp(-0.3 * layer)
            w_qt = (_head_perm(b_w_q[j]) * (A_HEAD_DIM ** -0.5 * math.log2(math.e))
                    ).T.astype(BF16)
            q_t = _q_proj(xf, row(g[0]), w_qt, cos_t, sin_t)
            y = _attention(q_t, k_sh, vt_sh, k_norms, b_lam[j].astype(F32),
                           b_g_head[j].reshape(-1, 1).astype(F32), batch, seq, lam_init)
            w_out = b_w_out[j]
        xf = _post(xf, y, w_out.astype(BF16), row(g[1]), row(g[2]), row(g[3]),
                   mlp_w_up[layer].astype(BF16), mlp_w_down[layer].astype(BF16),
                   y_feature_major=layer >= n_a)
    return xf.reshape(batch, seq, D_MODEL)
```

```python
import functools
import math

import jax
import jax.numpy as jnp
from jax import lax
from jax.experimental import pallas as pl
from jax.experimental.pallas import tpu as pltpu

F32 = jnp.float32
BF16 = jnp.bfloat16

D_MODEL = 1024
D_FF = 4 * D_MODEL
NORM_EPS = 1e-6

M_HEADS = 4
M_V_DIM = D_MODEL // M_HEADS
M_QK_DIM = M_V_DIM // 2
M_QK_COLS = M_HEADS * M_QK_DIM
GATE_SOFTCAP = 15.0
M_CHUNK = 512

A_HEADS = 8
A_HEAD_DIM = D_MODEL // (2 * A_HEADS)
A_V_DIM = 2 * A_HEAD_DIM
ROPE_THETA = 10000.0

LANES = 128
VMEM_LIMIT_BYTES = 56 * 1024 * 1024

ROW_TILE = 512
ATT_TQ = 512
ATT_KV_TILE = 4 * ATT_TQ
PROJ_TILE = 1024
KV_PROJ_TILE = 1024
ATT_TRIP_TILES = (4, 2, 1)
BOUND_MAX = 40.0
BOUND_SLACK = 1.03
V_ROWS = A_V_DIM + 16

_NN = (((1,), (0,)), ((), ()))
_NT = (((1,), (1,)), ((), ()))
_TN = (((0,), (0,)), ((), ()))


def _params(*sem):
    return pltpu.CompilerParams(dimension_semantics=sem, vmem_limit_bytes=VMEM_LIMIT_BYTES)


def _rms(x, g):
    ms = jnp.mean(x * x, axis=-1, keepdims=True)
    return x * lax.rsqrt(ms + NORM_EPS) * g


def _const_spec(shape):
    nd = len(shape)
    return pl.BlockSpec(shape, lambda *_: (0,) * nd, pipeline_mode=pl.Buffered(1))


def _rope_table_kernel(pos_ref, inv_ref, sign_ref, cos_ref, sin_ref, cost_ref, sint_ref):
    ang = inv_ref[...] * pos_ref[...].astype(F32)
    cos32 = jnp.cos(ang)
    sin32 = jnp.sin(ang)
    cost_ref[...] = jnp.concatenate([cos32, cos32], axis=0)
    sint_ref[...] = jnp.concatenate([sin32, sin32], axis=0)
    cos_ref[...] = jnp.concatenate([cos32] * 4, axis=0).T
    sin_ref[...] = jnp.concatenate([sin32] * 4, axis=0).T * sign_ref[...]


def _rope_tables(pos_row):
    t = pos_row.shape[1]
    half = A_HEAD_DIM // 2
    inv = 1.0 / (ROPE_THETA ** (jnp.arange(0, A_HEAD_DIM, 2, dtype=F32) / A_HEAD_DIM))
    sign = jnp.where(jnp.arange(LANES) < LANES // 2, -1.0, 1.0).astype(F32).reshape(1, LANES)
    tm = ROW_TILE
    return pl.pallas_call(
        _rope_table_kernel,
        grid=(t // tm,),
        in_specs=[pl.BlockSpec((1, tm), lambda i: (0, i)),
                  _const_spec((half, 1)), _const_spec((1, LANES))],
        out_specs=[pl.BlockSpec((tm, LANES), lambda i: (i, 0))] * 2
        + [pl.BlockSpec((A_HEAD_DIM, tm), lambda i: (0, i))] * 2,
        out_shape=[jax.ShapeDtypeStruct((t, LANES), F32)] * 2
        + [jax.ShapeDtypeStruct((A_HEAD_DIM, t), F32)] * 2,
        compiler_params=_params("parallel"),
        name="rope_tables",
    )(pos_row, inv.reshape(half, 1), sign)


def _rope(t, cos, sin):
    return t * cos + pltpu.roll(t, LANES // 2, axis=1) * sin


def _a_proj_kernel(x_ref, g_ref, w_ref, wkt_ref, wgt_ref, bg_ref, voq_ref, kt_ref, gates_ref):
    hb = _rms(x_ref[...], g_ref[...]).astype(BF16)
    vo_cols = 2 * D_MODEL
    for n in range(0, voq_ref.shape[1], 512):
        src = n + 2 * M_QK_COLS if n < vo_cols else n - vo_cols
        voq_ref[:, n:n + 512] = jnp.dot(
            hb, w_ref[:, src:src + 512], preferred_element_type=F32).astype(BF16)
    kt_ref[...] = lax.dot_general(wkt_ref[...], hb, _NT,
                                  preferred_element_type=F32).astype(BF16)
    gr = lax.dot_general(wgt_ref[...], hb, _NT, preferred_element_type=F32) + bg_ref[...]
    gr = GATE_SOFTCAP * jnp.tanh(gr / GATE_SOFTCAP)
    log_sig = jnp.minimum(gr, 0.0) - jnp.log1p(jnp.exp(-jnp.abs(gr)))
    row = lax.broadcasted_iota(jnp.int32, gr.shape, 0)
    gates_ref[...] = jnp.where(row < M_HEADS, gr, log_sig)


def _a_proj(x, g, w_all, w_kt, w_gt, b_g):
    t = x.shape[0]
    tm = PROJ_TILE
    n_out = 2 * D_MODEL + M_QK_COLS
    return pl.pallas_call(
        _a_proj_kernel,
        grid=(t // tm,),
        in_specs=[pl.BlockSpec((tm, D_MODEL), lambda i: (i, 0)),
                  _const_spec((1, D_MODEL)),
                  _const_spec(w_all.shape),
                  _const_spec((M_QK_COLS, D_MODEL)),
                  _const_spec((2 * M_HEADS, D_MODEL)),
                  _const_spec((2 * M_HEADS, 1))],
        out_specs=[pl.BlockSpec((tm, n_out), lambda i: (i, 0)),
                   pl.BlockSpec((M_QK_COLS, tm), lambda i: (0, i)),
                   pl.BlockSpec((2 * M_HEADS, tm), lambda i: (0, i))],
        out_shape=[jax.ShapeDtypeStruct((t, n_out), BF16),
                   jax.ShapeDtypeStruct((M_QK_COLS, t), BF16),
                   jax.ShapeDtypeStruct((2 * M_HEADS, t), F32)],
        compiler_params=_params("parallel"),
        name="mlstm_proj",
    )(x, g, w_all, w_kt, w_gt, b_g)


def _mlstm_kernel(q_ref, kt_ref, v_ref, o_ref, g_ref, gh_ref, y_ref, c_ref, m_ref):
    L = q_ref.shape[0]

    @pl.when(pl.program_id(1) == 0)
    def _():
        c_ref[...] = jnp.zeros_like(c_ref)
        m_ref[...] = jnp.zeros_like(m_ref)

    jj = lax.broadcasted_iota(jnp.int32, (L, L), 0)
    ss = lax.broadcasted_iota(jnp.int32, (L, L), 1)
    causal = ss <= jj
    upper = (jj <= ss).astype(F32)

    gates = g_ref[...]
    cums = jnp.dot(gates, upper, preferred_element_type=F32, precision=lax.Precision.HIGHEST)
    b8 = pltpu.roll(cums, M_HEADS, axis=0)
    r8 = gates - b8
    lane = lax.broadcasted_iota(jnp.int32, r8.shape, 1)
    cm8 = r8
    shift = 1
    while shift < L:
        cm8 = jnp.maximum(cm8, jnp.where(lane >= shift, pltpu.roll(cm8, shift, axis=1), -jnp.inf))
        shift *= 2
    stack = jnp.concatenate([cm8, b8, jnp.zeros((LANES - 16, L), F32)], axis=0)
    cols = stack.T
    lane1 = lax.broadcasted_iota(jnp.int32, (1, LANES), 1)
    m_prev = [m_ref[h] for h in range(M_HEADS)]
    m_lanes = sum(jnp.where(lane1 == h, m_prev[h], 0.0) for h in range(M_HEADS))
    m_cols = jnp.maximum(cols, m_lanes)
    b_cols = pltpu.roll(cols, LANES - 8, axis=1)
    is_head = lane1 < M_HEADS
    s_inter = jnp.where(is_head, jnp.exp(m_lanes - m_cols), 0.0).astype(BF16)
    floor = jnp.where(is_head, jnp.exp(-(b_cols + m_cols)), 0.0).astype(BF16)
    spread = (lax.broadcasted_iota(jnp.int32, (LANES, M_HEADS * LANES), 0)
              == lax.broadcasted_iota(jnp.int32, (LANES, M_HEADS * LANES), 1) // LANES
              ).astype(BF16)
    s_inter = jnp.dot(s_inter, spread, preferred_element_type=F32)
    floor = jnp.dot(floor, spread, preferred_element_type=F32)
    ones_blk = jnp.ones((L, LANES), BF16)
    mean_blk = jnp.full((M_V_DIM, LANES), 1.0 / M_V_DIM, BF16)

    for h in range(M_HEADS):
        lanes_h = slice(h * LANES, (h + 1) * LANES)
        q = q_ref[:, lanes_h]
        kt = kt_ref[lanes_h, :]
        v_ext = jnp.concatenate([v_ref[:, h * M_V_DIM:(h + 1) * M_V_DIM], ones_blk], axis=1)
        c_prev = c_ref[h]
        r_row = r8[h:h + 1]

        w_intra = jnp.exp(jnp.where(causal, r_row, -jnp.inf) - m_cols[:, h:h + 1])
        s_qk = jnp.dot(q, kt, preferred_element_type=F32) * w_intra
        s_in = s_inter[:, lanes_h]
        ext = (jnp.concatenate([s_in, s_in, s_in], axis=1)
               * jnp.dot(q, c_prev.astype(BF16), preferred_element_type=F32)
               + jnp.dot(s_qk.astype(BF16), v_ext, preferred_element_type=F32))
        den = ext[:, M_V_DIM:]
        inv = 1.0 / jnp.maximum(jnp.abs(den), floor[:, lanes_h])
        h_out = ext[:, :M_V_DIM] * jnp.concatenate([inv, inv], axis=1)

        m_last = jnp.maximum(cm8[h:h + 1, L - 1:L], m_prev[h])
        decay = jnp.exp(m_prev[h] - m_last)
        kwt = (kt.astype(F32) * jnp.exp(r_row - m_last)).astype(BF16)
        c_ref[h] = decay * c_prev + jnp.dot(kwt, v_ext, preferred_element_type=F32)
        m_ref[h] = b8[h:h + 1, L - 1:L] + m_last

        sl = slice(h * M_V_DIM, (h + 1) * M_V_DIM)
        ms = jnp.dot((h_out * h_out).astype(BF16), mean_blk, preferred_element_type=F32)
        rs = lax.rsqrt(ms + NORM_EPS)
        og = jax.nn.sigmoid(o_ref[:, sl].astype(F32))
        y_ref[:, sl] = (og * h_out * jnp.concatenate([rs, rs], axis=1)
                        * gh_ref[:, sl]).astype(BF16)


def _mlstm(voq, kt, gates, g_head, batch, seq):
    L = M_CHUNK
    nc = seq // L
    t = batch * seq
    row = lambda b, c: b * nc + c
    return pl.pallas_call(
        _mlstm_kernel,
        grid=(batch, nc),
        in_specs=[pl.BlockSpec((L, M_QK_COLS), lambda b, c: (row(b, c), 2 * D_MODEL // M_QK_COLS)),
                  pl.BlockSpec((M_QK_COLS, L), lambda b, c: (0, row(b, c))),
                  pl.BlockSpec((L, D_MODEL), lambda b, c: (row(b, c), 0)),
                  pl.BlockSpec((L, D_MODEL), lambda b, c: (row(b, c), 1)),
                  pl.BlockSpec((2 * M_HEADS, L), lambda b, c: (0, row(b, c))),
                  pl.BlockSpec((1, D_MODEL), lambda b, c: (0, 0))],
        out_specs=pl.BlockSpec((L, D_MODEL), lambda b, c: (row(b, c), 0)),
        out_shape=jax.ShapeDtypeStruct((t, D_MODEL), BF16),
        scratch_shapes=[pltpu.VMEM((M_HEADS, M_QK_DIM, M_V_DIM + LANES), F32),
                        pltpu.VMEM((M_HEADS, 1, 1), F32)],
        compiler_params=_params("parallel", "arbitrary"),
        name="mlstm_mixer",
    )(voq, kt, voq, voq, gates, g_head)


def _post_kernel(x_ref, y_ref, wo_ref, g1_ref, g2_ref, g3_ref, wu_ref, wd_ref, out_ref, *,
                 y_feature_major):
    z = lax.dot_general(y_ref[...], wo_ref[...], _TN if y_feature_major else _NN,
                        preferred_element_type=F32)
    x1 = x_ref[...] + _rms(z, g1_ref[...])
    hb = _rms(x1, g2_ref[...]).astype(BF16)
    acc = jnp.zeros(x1.shape, F32)
    for f in range(0, D_FF, 1024):
        u = jnp.maximum(jnp.dot(hb, wu_ref[:, f:f + 1024], preferred_element_type=F32), 0.0)
        acc = acc + jnp.dot((u * u).astype(BF16), wd_ref[f:f + 1024, :],
                            preferred_element_type=F32)
    out_ref[...] = x1 + _rms(acc, g3_ref[...])


def _post(x, y, w_out, g1, g2, g3, w_up, w_down, y_feature_major):
    t = x.shape[0]
    tm = ROW_TILE
    y_spec = (pl.BlockSpec((D_MODEL, tm), lambda i: (0, i)) if y_feature_major
              else pl.BlockSpec((tm, D_MODEL), lambda i: (i, 0)))
    return pl.pallas_call(
        functools.partial(_post_kernel, y_feature_major=y_feature_major),
        grid=(t // tm,),
        in_specs=[pl.BlockSpec((tm, D_MODEL), lambda i: (i, 0)),
                  y_spec,
                  _const_spec((D_MODEL, D_MODEL)),
                  _const_spec((1, D_MODEL)), _const_spec((1, D_MODEL)), _const_spec((1, D_MODEL)),
                  _const_spec((D_MODEL, D_FF)),
                  _const_spec((D_FF, D_MODEL))],
        out_specs=pl.BlockSpec((tm, D_MODEL), lambda i: (i, 0)),
        out_shape=jax.ShapeDtypeStruct((t, D_MODEL), F32),
        compiler_params=_params("parallel"),
        name="outproj_mlp",
    )(x, y, w_out, g1, g2, g3, w_up, w_down)


def _kv_proj_kernel(x_ref, g_ref, wk_ref, wvt_ref, cos_ref, sin_ref, k_ref, vt_ref, kn_ref):
    hb = _rms(x_ref[...], g_ref[...]).astype(BF16)
    cos = cos_ref[...]
    sin = sin_ref[...]
    kk = jnp.dot(hb, wk_ref[...], preferred_element_type=F32)
    col = lax.broadcasted_iota(jnp.int32, (D_MODEL, LANES), 0)
    out = lax.broadcasted_iota(jnp.int32, (D_MODEL, LANES), 1)
    pick = (out == 2 * (col >> 7) + ((col >> 5) & 1)).astype(BF16)
    k_sq = jnp.dot((kk * kk).astype(BF16), pick, preferred_element_type=F32)
    kn_ref[0] = jnp.max(k_sq, axis=0, keepdims=True)
    for h in range(A_HEADS):
        sl = slice(h * LANES, (h + 1) * LANES)
        k_ref[:, sl] = _rope(kk[:, sl], cos, sin).astype(BF16)
    vt = lax.dot_general(wvt_ref[...], hb, _NT, preferred_element_type=F32)
    ones = jnp.ones((V_ROWS - A_V_DIM, vt.shape[1]), BF16)
    for h in range(A_HEADS):
        vt_ref[0, 0, h, :A_V_DIM, :] = vt[h * A_V_DIM:(h + 1) * A_V_DIM].astype(BF16)
        vt_ref[0, 0, h, A_V_DIM:, :] = ones


def _kv_proj(x, g, w_k, w_vt, cos, sin, batch, seq):
    t = x.shape[0]
    tm = KV_PROJ_TILE
    per_stored = ATT_KV_TILE // tm
    per_batch = seq // tm
    return pl.pallas_call(
        _kv_proj_kernel,
        grid=(t // tm,),
        in_specs=[pl.BlockSpec((tm, D_MODEL), lambda i: (i, 0)),
                  _const_spec((1, D_MODEL)),
                  _const_spec((D_MODEL, D_MODEL)),
                  _const_spec((D_MODEL, D_MODEL)),
                  pl.BlockSpec((tm, LANES), lambda i: (i, 0)),
                  pl.BlockSpec((tm, LANES), lambda i: (i, 0))],
        out_specs=[pl.BlockSpec((tm, D_MODEL), lambda i: (i, 0)),
                   pl.BlockSpec((1, 1, A_HEADS, V_ROWS, tm),
                                lambda i: (i // per_batch, (i % per_batch) // per_stored, 0, 0,
                                           i % per_stored)),
                   pl.BlockSpec((1, 1, LANES), lambda i: (i, 0, 0))],
        out_shape=[jax.ShapeDtypeStruct((t, D_MODEL), BF16),
                   jax.ShapeDtypeStruct((batch, seq // ATT_KV_TILE, A_HEADS, V_ROWS, ATT_KV_TILE),
                                        BF16),
                   jax.ShapeDtypeStruct((t // tm, 1, LANES), F32)],
        compiler_params=_params("parallel"),
        name="kv_proj",
    )(x, g, w_k, w_vt, cos, sin)


def _q_proj_kernel(x_ref, g_ref, wqt_ref, cos_ref, sin_ref, qt_ref):
    hb = _rms(x_ref[...], g_ref[...]).astype(BF16)
    cos = cos_ref[...]
    sin = sin_ref[...]
    qq = lax.dot_general(wqt_ref[...], hb, _NT, preferred_element_type=F32)
    half = LANES // 2
    for h in range(A_HEADS):
        lo = qq[h * LANES:h * LANES + half]
        hi = qq[h * LANES + half:(h + 1) * LANES]
        qt_ref[h * LANES:h * LANES + half, :] = (lo * cos - hi * sin).astype(BF16)
        qt_ref[h * LANES + half:(h + 1) * LANES, :] = (hi * cos + lo * sin).astype(BF16)


def _q_proj(x, g, w_qt, cos_t, sin_t):
    t = x.shape[0]
    tm = PROJ_TILE
    return pl.pallas_call(
        _q_proj_kernel,
        grid=(t // tm,),
        in_specs=[pl.BlockSpec((tm, D_MODEL), lambda i: (i, 0)),
                  _const_spec((1, D_MODEL)),
                  _const_spec((D_MODEL, D_MODEL)),
                  pl.BlockSpec((A_HEAD_DIM, tm), lambda i: (0, i)),
                  pl.BlockSpec((A_HEAD_DIM, tm), lambda i: (0, i))],
        out_specs=pl.BlockSpec((D_MODEL, tm), lambda i: (0, i)),
        out_shape=jax.ShapeDtypeStruct((D_MODEL, t), BF16),
        compiler_params=_params("parallel"),
        name="q_proj",
    )(x, g, w_qt, cos_t, sin_t)


def _attn_kernel(lam_ref, kn_ref, q_ref, k_ref, vt_ref, g_ref, o_ref, sa_ref, sb_ref, acc_ref, *,
                 lam_init):
    tq = q_ref.shape[1]
    tks = vt_ref.shape[4]
    head = pl.program_id(1)
    i = pl.program_id(2)

    lam = lam_ref[...]
    lam_full = (jnp.exp(jnp.sum(lam[0:1] * lam[1:2], keepdims=True))
                - jnp.exp(jnp.sum(lam[2:3] * lam[3:4], keepdims=True)) + lam_init)

    q_t = q_ref[...].astype(F32)
    row_map = (lax.broadcasted_iota(jnp.int32, q_t.shape, 0) >> 5) & 1
    q_cat = jnp.concatenate([jnp.where(row_map == c, q_t, 0.0) for c in range(2)],
                            axis=1).astype(BF16)

    k_sq = jnp.max(kn_ref[...], axis=0)
    lane = lax.broadcasted_iota(jnp.int32, k_sq.shape, 1)
    q_sq = q_t * q_t
    bound = jnp.concatenate(
        [jnp.sqrt(jnp.sum(jnp.where(row_map == c, q_sq, 0.0), axis=0, keepdims=True)
                  * jnp.max(jnp.where(lane == 2 * head + c, k_sq, 0.0), axis=1, keepdims=True))
         for c in range(2)], axis=1) * BOUND_SLACK

    def causal(key0, s):
        key_i = key0 + lax.broadcasted_iota(jnp.int32, s.shape, 0)
        qry_i = i * tq + (lax.broadcasted_iota(jnp.int32, s.shape, 1) & (tq - 1))
        return jnp.where(key_i <= qry_i, s, -jnp.inf)

    def bounded():
        ratio = tks // tq
        diag = i // ratio

        def start(groups):
            def init():
                rows = groups * tq
                kb = k_ref[pl.ds(pl.multiple_of(diag * tks, tks), rows), :]
                s = jnp.dot(kb, q_cat, preferred_element_type=F32) - bound
                last = causal(diag * tks + rows - tq, s[rows - tq:])
                s = last if groups == 1 else jnp.concatenate([s[:rows - tq], last], axis=0)
                acc_ref[...] = jnp.dot(vt_ref[0, diag, 0, :, :rows], jnp.exp2(s).astype(BF16),
                                       preferred_element_type=F32)
            return init

        def stored_tile(jb):
            kb = k_ref[pl.ds(pl.multiple_of(jb * tks, tks), tks), :]
            s = jnp.dot(kb, q_cat, preferred_element_type=F32) - bound
            return jnp.dot(vt_ref[0, jb, 0], jnp.exp2(s).astype(BF16),
                           preferred_element_type=F32)

        def trip(n_tiles, first):
            def body(jj, carry):
                jb = first + n_tiles * jj
                total = stored_tile(jb)
                for t in range(1, n_tiles):
                    total = total + stored_tile(jb + t)
                acc_ref[...] += total
                return carry
            return body

        def choose(lo, hi):
            if hi - lo == 1:
                return start(lo + 1)
            mid = (lo + hi) // 2
            return lambda: lax.cond(i % ratio < mid, choose(lo, mid), choose(mid, hi))

        choose(0, ratio)()
        done = 0
        for n_tiles in ATT_TRIP_TILES:
            n_trips = (diag - done) // n_tiles
            lax.fori_loop(0, n_trips, trip(n_tiles, done), 0)
            done = done + n_trips * n_tiles

    def online():
        tk = tks // 2
        n_full = (i * tq) // tk

        def scores(j, s_ref):
            kb = k_ref[pl.ds(pl.multiple_of(j * tk, tk), tk), :]
            s_ref[...] = jnp.dot(kb, q_cat, preferred_element_type=F32)

        def softmax_pv(j, half, s_ref, m_old, diagonal):
            s = s_ref[...]
            if diagonal:
                s = causal(j * tk, s)
            m_new = jnp.maximum(m_old, jnp.max(s, axis=0, keepdims=True))
            alpha = jnp.exp2(m_old - m_new)
            p = jnp.exp2(s - m_new).astype(BF16)
            acc_ref[...] = alpha * acc_ref[...] + jnp.dot(
                vt_ref[0, j // 2, 0, :, half * tk:(half + 1) * tk], p, preferred_element_type=F32)
            return m_new

        acc_ref[...] = jnp.zeros_like(acc_ref)
        scores(0, sa_ref)

        def pair(jj, m_run):
            j = 2 * jj
            scores(j + 1, sb_ref)
            m_run = softmax_pv(j, 0, sa_ref, m_run, False)
            scores(j + 2, sa_ref)
            return softmax_pv(j + 1, 1, sb_ref, m_run, False)

        m_run = lax.fori_loop(0, n_full // 2, pair, jnp.full((1, 2 * tq), -jnp.inf, F32))

        def tail_odd(m_old):
            scores(n_full, sb_ref)
            m_mid = softmax_pv(n_full - 1, 0, sa_ref, m_old, False)
            return softmax_pv(n_full, 1, sb_ref, m_mid, True)

        def tail_even(m_old):
            return softmax_pv(n_full, 0, sa_ref, m_old, True)

        lax.cond(n_full % 2 == 1, tail_odd, tail_even, m_run)

    lax.cond(jnp.max(bound) <= BOUND_MAX, bounded, online)

    acc = acc_ref[...]
    inv_l = 1.0 / acc[A_V_DIM:A_V_DIM + 1]
    o = acc[:A_V_DIM, :tq] * inv_l[:, :tq] - lam_full * (acc[:A_V_DIM, tq:] * inv_l[:, tq:])
    ms = jnp.mean(o * o, axis=0, keepdims=True)
    o = o * lax.rsqrt(ms + NORM_EPS) * g_ref[...] * (1.0 - lam_init)
    o_ref[...] = o.astype(BF16)


def _attention(q, k, vt, k_norms, lam, g_col, batch, seq, lam_init):
    tq, tk = ATT_TQ, ATT_KV_TILE
    assert tk % (2 * tq) == 0 and tq & (tq - 1) == 0
    nq = seq // tq
    nk = seq // tk
    t = batch * seq
    return pl.pallas_call(
        functools.partial(_attn_kernel, lam_init=lam_init),
        grid=(batch, A_HEADS, nq),
        in_specs=[pl.BlockSpec((4, A_HEAD_DIM), lambda b, h, i: (0, 0)),
                  pl.BlockSpec((seq // KV_PROJ_TILE, 1, LANES), lambda b, h, i: (b, 0, 0)),
                  pl.BlockSpec((LANES, tq), lambda b, h, i: (h, b * nq + i)),
                  pl.BlockSpec((seq, LANES), lambda b, h, i: (b, h)),
                  pl.BlockSpec((1, nk, 1, V_ROWS, tk), lambda b, h, i: (b, 0, h, 0, 0)),
                  pl.BlockSpec((A_V_DIM, 1), lambda b, h, i: (h, 0))],
        out_specs=pl.BlockSpec((A_V_DIM, tq), lambda b, h, i: (h, b * nq + i)),
        out_shape=jax.ShapeDtypeStruct((D_MODEL, t), BF16),
        scratch_shapes=[pltpu.VMEM((tk // 2, 2 * tq), F32),
                        pltpu.VMEM((tk // 2, 2 * tq), F32),
                        pltpu.VMEM((V_ROWS, 2 * tq), F32)],
        compiler_params=_params("parallel", "parallel", "arbitrary"),
        name="diff_attention",
    )(lam, k_norms, q, k, vt, g_col)


def _head_perm(w):
    half = A_HEAD_DIM // 2
    w = w.reshape(w.shape[0], A_HEADS, 2, 2, half)
    return w.transpose(0, 1, 3, 2, 4).reshape(w.shape[0], D_MODEL)


def kernel(x, positions, norm_g, a_w_in, a_b_gates, a_g_head, a_w_out, kv_norm_g, w_kv,
           b_w_q, b_lam, b_g_head, b_w_out, mlp_w_up, mlp_w_down):
    batch, seq, _ = x.shape
    t = batch * seq
    depth = norm_g.shape[0]
    n_a = a_w_in.shape[0]
    xf = x.reshape(t, D_MODEL)
    row = lambda v: v.reshape(1, -1).astype(F32)

    cos, sin, cos_t, sin_t = _rope_tables(positions.reshape(1, t))
    k_sh = vt_sh = k_norms = None
    for layer in range(depth):
        g = norm_g[layer]
        if layer < n_a:
            w_in = a_w_in[layer]
            vo_end = 2 * M_QK_COLS + 2 * D_MODEL
            w_kt = (w_in[:, M_QK_COLS:2 * M_QK_COLS] * M_QK_DIM ** -0.5).T.astype(BF16)
            w_gt = w_in[:, vo_end:].T.astype(BF16)
            voq, kt, gates = _a_proj(xf, row(g[0]), w_in.astype(BF16), w_kt, w_gt,
                                     a_b_gates[layer].reshape(-1, 1).astype(F32))
            y = _mlstm(voq, kt, gates, row(a_g_head[layer]), batch, seq)
            w_out = a_w_out[layer]
        else:
            j = layer - n_a
            if j == 0:
                w_k = _head_perm(w_kv[:, :D_MODEL]).astype(BF16)
                w_vt = w_kv[:, D_MODEL:].T.astype(BF16)
                k_sh, vt_sh, k_norms = _kv_proj(xf, row(kv_norm_g), w_k, w_vt, cos, sin,
                                                batch, seq)
            lam_init = 0.8 - 0.6 * math.exp(-0.3 * layer)
            w_qt = (_head_perm(b_w_q[j]) * (A_HEAD_DIM ** -0.5 * math.log2(math.e))
                    ).T.astype(BF16)
            q_t = _q_proj(xf, row(g[0]), w_qt, cos_t, sin_t)
            y = _attention(q_t, k_sh, vt_sh, k_norms, b_lam[j].astype(F32),
                           b_g_head[j].reshape(-1, 1).astype(F32), batch, seq, lam_init)
            w_out = b_w_out[j]
        xf = _post(xf, y, w_out.astype(BF16), row(g[1]), row(g[2]), row(g[3]),
                   mlp_w_up[layer].astype(BF16), mlp_w_down[layer].astype(BF16),
                   y_feature_major=layer >= n_a)
    return xf.reshape(batch, seq, D_MODEL)
```

```python
import functools
import math

import jax
import jax.numpy as jnp
from jax import lax
from jax.experimental import pallas as pl
from jax.experimental.pallas import tpu as pltpu

F32 = jnp.float32
BF16 = jnp.bfloat16

D_MODEL = 1024
D_FF = 4 * D_MODEL
NORM_EPS = 1e-6

M_HEADS = 4
M_V_DIM = D_MODEL // M_HEADS
M_QK_DIM = M_V_DIM // 2
M_QK_COLS = M_HEADS * M_QK_DIM
GATE_SOFTCAP = 15.0
M_CHUNK = 512

A_HEADS = 8
A_HEAD_DIM = D_MODEL // (2 * A_HEADS)
A_V_DIM = 2 * A_HEAD_DIM
ROPE_THETA = 10000.0

LANES = 128
VMEM_LIMIT_BYTES = 56 * 1024 * 1024

ROW_TILE = 512
ATT_TQ = 512
ATT_KV_TILE = 4 * ATT_TQ
PROJ_TILE = 1024
KV_PROJ_TILE = 1024
ATT_TRIP_TILES = (4, 2, 1)
BOUND_MAX = 40.0
BOUND_SLACK = 1.03
V_ROWS = A_V_DIM + 16

_NN = (((1,), (0,)), ((), ()))
_NT = (((1,), (1,)), ((), ()))
_TN = (((0,), (0,)), ((), ()))


def _params(*sem):
    return pltpu.CompilerParams(dimension_semantics=sem, vmem_limit_bytes=VMEM_LIMIT_BYTES)


def _rms(x, g):
    ms = jnp.mean(x * x, axis=-1, keepdims=True)
    return x * lax.rsqrt(ms + NORM_EPS) * g


def _const_spec(shape):
    nd = len(shape)
    return pl.BlockSpec(shape, lambda *_: (0,) * nd, pipeline_mode=pl.Buffered(1))


def _rope_table_kernel(pos_ref, inv_ref, sign_ref, cos_ref, sin_ref, cost_ref, sint_ref):
    ang = inv_ref[...] * pos_ref[...].astype(F32)
    cos32 = jnp.cos(ang)
    sin32 = jnp.sin(ang)
    cost_ref[...] = jnp.concatenate([cos32, cos32], axis=0)
    sint_ref[...] = jnp.concatenate([sin32, sin32], axis=0)
    cos_ref[...] = jnp.concatenate([cos32] * 4, axis=0).T
    sin_ref[...] = jnp.concatenate([sin32] * 4, axis=0).T * sign_ref[...]


def _rope_tables(pos_row):
    t = pos_row.shape[1]
    half = A_HEAD_DIM // 2
    inv = 1.0 / (ROPE_THETA ** (jnp.arange(0, A_HEAD_DIM, 2, dtype=F32) / A_HEAD_DIM))
    sign = jnp.where(jnp.arange(LANES) < LANES // 2, -1.0, 1.0).astype(F32).reshape(1, LANES)
    tm = ROW_TILE
    return pl.pallas_call(
        _rope_table_kernel,
        grid=(t // tm,),
        in_specs=[pl.BlockSpec((1, tm), lambda i: (0, i)),
                  _const_spec((half, 1)), _const_spec((1, LANES))],
        out_specs=[pl.BlockSpec((tm, LANES), lambda i: (i, 0))] * 2
        + [pl.BlockSpec((A_HEAD_DIM, tm), lambda i: (0, i))] * 2,
        out_shape=[jax.ShapeDtypeStruct((t, LANES), F32)] * 2
        + [jax.ShapeDtypeStruct((A_HEAD_DIM, t), F32)] * 2,
        compiler_params=_params("parallel"),
        name="rope_tables",
    )(pos_row, inv.reshape(half, 1), sign)


def _rope(t, cos, sin):
    return t * cos + pltpu.roll(t, LANES // 2, axis=1) * sin


def _a_proj_kernel(x_ref, g_ref, w_ref, wkt_ref, wgt_ref, bg_ref, voq_ref, kt_ref, gates_ref):
    hb = _rms(x_ref[...], g_ref[...]).astype(BF16)
    n_out = voq_ref.shape[1]
    for n in range(0, n_out, 512):
        voq_ref[:, n:n + 512] = jnp.dot(
            hb, w_ref[:, n:n + 512], preferred_element_type=F32).astype(BF16)
    kt_ref[...] = lax.dot_general(wkt_ref[...], hb, _NT,
                                  preferred_element_type=F32).astype(BF16)
    gr = lax.dot_general(wgt_ref[...], hb, _NT, preferred_element_type=F32) + bg_ref[...]
    gr = GATE_SOFTCAP * jnp.tanh(gr / GATE_SOFTCAP)
    log_sig = jnp.minimum(gr, 0.0) - jnp.log1p(jnp.exp(-jnp.abs(gr)))
    row = lax.broadcasted_iota(jnp.int32, gr.shape, 0)
    gates = jnp.where(row < M_HEADS, gr, log_sig)
    L = M_CHUNK
    jj = lax.broadcasted_iota(jnp.int32, (L, L), 0)
    ss = lax.broadcasted_iota(jnp.int32, (L, L), 1)
    upper = (jj <= ss).astype(F32)
    lane = lax.broadcasted_iota(jnp.int32, (2 * M_HEADS, L), 1)
    for c in range(0, gates.shape[1], L):
        g = gates[:, c:c + L]
        cums = jnp.dot(g, upper, preferred_element_type=F32, precision=lax.Precision.HIGHEST)
        b8 = pltpu.roll(cums, M_HEADS, axis=0)
        r8 = g - b8
        cm8 = r8
        shift = 1
        while shift < L:
            cm8 = jnp.maximum(cm8, jnp.where(lane >= shift, pltpu.roll(cm8, shift, axis=1),
                                             -jnp.inf))
            shift *= 2
        gates_ref[:, c:c + L] = jnp.concatenate([cm8, b8, r8], axis=0)


def _a_proj(x, g, w_voq, w_kt, w_gt, b_g):
    t = x.shape[0]
    tm = PROJ_TILE
    n_out = w_voq.shape[1]
    return pl.pallas_call(
        _a_proj_kernel,
        grid=(t // tm,),
        in_specs=[pl.BlockSpec((tm, D_MODEL), lambda i: (i, 0)),
                  _const_spec((1, D_MODEL)),
                  _const_spec((D_MODEL, n_out)),
                  _const_spec((M_QK_COLS, D_MODEL)),
                  _const_spec((2 * M_HEADS, D_MODEL)),
                  _const_spec((2 * M_HEADS, 1))],
        out_specs=[pl.BlockSpec((tm, n_out), lambda i: (i, 0)),
                   pl.BlockSpec((M_QK_COLS, tm), lambda i: (0, i)),
                   pl.BlockSpec((6 * M_HEADS, tm), lambda i: (0, i))],
        out_shape=[jax.ShapeDtypeStruct((t, n_out), BF16),
                   jax.ShapeDtypeStruct((M_QK_COLS, t), BF16),
                   jax.ShapeDtypeStruct((6 * M_HEADS, t), F32)],
        compiler_params=_params("parallel"),
        name="mlstm_proj",
    )(x, g, w_voq, w_kt, w_gt, b_g)


def _mlstm_kernel(q_ref, kt_ref, v_ref, o_ref, g_ref, gh_ref, y_ref, c_ref, m_ref):
    L = q_ref.shape[0]

    @pl.when(pl.program_id(1) == 0)
    def _():
        c_ref[...] = jnp.zeros_like(c_ref)
        m_ref[...] = jnp.zeros_like(m_ref)

    jj = lax.broadcasted_iota(jnp.int32, (L, L), 0)
    ss = lax.broadcasted_iota(jnp.int32, (L, L), 1)
    causal = ss <= jj

    cm8 = g_ref[0:8]
    b8 = g_ref[8:16]
    r8 = g_ref[16:24]
    stack = jnp.concatenate([cm8, b8, jnp.zeros((LANES - 16, L), F32)], axis=0)
    cols = stack.T
    lane1 = lax.broadcasted_iota(jnp.int32, (1, LANES), 1)
    m_prev = [m_ref[h] for h in range(M_HEADS)]
    m_lanes = sum(jnp.where(lane1 == h, m_prev[h], 0.0) for h in range(M_HEADS))
    m_cols = jnp.maximum(cols, m_lanes)
    b_cols = pltpu.roll(cols, LANES - 8, axis=1)
    is_head = lane1 < M_HEADS
    s_inter = jnp.where(is_head, jnp.exp(m_lanes - m_cols), 0.0).astype(BF16)
    floor = jnp.where(is_head, jnp.exp(-(b_cols + m_cols)), 0.0).astype(BF16)
    spread = (lax.broadcasted_iota(jnp.int32, (LANES, M_HEADS * LANES), 0)
              == lax.broadcasted_iota(jnp.int32, (LANES, M_HEADS * LANES), 1) // LANES
              ).astype(BF16)
    s_inter = jnp.dot(s_inter, spread, preferred_element_type=F32)
    floor = jnp.dot(floor, spread, preferred_element_type=F32)
    ones_blk = jnp.ones((L, LANES), BF16)
    mean_blk = jnp.full((M_V_DIM, LANES), 1.0 / M_V_DIM, BF16)

    for h in range(M_HEADS):
        lanes_h = slice(h * LANES, (h + 1) * LANES)
        q = q_ref[:, lanes_h]
        kt = kt_ref[lanes_h, :]
        v_ext = jnp.concatenate([v_ref[:, h * M_V_DIM:(h + 1) * M_V_DIM], ones_blk], axis=1)
        c_prev = c_ref[h]
        r_row = r8[h:h + 1]

        w_intra = jnp.exp(jnp.where(causal, r_row, -jnp.inf) - m_cols[:, h:h + 1])
        s_qk = jnp.dot(q, kt, preferred_element_type=F32) * w_intra
        s_in = s_inter[:, lanes_h]
        ext = (jnp.concatenate([s_in, s_in, s_in], axis=1)
               * jnp.dot(q, c_prev.astype(BF16), preferred_element_type=F32)
               + jnp.dot(s_qk.astype(BF16), v_ext, preferred_element_type=F32))
        den = ext[:, M_V_DIM:]
        inv = 1.0 / jnp.maximum(jnp.abs(den), floor[:, lanes_h])
        h_out = ext[:, :M_V_DIM] * jnp.concatenate([inv, inv], axis=1)

        m_last = jnp.maximum(cm8[h:h + 1, L - 1:L], m_prev[h])
        decay = jnp.exp(m_prev[h] - m_last)
        kwt = (kt.astype(F32) * jnp.exp(r_row - m_last)).astype(BF16)
        c_ref[h] = decay * c_prev + jnp.dot(kwt, v_ext, preferred_element_type=F32)
        m_ref[h] = b8[h:h + 1, L - 1:L] + m_last

        sl = slice(h * M_V_DIM, (h + 1) * M_V_DIM)
        ms = jnp.dot((h_out * h_out).astype(BF16), mean_blk, preferred_element_type=F32)
        rs = lax.rsqrt(ms + NORM_EPS)
        og = jax.nn.sigmoid(o_ref[:, sl].astype(F32))
        y_ref[:, sl] = (og * h_out * jnp.concatenate([rs, rs], axis=1)
                        * gh_ref[:, sl]).astype(BF16)


def _mlstm(voq, kt, gates, g_head, batch, seq):
    L = M_CHUNK
    nc = seq // L
    t = batch * seq
    row = lambda b, c: b * nc + c
    return pl.pallas_call(
        _mlstm_kernel,
        grid=(batch, nc),
        in_specs=[pl.BlockSpec((L, M_QK_COLS), lambda b, c: (row(b, c), 2 * D_MODEL // M_QK_COLS)),
                  pl.BlockSpec((M_QK_COLS, L), lambda b, c: (0, row(b, c))),
                  pl.BlockSpec((L, D_MODEL), lambda b, c: (row(b, c), 0)),
                  pl.BlockSpec((L, D_MODEL), lambda b, c: (row(b, c), 1)),
                  pl.BlockSpec((6 * M_HEADS, L), lambda b, c: (0, row(b, c))),
                  pl.BlockSpec((1, D_MODEL), lambda b, c: (0, 0))],
        out_specs=pl.BlockSpec((L, D_MODEL), lambda b, c: (row(b, c), 0)),
        out_shape=jax.ShapeDtypeStruct((t, D_MODEL), BF16),
        scratch_shapes=[pltpu.VMEM((M_HEADS, M_QK_DIM, M_V_DIM + LANES), F32),
                        pltpu.VMEM((M_HEADS, 1, 1), F32)],
        compiler_params=_params("parallel", "arbitrary"),
        name="mlstm_mixer",
    )(voq, kt, voq, voq, gates, g_head)


def _post_kernel(x_ref, y_ref, wo_ref, g1_ref, g2_ref, g3_ref, wu_ref, wd_ref, out_ref, *,
                 y_feature_major):
    z = lax.dot_general(y_ref[...], wo_ref[...], _TN if y_feature_major else _NN,
                        preferred_element_type=F32)
    x1 = x_ref[...] + _rms(z, g1_ref[...])
    hb = _rms(x1, g2_ref[...]).astype(BF16)
    acc = jnp.zeros(x1.shape, F32)
    for f in range(0, D_FF, 1024):
        u = jnp.maximum(jnp.dot(hb, wu_ref[:, f:f + 1024], preferred_element_type=F32), 0.0)
        acc = acc + jnp.dot((u * u).astype(BF16), wd_ref[f:f + 1024, :],
                            preferred_element_type=F32)
    out_ref[...] = x1 + _rms(acc, g3_ref[...])


def _post(x, y, w_out, g1, g2, g3, w_up, w_down, y_feature_major):
    t = x.shape[0]
    tm = ROW_TILE
    y_spec = (pl.BlockSpec((D_MODEL, tm), lambda i: (0, i)) if y_feature_major
              else pl.BlockSpec((tm, D_MODEL), lambda i: (i, 0)))
    return pl.pallas_call(
        functools.partial(_post_kernel, y_feature_major=y_feature_major),
        grid=(t // tm,),
        in_specs=[pl.BlockSpec((tm, D_MODEL), lambda i: (i, 0)),
                  y_spec,
                  _const_spec((D_MODEL, D_MODEL)),
                  _const_spec((1, D_MODEL)), _const_spec((1, D_MODEL)), _const_spec((1, D_MODEL)),
                  _const_spec((D_MODEL, D_FF)),
                  _const_spec((D_FF, D_MODEL))],
        out_specs=pl.BlockSpec((tm, D_MODEL), lambda i: (i, 0)),
        out_shape=jax.ShapeDtypeStruct((t, D_MODEL), F32),
        compiler_params=_params("parallel"),
        name="outproj_mlp",
    )(x, y, w_out, g1, g2, g3, w_up, w_down)


def _kv_proj_kernel(x_ref, g_ref, wk_ref, wvt_ref, cos_ref, sin_ref, k_ref, vt_ref, kn_ref):
    hb = _rms(x_ref[...], g_ref[...]).astype(BF16)
    cos = cos_ref[...]
    sin = sin_ref[...]
    kk = jnp.dot(hb, wk_ref[...], preferred_element_type=F32)
    col = lax.broadcasted_iota(jnp.int32, (D_MODEL, LANES), 0)
    out = lax.broadcasted_iota(jnp.int32, (D_MODEL, LANES), 1)
    pick = (out == 2 * (col >> 7) + ((col >> 5) & 1)).astype(BF16)
    k_sq = jnp.dot((kk * kk).astype(BF16), pick, preferred_element_type=F32)
    kn_ref[0] = jnp.max(k_sq, axis=0, keepdims=True)
    for h in range(A_HEADS):
        sl = slice(h * LANES, (h + 1) * LANES)
        k_ref[:, sl] = _rope(kk[:, sl], cos, sin).astype(BF16)
    vt = lax.dot_general(wvt_ref[...], hb, _NT, preferred_element_type=F32)
    ones = jnp.ones((V_ROWS - A_V_DIM, vt.shape[1]), BF16)
    for h in range(A_HEADS):
        vt_ref[0, 0, h, :A_V_DIM, :] = vt[h * A_V_DIM:(h + 1) * A_V_DIM].astype(BF16)
        vt_ref[0, 0, h, A_V_DIM:, :] = ones


def _kv_proj(x, g, w_k, w_vt, cos, sin, batch, seq):
    t = x.shape[0]
    tm = KV_PROJ_TILE
    per_stored = ATT_KV_TILE // tm
    per_batch = seq // tm
    return pl.pallas_call(
        _kv_proj_kernel,
        grid=(t // tm,),
        in_specs=[pl.BlockSpec((tm, D_MODEL), lambda i: (i, 0)),
                  _const_spec((1, D_MODEL)),
                  _const_spec((D_MODEL, D_MODEL)),
                  _const_spec((D_MODEL, D_MODEL)),
                  pl.BlockSpec((tm, LANES), lambda i: (i, 0)),
                  pl.BlockSpec((tm, LANES), lambda i: (i, 0))],
        out_specs=[pl.BlockSpec((tm, D_MODEL), lambda i: (i, 0)),
                   pl.BlockSpec((1, 1, A_HEADS, V_ROWS, tm),
                                lambda i: (i // per_batch, (i % per_batch) // per_stored, 0, 0,
                                           i % per_stored)),
                   pl.BlockSpec((1, 1, LANES), lambda i: (i, 0, 0))],
        out_shape=[jax.ShapeDtypeStruct((t, D_MODEL), BF16),
                   jax.ShapeDtypeStruct((batch, seq // ATT_KV_TILE, A_HEADS, V_ROWS, ATT_KV_TILE),
                                        BF16),
                   jax.ShapeDtypeStruct((t // tm, 1, LANES), F32)],
        compiler_params=_params("parallel"),
        name="kv_proj",
    )(x, g, w_k, w_vt, cos, sin)


def _q_proj_kernel(x_ref, g_ref, wqt_ref, cos_ref, sin_ref, qt_ref):
    hb = _rms(x_ref[...], g_ref[...]).astype(BF16)
    cos = cos_ref[...]
    sin = sin_ref[...]
    qq = lax.dot_general(wqt_ref[...], hb, _NT, preferred_element_type=F32)
    half = LANES // 2
    for h in range(A_HEADS):
        lo = qq[h * LANES:h * LANES + half]
        hi = qq[h * LANES + half:(h + 1) * LANES]
        qt_ref[h * LANES:h * LANES + half, :] = (lo * cos - hi * sin).astype(BF16)
        qt_ref[h * LANES + half:(h + 1) * LANES, :] = (hi * cos + lo * sin).astype(BF16)


def _q_proj(x, g, w_qt, cos_t, sin_t):
    t = x.shape[0]
    tm = PROJ_TILE
    return pl.pallas_call(
        _q_proj_kernel,
        grid=(t // tm,),
        in_specs=[pl.BlockSpec((tm, D_MODEL), lambda i: (i, 0)),
                  _const_spec((1, D_MODEL)),
                  _const_spec((D_MODEL, D_MODEL)),
                  pl.BlockSpec((A_HEAD_DIM, tm), lambda i: (0, i)),
                  pl.BlockSpec((A_HEAD_DIM, tm), lambda i: (0, i))],
        out_specs=pl.BlockSpec((D_MODEL, tm), lambda i: (0, i)),
        out_shape=jax.ShapeDtypeStruct((D_MODEL, t), BF16),
        compiler_params=_params("parallel"),
        name="q_proj",
    )(x, g, w_qt, cos_t, sin_t)


def _attn_kernel(lam_ref, kn_ref, q_ref, k_ref, vt_ref, g_ref, o_ref, sa_ref, sb_ref, acc_ref, *,
                 lam_init):
    tq = q_ref.shape[1]
    tks = vt_ref.shape[4]
    head = pl.program_id(1)
    i = pl.program_id(2)

    lam = lam_ref[...]
    lam_full = (jnp.exp(jnp.sum(lam[0:1] * lam[1:2], keepdims=True))
                - jnp.exp(jnp.sum(lam[2:3] * lam[3:4], keepdims=True)) + lam_init)

    q_t = q_ref[...].astype(F32)
    row_map = (lax.broadcasted_iota(jnp.int32, q_t.shape, 0) >> 5) & 1
    q_cat = jnp.concatenate([jnp.where(row_map == c, q_t, 0.0) for c in range(2)],
                            axis=1).astype(BF16)

    k_sq = jnp.max(kn_ref[...], axis=0)
    lane = lax.broadcasted_iota(jnp.int32, k_sq.shape, 1)
    q_sq = q_t * q_t
    bound = jnp.concatenate(
        [jnp.sqrt(jnp.sum(jnp.where(row_map == c, q_sq, 0.0), axis=0, keepdims=True)
                  * jnp.max(jnp.where(lane == 2 * head + c, k_sq, 0.0), axis=1, keepdims=True))
         for c in range(2)], axis=1) * BOUND_SLACK

    def causal(key0, s):
        key_i = key0 + lax.broadcasted_iota(jnp.int32, s.shape, 0)
        qry_i = i * tq + (lax.broadcasted_iota(jnp.int32, s.shape, 1) & (tq - 1))
        return jnp.where(key_i <= qry_i, s, -jnp.inf)

    def bounded():
        ratio = tks // tq
        diag = i // ratio

        def start(groups):
            def init():
                rows = groups * tq
                kb = k_ref[pl.ds(pl.multiple_of(diag * tks, tks), rows), :]
                s = jnp.dot(kb, q_cat, preferred_element_type=F32) - bound
                last = causal(diag * tks + rows - tq, s[rows - tq:])
                s = last if groups == 1 else jnp.concatenate([s[:rows - tq], last], axis=0)
                acc_ref[...] = jnp.dot(vt_ref[0, diag, 0, :, :rows], jnp.exp2(s).astype(BF16),
                                       preferred_element_type=F32)
            return init

        def stored_tile(jb):
            kb = k_ref[pl.ds(pl.multiple_of(jb * tks, tks), tks), :]
            s = jnp.dot(kb, q_cat, preferred_element_type=F32) - bound
            return jnp.dot(vt_ref[0, jb, 0], jnp.exp2(s).astype(BF16),
                           preferred_element_type=F32)

        def trip(n_tiles, first):
            def body(jj, carry):
                jb = first + n_tiles * jj
                total = stored_tile(jb)
                for t in range(1, n_tiles):
                    total = total + stored_tile(jb + t)
                acc_ref[...] += total
                return carry
            return body

        def choose(lo, hi):
            if hi - lo == 1:
                return start(lo + 1)
            mid = (lo + hi) // 2
            return lambda: lax.cond(i % ratio < mid, choose(lo, mid), choose(mid, hi))

        choose(0, ratio)()
        done = 0
        for n_tiles in ATT_TRIP_TILES:
            n_trips = (diag - done) // n_tiles
            lax.fori_loop(0, n_trips, trip(n_tiles, done), 0)
            done = done + n_trips * n_tiles

    def online():
        tk = tks // 2
        n_full = (i * tq) // tk

        def scores(j, s_ref):
            kb = k_ref[pl.ds(pl.multiple_of(j * tk, tk), tk), :]
            s_ref[...] = jnp.dot(kb, q_cat, preferred_element_type=F32)

        def softmax_pv(j, half, s_ref, m_old, diagonal):
            s = s_ref[...]
            if diagonal:
                s = causal(j * tk, s)
            m_new = jnp.maximum(m_old, jnp.max(s, axis=0, keepdims=True))
            alpha = jnp.exp2(m_old - m_new)
            p = jnp.exp2(s - m_new).astype(BF16)
            acc_ref[...] = alpha * acc_ref[...] + jnp.dot(
                vt_ref[0, j // 2, 0, :, half * tk:(half + 1) * tk], p, preferred_element_type=F32)
            return m_new

        acc_ref[...] = jnp.zeros_like(acc_ref)
        scores(0, sa_ref)

        def pair(jj, m_run):
            j = 2 * jj
            scores(j + 1, sb_ref)
            m_run = softmax_pv(j, 0, sa_ref, m_run, False)
            scores(j + 2, sa_ref)
            return softmax_pv(j + 1, 1, sb_ref, m_run, False)

        m_run = lax.fori_loop(0, n_full // 2, pair, jnp.full((1, 2 * tq), -jnp.inf, F32))

        def tail_odd(m_old):
            scores(n_full, sb_ref)
            m_mid = softmax_pv(n_full - 1, 0, sa_ref, m_old, False)
            return softmax_pv(n_full, 1, sb_ref, m_mid, True)

        def tail_even(m_old):
            return softmax_pv(n_full, 0, sa_ref, m_old, True)

        lax.cond(n_full % 2 == 1, tail_odd, tail_even, m_run)

    lax.cond(jnp.max(bound) <= BOUND_MAX, bounded, online)

    acc = acc_ref[...]
    inv_l = 1.0 / acc[A_V_DIM:A_V_DIM + 1]
    o = acc[:A_V_DIM, :tq] * inv_l[:, :tq] - lam_full * (acc[:A_V_DIM, tq:] * inv_l[:, tq:])
    ms = jnp.mean(o * o, axis=0, keepdims=True)
    o = o * lax.rsqrt(ms + NORM_EPS) * g_ref[...] * (1.0 - lam_init)
    o_ref[...] = o.astype(BF16)


def _attention(q, k, vt, k_norms, lam, g_col, batch, seq, lam_init):
    tq, tk = ATT_TQ, ATT_KV_TILE
    assert tk % (2 * tq) == 0 and tq & (tq - 1) == 0
    nq = seq // tq
    nk = seq // tk
    t = batch * seq
    return pl.pallas_call(
        functools.partial(_attn_kernel, lam_init=lam_init),
        grid=(batch, A_HEADS, nq),
        in_specs=[pl.BlockSpec((4, A_HEAD_DIM), lambda b, h, i: (0, 0)),
                  pl.BlockSpec((seq // KV_PROJ_TILE, 1, LANES), lambda b, h, i: (b, 0, 0)),
                  pl.BlockSpec((LANES, tq), lambda b, h, i: (h, b * nq + i)),
                  pl.BlockSpec((seq, LANES), lambda b, h, i: (b, h)),
                  pl.BlockSpec((1, nk, 1, V_ROWS, tk), lambda b, h, i: (b, 0, h, 0, 0)),
                  pl.BlockSpec((A_V_DIM, 1), lambda b, h, i: (h, 0))],
        out_specs=pl.BlockSpec((A_V_DIM, tq), lambda b, h, i: (h, b * nq + i)),
        out_shape=jax.ShapeDtypeStruct((D_MODEL, t), BF16),
        scratch_shapes=[pltpu.VMEM((tk // 2, 2 * tq), F32),
                        pltpu.VMEM((tk // 2, 2 * tq), F32),
                        pltpu.VMEM((V_ROWS, 2 * tq), F32)],
        compiler_params=_params("parallel", "parallel", "arbitrary"),
        name="diff_attention",
    )(lam, k_norms, q, k, vt, g_col)


def _head_perm(w):
    half = A_HEAD_DIM // 2
    w = w.reshape(w.shape[0], A_HEADS, 2, 2, half)
    return w.transpose(0, 1, 3, 2, 4).reshape(w.shape[0], D_MODEL)


def kernel(x, positions, norm_g, a_w_in, a_b_gates, a_g_head, a_w_out, kv_norm_g, w_kv,
           b_w_q, b_lam, b_g_head, b_w_out, mlp_w_up, mlp_w_down):
    batch, seq, _ = x.shape
    t = batch * seq
    depth = norm_g.shape[0]
    n_a = a_w_in.shape[0]
    xf = x.reshape(t, D_MODEL)
    row = lambda v: v.reshape(1, -1).astype(F32)

    cos, sin, cos_t, sin_t = _rope_tables(positions.reshape(1, t))
    k_sh = vt_sh = k_norms = None
    for layer in range(depth):
        g = norm_g[layer]
        if layer < n_a:
            w_in = a_w_in[layer]
            vo_end = 2 * M_QK_COLS + 2 * D_MODEL
            w_voq = jnp.concatenate([w_in[:, 2 * M_QK_COLS:vo_end], w_in[:, :M_QK_COLS]],
                                    axis=1).astype(BF16)
            w_kt = (w_in[:, M_QK_COLS:2 * M_QK_COLS] * M_QK_DIM ** -0.5).T.astype(BF16)
            w_gt = w_in[:, vo_end:].T.astype(BF16)
            voq, kt, gates = _a_proj(xf, row(g[0]), w_voq, w_kt, w_gt,
                                     a_b_gates[layer].reshape(-1, 1).astype(F32))
            y = _mlstm(voq, kt, gates, row(a_g_head[layer]), batch, seq)
            w_out = a_w_out[layer]
        else:
            j = layer - n_a
            if j == 0:
                w_k = _head_perm(w_kv[:, :D_MODEL]).astype(BF16)
                w_vt = w_kv[:, D_MODEL:].T.astype(BF16)
                k_sh, vt_sh, k_norms = _kv_proj(xf, row(kv_norm_g), w_k, w_vt, cos, sin,
                                                batch, seq)
            lam_init = 0.8 - 0.6 * math.exp(-0.3 * layer)
            w_qt = (_head_perm(b_w_q[j]) * (A_HEAD_DIM ** -0.5 * math.log2(math.e))
                    ).T.astype(BF16)
            q_t = _q_proj(xf, row(g[0]), w_qt, cos_t, sin_t)
            y = _attention(q_t, k_sh, vt_sh, k_norms, b_lam[j].astype(F32),
                           b_g_head[j].reshape(-1, 1).astype(F32), batch, seq, lam_init)
            w_out = b_w_out[j]
        xf = _post(xf, y, w_out.astype(BF16), row(g[1]), row(g[2]), row(g[3]),
                   mlp_w_up[layer].astype(BF16), mlp_w_down[layer].astype(BF16),
                   y_feature_major=layer >= n_a)
    return xf.reshape(batch, seq, D_MODEL)
```

```python
import functools
import math

import jax
import jax.numpy as jnp
from jax import lax
from jax.experimental import pallas as pl
from jax.experimental.pallas import tpu as pltpu

F32 = jnp.float32
BF16 = jnp.bfloat16

D_MODEL = 1024
D_FF = 4 * D_MODEL
NORM_EPS = 1e-6

M_HEADS = 4
M_V_DIM = D_MODEL // M_HEADS
M_QK_DIM = M_V_DIM // 2
M_QK_COLS = M_HEADS * M_QK_DIM
GATE_SOFTCAP = 15.0
M_CHUNK = 512

A_HEADS = 8
A_HEAD_DIM = D_MODEL // (2 * A_HEADS)
A_V_DIM = 2 * A_HEAD_DIM
ROPE_THETA = 10000.0

LANES = 128
VMEM_LIMIT_BYTES = 56 * 1024 * 1024

ROW_TILE = 512
ATT_TQ = 512
ATT_KV_TILE = 4 * ATT_TQ
PROJ_TILE = 1024
KV_PROJ_TILE = 1024
ATT_TRIP_TILES = (4, 2, 1)
BOUND_MAX = 40.0
BOUND_SLACK = 1.03
V_ROWS = A_V_DIM + 16

_NN = (((1,), (0,)), ((), ()))
_NT = (((1,), (1,)), ((), ()))
_TN = (((0,), (0,)), ((), ()))


def _params(*sem):
    return pltpu.CompilerParams(dimension_semantics=sem, vmem_limit_bytes=VMEM_LIMIT_BYTES)


def _rms(x, g):
    ms = jnp.mean(x * x, axis=-1, keepdims=True)
    return x * lax.rsqrt(ms + NORM_EPS) * g


def _const_spec(shape):
    nd = len(shape)
    return pl.BlockSpec(shape, lambda *_: (0,) * nd, pipeline_mode=pl.Buffered(1))


def _rope_table_kernel(pos_ref, inv_ref, sign_ref, cos_ref, sin_ref, cost_ref, sint_ref):
    ang = inv_ref[...] * pos_ref[...].astype(F32)
    cos32 = jnp.cos(ang)
    sin32 = jnp.sin(ang)
    cost_ref[...] = jnp.concatenate([cos32, cos32], axis=0)
    sint_ref[...] = jnp.concatenate([sin32, sin32], axis=0)
    cos_ref[...] = jnp.concatenate([cos32] * 4, axis=0).T
    sin_ref[...] = jnp.concatenate([sin32] * 4, axis=0).T * sign_ref[...]


def _rope_tables(pos_row):
    t = pos_row.shape[1]
    half = A_HEAD_DIM // 2
    inv = 1.0 / (ROPE_THETA ** (jnp.arange(0, A_HEAD_DIM, 2, dtype=F32) / A_HEAD_DIM))
    sign = jnp.where(jnp.arange(LANES) < LANES // 2, -1.0, 1.0).astype(F32).reshape(1, LANES)
    tm = ROW_TILE
    return pl.pallas_call(
        _rope_table_kernel,
        grid=(t // tm,),
        in_specs=[pl.BlockSpec((1, tm), lambda i: (0, i)),
                  _const_spec((half, 1)), _const_spec((1, LANES))],
        out_specs=[pl.BlockSpec((tm, LANES), lambda i: (i, 0))] * 2
        + [pl.BlockSpec((A_HEAD_DIM, tm), lambda i: (0, i))] * 2,
        out_shape=[jax.ShapeDtypeStruct((t, LANES), F32)] * 2
        + [jax.ShapeDtypeStruct((A_HEAD_DIM, t), F32)] * 2,
        compiler_params=_params("parallel"),
        name="rope_tables",
    )(pos_row, inv.reshape(half, 1), sign)


def _rope(t, cos, sin):
    return t * cos + pltpu.roll(t, LANES // 2, axis=1) * sin


def _a_proj_kernel(x_ref, g_ref, w_ref, wkt_ref, wgt_ref, bg_ref, voq_ref, kt_ref, gates_ref):
    hb = _rms(x_ref[...], g_ref[...]).astype(BF16)
    n_out = voq_ref.shape[1]
    for n in range(0, n_out, 512):
        voq_ref[:, n:n + 512] = jnp.dot(
            hb, w_ref[:, n:n + 512], preferred_element_type=F32).astype(BF16)
    kt_ref[...] = lax.dot_general(wkt_ref[...], hb, _NT,
                                  preferred_element_type=F32).astype(BF16)
    gr = lax.dot_general(wgt_ref[...], hb, _NT, preferred_element_type=F32) + bg_ref[...]
    gr = GATE_SOFTCAP * jnp.tanh(gr / GATE_SOFTCAP)
    log_sig = jnp.minimum(gr, 0.0) - jnp.log1p(jnp.exp(-jnp.abs(gr)))
    row = lax.broadcasted_iota(jnp.int32, gr.shape, 0)
    gates = jnp.where(row < M_HEADS, gr, log_sig)
    L = M_CHUNK
    jj = lax.broadcasted_iota(jnp.int32, (L, L), 0)
    ss = lax.broadcasted_iota(jnp.int32, (L, L), 1)
    upper = (jj <= ss).astype(F32)
    lane = lax.broadcasted_iota(jnp.int32, (2 * M_HEADS, L), 1)
    for c in range(0, gates.shape[1], L):
        g = gates[:, c:c + L]
        cums = jnp.dot(g, upper, preferred_element_type=F32, precision=lax.Precision.HIGHEST)
        b8 = pltpu.roll(cums, M_HEADS, axis=0)
        r8 = g - b8
        cm8 = r8
        shift = 1
        while shift < L:
            cm8 = jnp.maximum(cm8, jnp.where(lane >= shift, pltpu.roll(cm8, shift, axis=1),
                                             -jnp.inf))
            shift *= 2
        gates_ref[:, c:c + L] = jnp.concatenate([cm8, b8, r8], axis=0)


def _a_proj(x, g, w_voq, w_kt, w_gt, b_g):
    t = x.shape[0]
    tm = PROJ_TILE
    n_out = w_voq.shape[1]
    return pl.pallas_call(
        _a_proj_kernel,
        grid=(t // tm,),
        in_specs=[pl.BlockSpec((tm, D_MODEL), lambda i: (i, 0)),
                  _const_spec((1, D_MODEL)),
                  _const_spec((D_MODEL, n_out)),
                  _const_spec((M_QK_COLS, D_MODEL)),
                  _const_spec((2 * M_HEADS, D_MODEL)),
                  _const_spec((2 * M_HEADS, 1))],
        out_specs=[pl.BlockSpec((tm, n_out), lambda i: (i, 0)),
                   pl.BlockSpec((M_QK_COLS, tm), lambda i: (0, i)),
                   pl.BlockSpec((6 * M_HEADS, tm), lambda i: (0, i))],
        out_shape=[jax.ShapeDtypeStruct((t, n_out), BF16),
                   jax.ShapeDtypeStruct((M_QK_COLS, t), BF16),
                   jax.ShapeDtypeStruct((6 * M_HEADS, t), F32)],
        compiler_params=_params("parallel"),
        name="mlstm_proj",
    )(x, g, w_voq, w_kt, w_gt, b_g)


def _mlstm_kernel(q_ref, kt_ref, v_ref, o_ref, g_ref, gh_ref, y_ref, c_ref, m_ref):
    L = q_ref.shape[0]

    @pl.when(pl.program_id(1) == 0)
    def _():
        c_ref[...] = jnp.zeros_like(c_ref)
        m_ref[...] = jnp.zeros_like(m_ref)

    jj = lax.broadcasted_iota(jnp.int32, (L, L), 0)
    ss = lax.broadcasted_iota(jnp.int32, (L, L), 1)
    causal = ss <= jj

    cm8 = g_ref[0:8]
    b8 = g_ref[8:16]
    r8 = g_ref[16:24]
    stack = jnp.concatenate([cm8, b8, jnp.zeros((LANES - 16, L), F32)], axis=0)
    cols = stack.T
    lane1 = lax.broadcasted_iota(jnp.int32, (1, LANES), 1)
    m_prev = [m_ref[h] for h in range(M_HEADS)]
    m_lanes = sum(jnp.where(lane1 == h, m_prev[h], 0.0) for h in range(M_HEADS))
    m_cols = jnp.maximum(cols, m_lanes)
    b_cols = pltpu.roll(cols, LANES - 8, axis=1)
    is_head = lane1 < M_HEADS
    s_inter = jnp.where(is_head, jnp.exp(m_lanes - m_cols), 0.0).astype(BF16)
    floor = jnp.where(is_head, jnp.exp(-(b_cols + m_cols)), 0.0).astype(BF16)
    spread = (lax.broadcasted_iota(jnp.int32, (LANES, M_HEADS * LANES), 0)
              == lax.broadcasted_iota(jnp.int32, (LANES, M_HEADS * LANES), 1) // LANES
              ).astype(BF16)
    s_inter = jnp.dot(s_inter, spread, preferred_element_type=F32)
    floor = jnp.dot(floor, spread, preferred_element_type=F32)
    ones_blk = jnp.ones((L, LANES), BF16)
    mean_blk = jnp.full((M_V_DIM, LANES), 1.0 / M_V_DIM, BF16)

    for h in range(M_HEADS):
        lanes_h = slice(h * LANES, (h + 1) * LANES)
        q = q_ref[:, lanes_h]
        kt = kt_ref[lanes_h, :]
        v_ext = jnp.concatenate([v_ref[:, h * M_V_DIM:(h + 1) * M_V_DIM], ones_blk], axis=1)
        c_prev = c_ref[h]
        r_row = r8[h:h + 1]

        w_intra = jnp.exp(jnp.where(causal, r_row, -jnp.inf) - m_cols[:, h:h + 1])
        s_qk = jnp.dot(q, kt, preferred_element_type=F32) * w_intra
        s_in = s_inter[:, lanes_h]
        ext = (jnp.concatenate([s_in, s_in, s_in], axis=1)
               * jnp.dot(q, c_prev.astype(BF16), preferred_element_type=F32)
               + jnp.dot(s_qk.astype(BF16), v_ext, preferred_element_type=F32))
        den = ext[:, M_V_DIM:]
        inv = 1.0 / jnp.maximum(jnp.abs(den), floor[:, lanes_h])
        h_out = ext[:, :M_V_DIM] * jnp.concatenate([inv, inv], axis=1)

        m_last = jnp.maximum(cm8[h:h + 1, L - 1:L], m_prev[h])
        decay = jnp.exp(m_prev[h] - m_last)
        kwt = (kt.astype(F32) * jnp.exp(r_row - m_last)).astype(BF16)
        c_ref[h] = decay * c_prev + jnp.dot(kwt, v_ext, preferred_element_type=F32)
        m_ref[h] = b8[h:h + 1, L - 1:L] + m_last

        sl = slice(h * M_V_DIM, (h + 1) * M_V_DIM)
        ms = jnp.dot((h_out * h_out).astype(BF16), mean_blk, preferred_element_type=F32)
        rs = lax.rsqrt(ms + NORM_EPS)
        og = jax.nn.sigmoid(o_ref[:, sl].astype(F32))
        y_ref[:, sl] = (og * h_out * jnp.concatenate([rs, rs], axis=1)
                        * gh_ref[:, sl]).astype(BF16)


def _mlstm(voq, kt, gates, g_head, batch, seq):
    L = M_CHUNK
    nc = seq // L
    t = batch * seq
    row = lambda b, c: b * nc + c
    return pl.pallas_call(
        _mlstm_kernel,
        grid=(batch, nc),
        in_specs=[pl.BlockSpec((L, M_QK_COLS), lambda b, c: (row(b, c), 2 * D_MODEL // M_QK_COLS)),
                  pl.BlockSpec((M_QK_COLS, L), lambda b, c: (0, row(b, c))),
                  pl.BlockSpec((L, D_MODEL), lambda b, c: (row(b, c), 0)),
                  pl.BlockSpec((L, D_MODEL), lambda b, c: (row(b, c), 1)),
                  pl.BlockSpec((6 * M_HEADS, L), lambda b, c: (0, row(b, c))),
                  pl.BlockSpec((1, D_MODEL), lambda b, c: (0, 0))],
        out_specs=pl.BlockSpec((L, D_MODEL), lambda b, c: (row(b, c), 0)),
        out_shape=jax.ShapeDtypeStruct((t, D_MODEL), BF16),
        scratch_shapes=[pltpu.VMEM((M_HEADS, M_QK_DIM, M_V_DIM + LANES), F32),
                        pltpu.VMEM((M_HEADS, 1, 1), F32)],
        compiler_params=_params("parallel", "arbitrary"),
        name="mlstm_mixer",
    )(voq, kt, voq, voq, gates, g_head)


def _post_kernel(x_ref, y_ref, wo_ref, g1_ref, g2_ref, g3_ref, wu_ref, wd_ref, out_ref, *,
                 y_feature_major):
    z = lax.dot_general(y_ref[...], wo_ref[...], _TN if y_feature_major else _NN,
                        preferred_element_type=F32)
    x1 = x_ref[...] + _rms(z, g1_ref[...])
    hb = _rms(x1, g2_ref[...]).astype(BF16)
    acc = jnp.zeros(x1.shape, F32)
    for f in range(0, D_FF, 1024):
        u = jnp.maximum(jnp.dot(hb, wu_ref[:, f:f + 1024], preferred_element_type=F32), 0.0)
        acc = acc + jnp.dot((u * u).astype(BF16), wd_ref[f:f + 1024, :],
                            preferred_element_type=F32)
    out_ref[...] = x1 + _rms(acc, g3_ref[...])


def _post(x, y, w_out, g1, g2, g3, w_up, w_down, y_feature_major):
    t = x.shape[0]
    tm = ROW_TILE
    y_spec = (pl.BlockSpec((D_MODEL, tm), lambda i: (0, i)) if y_feature_major
              else pl.BlockSpec((tm, D_MODEL), lambda i: (i, 0)))
    return pl.pallas_call(
        functools.partial(_post_kernel, y_feature_major=y_feature_major),
        grid=(t // tm,),
        in_specs=[pl.BlockSpec((tm, D_MODEL), lambda i: (i, 0)),
                  y_spec,
                  _const_spec((D_MODEL, D_MODEL)),
                  _const_spec((1, D_MODEL)), _const_spec((1, D_MODEL)), _const_spec((1, D_MODEL)),
                  _const_spec((D_MODEL, D_FF)),
                  _const_spec((D_FF, D_MODEL))],
        out_specs=pl.BlockSpec((tm, D_MODEL), lambda i: (i, 0)),
        out_shape=jax.ShapeDtypeStruct((t, D_MODEL), F32),
        compiler_params=_params("parallel"),
        name="outproj_mlp",
    )(x, y, w_out, g1, g2, g3, w_up, w_down)


def _kv_proj_kernel(x_ref, g_ref, wk_ref, wvt_ref, cos_ref, sin_ref, k_ref, vt_ref, kn_ref):
    hb = _rms(x_ref[...], g_ref[...]).astype(BF16)
    cos = cos_ref[...]
    sin = sin_ref[...]
    kk = jnp.dot(hb, wk_ref[...], preferred_element_type=F32)
    col = lax.broadcasted_iota(jnp.int32, (D_MODEL, LANES), 0)
    out = lax.broadcasted_iota(jnp.int32, (D_MODEL, LANES), 1)
    pick = (out == 2 * (col >> 7) + ((col >> 5) & 1)).astype(BF16)
    k_sq = jnp.dot((kk * kk).astype(BF16), pick, preferred_element_type=F32)
    kn_ref[0] = jnp.max(k_sq, axis=0, keepdims=True)
    for h in range(A_HEADS):
        sl = slice(h * LANES, (h + 1) * LANES)
        k_ref[:, sl] = _rope(kk[:, sl], cos, sin).astype(BF16)
    vt = lax.dot_general(wvt_ref[...], hb, _NT, preferred_element_type=F32)
    ones = jnp.ones((V_ROWS - A_V_DIM, vt.shape[1]), BF16)
    for h in range(A_HEADS):
        vt_ref[0, 0, h, :A_V_DIM, :] = vt[h * A_V_DIM:(h + 1) * A_V_DIM].astype(BF16)
        vt_ref[0, 0, h, A_V_DIM:, :] = ones


def _kv_proj(x, g, w_k, w_vt, cos, sin, batch, seq):
    t = x.shape[0]
    tm = KV_PROJ_TILE
    per_stored = ATT_KV_TILE // tm
    per_batch = seq // tm
    return pl.pallas_call(
        _kv_proj_kernel,
        grid=(t // tm,),
        in_specs=[pl.BlockSpec((tm, D_MODEL), lambda i: (i, 0)),
                  _const_spec((1, D_MODEL)),
                  _const_spec((D_MODEL, D_MODEL)),
                  _const_spec((D_MODEL, D_MODEL)),
                  pl.BlockSpec((tm, LANES), lambda i: (i, 0)),
                  pl.BlockSpec((tm, LANES), lambda i: (i, 0))],
        out_specs=[pl.BlockSpec((tm, D_MODEL), lambda i: (i, 0)),
                   pl.BlockSpec((1, 1, A_HEADS, V_ROWS, tm),
                                lambda i: (i // per_batch, (i % per_batch) // per_stored, 0, 0,
                                           i % per_stored)),
                   pl.BlockSpec((1, 1, LANES), lambda i: (i, 0, 0))],
        out_shape=[jax.ShapeDtypeStruct((t, D_MODEL), BF16),
                   jax.ShapeDtypeStruct((batch, seq // ATT_KV_TILE, A_HEADS, V_ROWS, ATT_KV_TILE),
                                        BF16),
                   jax.ShapeDtypeStruct((t // tm, 1, LANES), F32)],
        compiler_params=_params("parallel"),
        name="kv_proj",
    )(x, g, w_k, w_vt, cos, sin)


def _q_proj_kernel(x_ref, g_ref, wqt_ref, cos_ref, sin_ref, qt_ref):
    hb = _rms(x_ref[...], g_ref[...]).astype(BF16)
    cos = cos_ref[...]
    sin = sin_ref[...]
    qq = lax.dot_general(wqt_ref[...], hb, _NT, preferred_element_type=F32)
    half = LANES // 2
    for h in range(A_HEADS):
        lo = qq[h * LANES:h * LANES + half]
        hi = qq[h * LANES + half:(h + 1) * LANES]
        qt_ref[h * LANES:h * LANES + half, :] = (lo * cos - hi * sin).astype(BF16)
        qt_ref[h * LANES + half:(h + 1) * LANES, :] = (hi * cos + lo * sin).astype(BF16)


def _q_proj(x, g, w_qt, cos_t, sin_t):
    t = x.shape[0]
    tm = PROJ_TILE
    return pl.pallas_call(
        _q_proj_kernel,
        grid=(t // tm,),
        in_specs=[pl.BlockSpec((tm, D_MODEL), lambda i: (i, 0)),
                  _const_spec((1, D_MODEL)),
                  _const_spec((D_MODEL, D_MODEL)),
                  pl.BlockSpec((A_HEAD_DIM, tm), lambda i: (0, i)),
                  pl.BlockSpec((A_HEAD_DIM, tm), lambda i: (0, i))],
        out_specs=pl.BlockSpec((D_MODEL, tm), lambda i: (0, i)),
        out_shape=jax.ShapeDtypeStruct((D_MODEL, t), BF16),
        compiler_params=_params("parallel"),
        name="q_proj",
    )(x, g, w_qt, cos_t, sin_t)


def _attn_kernel(lam_ref, kn_ref, q_ref, k_ref, vt_ref, g_ref, o_ref, sa_ref, sb_ref, acc_ref, *,
                 lam_init):
    tq = q_ref.shape[1]
    tks = vt_ref.shape[4]
    head = pl.program_id(1)
    i = pl.program_id(2)

    lam = lam_ref[...]
    lam_full = (jnp.exp(jnp.sum(lam[0:1] * lam[1:2], keepdims=True))
                - jnp.exp(jnp.sum(lam[2:3] * lam[3:4], keepdims=True)) + lam_init)

    q_t = q_ref[...].astype(F32)
    row_map = (lax.broadcasted_iota(jnp.int32, q_t.shape, 0) >> 5) & 1
    q_cat = jnp.concatenate([jnp.where(row_map == c, q_t, 0.0) for c in range(2)],
                            axis=1).astype(BF16)

    k_sq = jnp.max(kn_ref[...], axis=0)
    lane = lax.broadcasted_iota(jnp.int32, k_sq.shape, 1)
    q_sq = q_t * q_t
    bound = jnp.concatenate(
        [jnp.sqrt(jnp.sum(jnp.where(row_map == c, q_sq, 0.0), axis=0, keepdims=True)
                  * jnp.max(jnp.where(lane == 2 * head + c, k_sq, 0.0), axis=1, keepdims=True))
         for c in range(2)], axis=1) * BOUND_SLACK

    def causal(key0, s):
        key_i = key0 + lax.broadcasted_iota(jnp.int32, s.shape, 0)
        qry_i = i * tq + (lax.broadcasted_iota(jnp.int32, s.shape, 1) & (tq - 1))
        return jnp.where(key_i <= qry_i, s, -jnp.inf)

    def bounded():
        ratio = tks // tq
        diag = i // ratio

        def start(groups):
            def init():
                rows = groups * tq
                kb = k_ref[pl.ds(pl.multiple_of(diag * tks, tks), rows), :]
                s = jnp.dot(kb, q_cat, preferred_element_type=F32) - bound
                last = causal(diag * tks + rows - tq, s[rows - tq:])
                s = last if groups == 1 else jnp.concatenate([s[:rows - tq], last], axis=0)
                acc_ref[...] = jnp.dot(vt_ref[0, diag, 0, :, :rows], jnp.exp2(s).astype(BF16),
                                       preferred_element_type=F32)
            return init

        def stored_tile(jb):
            kb = k_ref[pl.ds(pl.multiple_of(jb * tks, tks), tks), :]
            s = jnp.dot(kb, q_cat, preferred_element_type=F32) - bound
            return jnp.dot(vt_ref[0, jb, 0], jnp.exp2(s).astype(BF16),
                           preferred_element_type=F32)

        def trip(n_tiles, first):
            def body(jj, carry):
                jb = first + n_tiles * jj
                total = stored_tile(jb)
                for t in range(1, n_tiles):
                    total = total + stored_tile(jb + t)
                acc_ref[...] += total
                return carry
            return body

        def choose(lo, hi):
            if hi - lo == 1:
                return start(lo + 1)
            mid = (lo + hi) // 2
            return lambda: lax.cond(i % ratio < mid, choose(lo, mid), choose(mid, hi))

        choose(0, ratio)()
        done = 0
        for n_tiles in ATT_TRIP_TILES:
            n_trips = (diag - done) // n_tiles
            lax.fori_loop(0, n_trips, trip(n_tiles, done), 0)
            done = done + n_trips * n_tiles

    def online():
        tk = tks // 2
        n_full = (i * tq) // tk

        def scores(j, s_ref):
            kb = k_ref[pl.ds(pl.multiple_of(j * tk, tk), tk), :]
            s_ref[...] = jnp.dot(kb, q_cat, preferred_element_type=F32)

        def softmax_pv(j, half, s_ref, m_old, diagonal):
            s = s_ref[...]
            if diagonal:
                s = causal(j * tk, s)
            m_new = jnp.maximum(m_old, jnp.max(s, axis=0, keepdims=True))
            alpha = jnp.exp2(m_old - m_new)
            p = jnp.exp2(s - m_new).astype(BF16)
            acc_ref[...] = alpha * acc_ref[...] + jnp.dot(
                vt_ref[0, j // 2, 0, :, half * tk:(half + 1) * tk], p, preferred_element_type=F32)
            return m_new

        acc_ref[...] = jnp.zeros_like(acc_ref)
        scores(0, sa_ref)

        def pair(jj, m_run):
            j = 2 * jj
            scores(j + 1, sb_ref)
            m_run = softmax_pv(j, 0, sa_ref, m_run, False)
            scores(j + 2, sa_ref)
            return softmax_pv(j + 1, 1, sb_ref, m_run, False)

        m_run = lax.fori_loop(0, n_full // 2, pair, jnp.full((1, 2 * tq), -jnp.inf, F32))

        def tail_odd(m_old):
            scores(n_full, sb_ref)
            m_mid = softmax_pv(n_full - 1, 0, sa_ref, m_old, False)
            return softmax_pv(n_full, 1, sb_ref, m_mid, True)

        def tail_even(m_old):
            return softmax_pv(n_full, 0, sa_ref, m_old, True)

        lax.cond(n_full % 2 == 1, tail_odd, tail_even, m_run)

    lax.cond(jnp.max(bound) <= BOUND_MAX, bounded, online)

    acc = acc_ref[...]
    inv_l = 1.0 / acc[A_V_DIM:A_V_DIM + 1]
    o = acc[:A_V_DIM, :tq] * inv_l[:, :tq] - lam_full * (acc[:A_V_DIM, tq:] * inv_l[:, tq:])
    ms = jnp.mean(o * o, axis=0, keepdims=True)
    o = o * lax.rsqrt(ms + NORM_EPS) * g_ref[...] * (1.0 - lam_init)
    o_ref[...] = o.astype(BF16)


def _attention(q, k, vt, k_norms, lam, g_col, batch, seq, lam_init):
    tq, tk = ATT_TQ, ATT_KV_TILE
    assert tk % (2 * tq) == 0 and tq & (tq - 1) == 0
    nq = seq // tq
    nk = seq // tk
    t = batch * seq
    return pl.pallas_call(
        functools.partial(_attn_kernel, lam_init=lam_init),
        grid=(batch, A_HEADS, nq),
        in_specs=[pl.BlockSpec((4, A_HEAD_DIM), lambda b, h, i: (0, 0)),
                  pl.BlockSpec((seq // KV_PROJ_TILE, 1, LANES), lambda b, h, i: (b, 0, 0)),
                  pl.BlockSpec((LANES, tq), lambda b, h, i: (h, b * nq + i)),
                  pl.BlockSpec((seq, LANES), lambda b, h, i: (b, h),
                               pipeline_mode=pl.Buffered(1)),
                  pl.BlockSpec((1, nk, 1, V_ROWS, tk), lambda b, h, i: (b, 0, h, 0, 0),
                               pipeline_mode=pl.Buffered(1)),
                  pl.BlockSpec((A_V_DIM, 1), lambda b, h, i: (h, 0))],
        out_specs=pl.BlockSpec((A_V_DIM, tq), lambda b, h, i: (h, b * nq + i)),
        out_shape=jax.ShapeDtypeStruct((D_MODEL, t), BF16),
        scratch_shapes=[pltpu.VMEM((tk // 2, 2 * tq), F32),
                        pltpu.VMEM((tk // 2, 2 * tq), F32),
                        pltpu.VMEM((V_ROWS, 2 * tq), F32)],
        compiler_params=_params("parallel", "parallel", "arbitrary"),
        name="diff_attention",
    )(lam, k_norms, q, k, vt, g_col)


def _head_perm(w):
    half = A_HEAD_DIM // 2
    w = w.reshape(w.shape[0], A_HEADS, 2, 2, half)
    return w.transpose(0, 1, 3, 2, 4).reshape(w.shape[0], D_MODEL)


def kernel(x, positions, norm_g, a_w_in, a_b_gates, a_g_head, a_w_out, kv_norm_g, w_kv,
           b_w_q, b_lam, b_g_head, b_w_out, mlp_w_up, mlp_w_down):
    batch, seq, _ = x.shape
    t = batch * seq
    depth = norm_g.shape[0]
    n_a = a_w_in.shape[0]
    xf = x.reshape(t, D_MODEL)
    row = lambda v: v.reshape(1, -1).astype(F32)

    cos, sin, cos_t, sin_t = _rope_tables(positions.reshape(1, t))
    k_sh = vt_sh = k_norms = None
    for layer in range(depth):
        g = norm_g[layer]
        if layer < n_a:
            w_in = a_w_in[layer]
            vo_end = 2 * M_QK_COLS + 2 * D_MODEL
            w_voq = jnp.concatenate([w_in[:, 2 * M_QK_COLS:vo_end], w_in[:, :M_QK_COLS]],
                                    axis=1).astype(BF16)
            w_kt = (w_in[:, M_QK_COLS:2 * M_QK_COLS] * M_QK_DIM ** -0.5).T.astype(BF16)
            w_gt = w_in[:, vo_end:].T.astype(BF16)
            voq, kt, gates = _a_proj(xf, row(g[0]), w_voq, w_kt, w_gt,
                                     a_b_gates[layer].reshape(-1, 1).astype(F32))
            y = _mlstm(voq, kt, gates, row(a_g_head[layer]), batch, seq)
            w_out = a_w_out[layer]
        else:
            j = layer - n_a
            if j == 0:
                w_k = _head_perm(w_kv[:, :D_MODEL]).astype(BF16)
                w_vt = w_kv[:, D_MODEL:].T.astype(BF16)
                k_sh, vt_sh, k_norms = _kv_proj(xf, row(kv_norm_g), w_k, w_vt, cos, sin,
                                                batch, seq)
            lam_init = 0.8 - 0.6 * math.exp(-0.3 * layer)
            w_qt = (_head_perm(b_w_q[j]) * (A_HEAD_DIM ** -0.5 * math.log2(math.e))
                    ).T.astype(BF16)
            q_t = _q_proj(xf, row(g[0]), w_qt, cos_t, sin_t)
            y = _attention(q_t, k_sh, vt_sh, k_norms, b_lam[j].astype(F32),
                           b_g_head[j].reshape(-1, 1).astype(F32), batch, seq, lam_init)
            w_out = b_w_out[j]
        xf = _post(xf, y, w_out.astype(BF16), row(g[1]), row(g[2]), row(g[3]),
                   mlp_w_up[layer].astype(BF16), mlp_w_down[layer].astype(BF16),
                   y_feature_major=layer >= n_a)
    return xf.reshape(batch, seq, D_MODEL)
```
